```python
import math
import jax, jax.numpy as jnp
from jax import lax
import numpy as np

D_MODEL = 1024
BATCH = 8
SEQ = 4096
DEPTH = 1

HEAD_DIM = 64
FOX_HEADS = 8
DIL_HEADS = 8
FOX_W = FOX_HEADS * HEAD_DIM
DIL_W = DIL_HEADS * HEAD_DIM
Q_BLOCK = 128
DILATED_PATTERNS = ((128, 1), (512, 4), (2048, 16))
ROPE_THETA = 500000.0
ROT_DIM = HEAD_DIM // 4
N_GROUPS = 4
EXPERTS_PER_GROUP = 8
N_EXPERTS = N_GROUPS * EXPERTS_PER_GROUP
TOP_K = 2
D_EXPERT = 512
MOE_BLOCK = 256
LN_EPS = 1e-5
NEG = -1e30
DEEPNORM_ALPHA = (2 * DEPTH) ** 0.25
DEEPNORM_BETA = (8 * DEPTH) ** (-0.25)
PROJ_SIZES = (FOX_W, FOX_W, FOX_W, FOX_W, FOX_HEADS, DIL_W, DIL_W, DIL_W)
PROJ_COLS = sum(PROJ_SIZES)

kernel_name = "hybrid_fox_dilated_hmoe_deepnorm_layer"


def layer_norm(x, g, b):
    xf = x.astype(jnp.float32)
    mu = jnp.mean(xf, axis=-1, keepdims=True)
    var = jnp.mean(jnp.square(xf - mu), axis=-1, keepdims=True)
    return ((xf - mu) * lax.rsqrt(var + LN_EPS) * g + b).astype(x.dtype)


def partial_rope(t, positions):
    half = ROT_DIM // 2
    inv_freq = ROPE_THETA ** (-jnp.arange(0, ROT_DIM, 2, dtype=jnp.float32) / ROT_DIM)
    ang = positions.astype(jnp.float32)[..., None] * inv_freq
    cos, sin = jnp.cos(ang)[:, :, None, :], jnp.sin(ang)[:, :, None, :]
    tf = t.astype(jnp.float32)
    x1, x2, rest = tf[..., :half], tf[..., half:ROT_DIM], tf[..., ROT_DIM:]
    out = jnp.concatenate([x1 * cos - x2 * sin, x2 * cos + x1 * sin, rest], axis=-1)
    return out.astype(t.dtype)


def forgetting_attention(q, k, v, log_f):
    B, H, S, hd = q.shape
    nb = S // Q_BLOCK
    cum = jnp.cumsum(log_f, axis=-1)
    qb = q.reshape(B, H, nb, Q_BLOCK, hd).transpose(2, 0, 1, 3, 4)
    cb = cum.reshape(B, H, nb, Q_BLOCK).transpose(2, 0, 1, 3)
    kpos = jnp.arange(S)
    scale = HEAD_DIM ** -0.5

    def block(args):
        qi, ci, n = args
        s = jnp.einsum('bhqd,bhkd->bhqk', qi, k) * scale + ci[..., :, None] - cum[:, :, None, :]
        qpos = n * Q_BLOCK + jnp.arange(Q_BLOCK)
        s = jnp.where(kpos[None, :] <= qpos[:, None], s, NEG)
        p = jax.nn.softmax(s, axis=-1)
        return jnp.einsum('bhqk,bhkd->bhqd', p, v)

    o = lax.map(block, (qb, cb, jnp.arange(nb)))
    return o.transpose(1, 2, 0, 3, 4).reshape(B, H, S, hd)


def dilated_window_attention(q, k, v, window, dilation):
    B, H, S, hd = q.shape
    w = window // dilation
    unit = dilation * w
    Sp = -(-S // unit) * unit
    L = Sp // dilation
    nb = L // w
    pad = ((0, 0), (0, 0), (0, Sp - S), (0, 0))

    def to_res(a):
        a = jnp.pad(a, pad).reshape(B, H, L, dilation, hd).transpose(0, 1, 3, 2, 4)
        return a.reshape(B, H, dilation, nb, w, hd)

    def with_prev(a):
        prev = jnp.pad(a, ((0, 0), (0, 0), (0, 0), (1, 0), (0, 0), (0, 0)))[:, :, :, :nb]
        return jnp.concatenate([prev, a], axis=4)

    qr = to_res(q)
    kr = with_prev(to_res(k))
    vr = with_prev(to_res(v))
    s = jnp.einsum('bhrnqd,bhrnkd->bhrnqk', qr, kr) * (HEAD_DIM ** -0.5)
    n_i = jnp.arange(nb)[:, None, None]
    i_i = jnp.arange(w)[None, :, None]
    j_i = jnp.arange(2 * w)[None, None, :]
    valid = (j_i >= i_i) & (j_i <= i_i + w) & ((n_i > 0) | (j_i >= w))
    s = jnp.where(valid, s, NEG)
    lse = jax.nn.logsumexp(s, axis=-1)
    p = jnp.exp(s - lse[..., None])
    o = jnp.einsum('bhrnqk,bhrnkd->bhrnqd', p, vr)
    o = o.reshape(B, H, dilation, L, hd).transpose(0, 1, 3, 2, 4).reshape(B, H, Sp, hd)[:, :, :S]
    lse = lse.reshape(B, H, dilation, L).transpose(0, 1, 3, 2).reshape(B, H, Sp)[:, :, :S]
    return o, lse


def mixing_sublayer(h, positions, w_in, b_forget, w_out):
    B, S, _ = h.shape
    proj = h @ w_in
    splits = list(np.cumsum(PROJ_SIZES)[:-1])
    fq, fk, fv, fog, ff, dq, dk, dv = jnp.split(proj, splits, axis=-1)

    def heads(t, n):
        return t.reshape(B, S, n, HEAD_DIM)

    def bhsd(t):
        return t.transpose(0, 2, 1, 3).astype(jnp.float32)

    log_f = jax.nn.log_sigmoid(ff.astype(jnp.float32) + b_forget).transpose(0, 2, 1)
    fox = forgetting_attention(bhsd(heads(fq, FOX_HEADS)), bhsd(heads(fk, FOX_HEADS)),
                               bhsd(heads(fv, FOX_HEADS)), log_f)
    fox = fox.transpose(0, 2, 1, 3).reshape(B, S, FOX_W) * jax.nn.sigmoid(fog.astype(jnp.float32))

    dqh = bhsd(partial_rope(heads(dq, DIL_HEADS), positions))
    dkh = bhsd(partial_rope(heads(dk, DIL_HEADS), positions))
    dvh = bhsd(heads(dv, DIL_HEADS))
    outs, lses = [], []
    for window, dilation in DILATED_PATTERNS:
        o, l = dilated_window_attention(dqh, dkh, dvh, window, dilation)
        outs.append(o)
        lses.append(l)
    wts = jax.nn.softmax(jnp.stack(lses, axis=0), axis=0)
    dil = jnp.sum(wts[..., None] * jnp.stack(outs, axis=0), axis=0)
    dil = dil.transpose(0, 2, 1, 3).reshape(B, S, DIL_W)

    merged = jnp.concatenate([fox, dil], axis=-1).astype(h.dtype)
    return merged @ w_out


def hierarchical_moe(h, w_rg, b_rg, w_re, b_re, w_up, w_gate, w_down):
    N, D = h.shape
    gl = (h @ w_rg).astype(jnp.float32) + b_rg
    g = jnp.argmax(gl, axis=-1)
    p_g = jnp.take_along_axis(jax.nn.softmax(gl, axis=-1), g[:, None], axis=-1)[:, 0]
    el = ((h @ w_re).astype(jnp.float32) + b_re).reshape(N, N_GROUPS, EXPERTS_PER_GROUP)
    el_g = jnp.take_along_axis(el, g[:, None, None], axis=1)[:, 0]
    top_v, top_i = lax.top_k(el_g, TOP_K)
    top_w = jax.nn.softmax(top_v, axis=-1) * p_g[:, None]

    A = N * TOP_K
    eid = (g[:, None] * EXPERTS_PER_GROUP + top_i).reshape(A).astype(jnp.int32)
    tok = jnp.repeat(jnp.arange(N, dtype=jnp.int32), TOP_K)
    wgt = top_w.reshape(A)
    order = jnp.argsort(eid)
    s_eid, s_tok, s_w = eid[order], tok[order], wgt[order]
    counts = jnp.bincount(eid, length=N_EXPERTS)
    pcounts = (counts + MOE_BLOCK - 1) // MOE_BLOCK * MOE_BLOCK
    offs = jnp.cumsum(counts) - counts
    pend = jnp.cumsum(pcounts)
    poffs = pend - pcounts
    dest = poffs[s_eid] + (jnp.arange(A) - offs[s_eid])
    nblk = -(-A // MOE_BLOCK) + N_EXPERTS
    P = nblk * MOE_BLOCK
    row_tok = jnp.full((P,), N, dtype=jnp.int32).at[dest].set(s_tok)
    row_w = jnp.zeros((P,), dtype=h.dtype).at[dest].set(s_w.astype(h.dtype))
    blk_e = jnp.minimum(jnp.searchsorted(pend, jnp.arange(nblk) * MOE_BLOCK, side='right'),
                        N_EXPERTS - 1)
    xr = h[jnp.minimum(row_tok, N - 1)].reshape(nblk, MOE_BLOCK, D)

    def expert_block(args):
        xb, e = args
        return (jax.nn.silu(xb @ w_gate[e]) * (xb @ w_up[e])) @ w_down[e]

    yr = lax.map(expert_block, (xr, blk_e)).reshape(P, D)
    return jax.ops.segment_sum(yr * row_w[:, None], row_tok, num_segments=N)


def setup_inputs(seed: int = 0) -> dict:
    key = jax.random.key(seed)
    ks = jax.random.split(key, 20)
    D = D_MODEL
    f32 = jnp.float32
    x = jax.random.normal(ks[0], (BATCH, SEQ, D), f32)
    c = jax.random.normal(ks[1], (BATCH, D), f32)
    positions = (jax.random.randint(ks[2], (BATCH, 1), 0, 2048)
                 + jnp.arange(SEQ)[None, :]).astype(jnp.int32)
    w_ada = jax.random.normal(ks[3], (D, 6 * D), f32) * (0.5 * D ** -0.5)
    b_ada = jax.random.normal(ks[4], (6 * D,), f32) * 0.01
    col_scale = np.ones((PROJ_COLS,), np.float32)
    bounds = np.concatenate([[0], np.cumsum(PROJ_SIZES)])
    for idx in (2, 7):
        col_scale[bounds[idx]:bounds[idx + 1]] = DEEPNORM_BETA
    w_in = jax.random.normal(ks[5], (D, PROJ_COLS), f32) * (D ** -0.5) * jnp.asarray(col_scale)
    b_forget = 2.0 + 4.0 * jax.random.uniform(ks[6], (FOX_HEADS,), f32)
    w_out = jax.random.normal(ks[7], (D, D), f32) * (D ** -0.5) * DEEPNORM_BETA
    ln1_g = 1.0 + 0.01 * jax.random.normal(ks[8], (D,), f32)
    ln1_b = 0.01 * jax.random.normal(ks[9], (D,), f32)
    w_router_group = jax.random.normal(ks[10], (D, N_GROUPS), f32) * (D ** -0.5)
    b_router_group = 0.01 * jax.random.normal(ks[11], (N_GROUPS,), f32)
    w_router_expert = jax.random.normal(ks[12], (D, N_EXPERTS), f32) * (D ** -0.5)
    b_router_expert = 0.01 * jax.random.normal(ks[13], (N_EXPERTS,), f32)
    w_up = jax.random.normal(ks[14], (N_EXPERTS, D, D_EXPERT), f32) * (D ** -0.5)
    w_gate = jax.random.normal(ks[15], (N_EXPERTS, D, D_EXPERT), f32) * (D ** -0.5)
    w_down = jax.random.normal(ks[16], (N_EXPERTS, D_EXPERT, D), f32) * (D_EXPERT ** -0.5) * DEEPNORM_BETA
    ln2_g = 1.0 + 0.01 * jax.random.normal(ks[17], (D,), f32)
    ln2_b = 0.01 * jax.random.normal(ks[18], (D,), f32)
    return {"x": x, "c": c, "positions": positions, "w_ada": w_ada, "b_ada": b_ada,
            "w_in": w_in, "b_forget": b_forget, "w_out": w_out, "ln1_g": ln1_g, "ln1_b": ln1_b,
            "w_router_group": w_router_group, "b_router_group": b_router_group,
            "w_router_expert": w_router_expert, "b_router_expert": b_router_expert,
            "w_up": w_up, "w_gate": w_gate, "w_down": w_down, "ln2_g": ln2_g, "ln2_b": ln2_b}


def reference(x, c, positions, w_ada, b_ada, w_in, b_forget, w_out, ln1_g, ln1_b,
              w_router_group, b_router_group, w_router_expert, b_router_expert,
              w_up, w_gate, w_down, ln2_g, ln2_b):
    B, S, D = x.shape
    mod = jax.nn.silu(c) @ w_ada + b_ada
    shift1, scale1, gate1, shift2, scale2, gate2 = jnp.split(mod, 6, axis=-1)
    for _ in range(DEPTH):
        h = x * (1.0 + scale1[:, None, :]) + shift1[:, None, :]
        y = mixing_sublayer(h, positions, w_in, b_forget, w_out)
        x = layer_norm(DEEPNORM_ALPHA * x + gate1[:, None, :] * y, ln1_g, ln1_b)
        h = x * (1.0 + scale2[:, None, :]) + shift2[:, None, :]
        y = hierarchical_moe(h.reshape(B * S, D), w_router_group, b_router_group,
                             w_router_expert, b_router_expert, w_up, w_gate, w_down).reshape(B, S, D)
        x = layer_norm(DEEPNORM_ALPHA * x + gate2[:, None, :] * y, ln2_g, ln2_b)
    return x
```

```python
import functools

import jax
import jax.numpy as jnp
import numpy as np
from jax import lax
from jax.experimental import pallas as pl
from jax.experimental.pallas import tpu as pltpu

HEAD_DIM = 64
FOX_HEADS = 8
DIL_HEADS = 8
FOX_W = FOX_HEADS * HEAD_DIM
DIL_W = DIL_HEADS * HEAD_DIM
DILATED_PATTERNS = ((128, 1), (512, 4), (2048, 16))
ROPE_THETA = 500000.0
ROT_DIM = HEAD_DIM // 4
N_GROUPS = 4
EXPERTS_PER_GROUP = 8
N_EXPERTS = N_GROUPS * EXPERTS_PER_GROUP
TOP_K = 2
D_EXPERT = 512
MOE_BLOCK = 256
LN_EPS = 1e-5
NEG = -1e30
DEPTH = 1
DEEPNORM_ALPHA = (2 * DEPTH) ** 0.25
QK_SCALE = HEAD_DIM ** -0.5

LANES = 128
HEADS_PER_VREG = LANES // HEAD_DIM
VMEM_LIMIT = 56 * 1024 * 1024

F32 = jnp.float32
BF16 = jnp.bfloat16
HIGHEST = lax.Precision.HIGHEST
NT_DIMS = (((1,), (1,)), ((), ()))


def _params(*sem):
    return pltpu.CompilerParams(dimension_semantics=sem, vmem_limit_bytes=VMEM_LIMIT)


def _sigmoid(v):
    return 1.0 / (1.0 + jnp.exp(-v))


def _layer_norm(v, g, b):
    mu = jnp.mean(v, axis=-1, keepdims=True)
    d = v - mu
    var = jnp.mean(d * d, axis=-1, keepdims=True)
    return d * lax.rsqrt(var + LN_EPS) * g + b


def _mod_kernel(c_ref, w_ref, b_ref, o_ref):
    c = c_ref[...]
    o_ref[...] = jnp.dot(c * _sigmoid(c), w_ref[...], precision=HIGHEST,
                         preferred_element_type=F32) + b_ref[...]


def _modulation(c, w_ada, b_ada):
    B, D = c.shape
    cols = w_ada.shape[1]
    tn = 1024
    return pl.pallas_call(
        _mod_kernel,
        grid=(cols // tn,),
        in_specs=[pl.BlockSpec((B, D), lambda j: (0, 0)),
                  pl.BlockSpec((D, tn), lambda j: (0, j)),
                  pl.BlockSpec((1, tn), lambda j: (0, j))],
        out_specs=pl.BlockSpec((B, tn), lambda j: (0, j)),
        out_shape=jax.ShapeDtypeStruct((B, cols), F32),
        compiler_params=_params("arbitrary"),
        name="mod",
    )(c, w_ada, b_ada.reshape(1, cols))


def _proj_kernel(x_ref, pos_ref, sc_ref, sh_ref, wf_ref, wd_ref, wff_ref, bf_ref, invf_ref, sgn_ref,
                 pf_ref, pd_ref, lf_ref):
    tm = x_ref.shape[1]
    h = (x_ref[0] * (1.0 + sc_ref[0]) + sh_ref[0]).astype(BF16)

    for ci in range(4):
        acc = jnp.dot(h, wf_ref[:, ci * FOX_W:(ci + 1) * FOX_W], preferred_element_type=F32)
        if ci == 0:
            acc = acc * QK_SCALE
        pf_ref[0, :, ci * FOX_W:(ci + 1) * FOX_W] = acc.astype(BF16)

    z = jnp.dot(h, wff_ref[...], preferred_element_type=F32) + bf_ref[...]
    lf_ref[0] = jnp.minimum(z, 0.0) - jnp.log1p(jnp.exp(-jnp.abs(z)))

    ang = pos_ref[0].astype(F32) * invf_ref[...]
    cs = jnp.cos(ang)
    sn = jnp.sin(ang) * sgn_ref[...]
    lane = lax.broadcasted_iota(jnp.int32, (tm, LANES), 1)
    first_half = (lane % HEAD_DIM) < (ROT_DIM // 2)
    for ci in range(3):
        acc = jnp.dot(h, wd_ref[:, ci * DIL_W:(ci + 1) * DIL_W], preferred_element_type=F32)
        if ci == 2:
            pd_ref[0, :, ci * DIL_W:(ci + 1) * DIL_W] = acc
            continue
        for j in range(DIL_W // LANES):
            t = acc[:, j * LANES:(j + 1) * LANES]
            partner = jnp.where(first_half, pltpu.roll(t, LANES - ROT_DIM // 2, 1),
                                pltpu.roll(t, ROT_DIM // 2, 1))
            r = t * cs + partner * sn
            if ci == 0:
                r = r * QK_SCALE
            pd_ref[0, :, ci * DIL_W + j * LANES:ci * DIL_W + (j + 1) * LANES] = r


def _projection(x, positions, scale1, shift1, w_in, b_forget, tm):
    B, S, D = x.shape
    o = np.cumsum((0, FOX_W, FOX_W, FOX_W, FOX_W, FOX_HEADS, DIL_W, DIL_W, DIL_W))
    w_fox = w_in[:, o[0]:o[4]].astype(BF16)
    w_ff = jnp.pad(w_in[:, o[4]:o[5]], ((0, 0), (0, LANES - FOX_HEADS))).astype(BF16)
    w_dil = w_in[:, o[5]:o[8]].astype(BF16)
    b_f = jnp.pad(b_forget, (0, LANES - FOX_HEADS)).reshape(1, LANES)
    e = np.arange(LANES) % HEAD_DIM
    inv_freq = ROPE_THETA ** (-jnp.arange(0, ROT_DIM, 2, dtype=F32) / ROT_DIM)
    invf = jnp.where(e < ROT_DIM, jnp.tile(inv_freq, LANES // (ROT_DIM // 2)), 0.0).reshape(1, LANES)
    sgn = jnp.asarray(np.where(e < ROT_DIM // 2, -1.0, np.where(e < ROT_DIM, 1.0, 0.0)), F32).reshape(1, LANES)
    const = lambda shape: pl.BlockSpec(shape, lambda b, i: (0,) * len(shape))
    return pl.pallas_call(
        _proj_kernel,
        grid=(B, S // tm),
        in_specs=[pl.BlockSpec((1, tm, D), lambda b, i: (b, i, 0)),
                  pl.BlockSpec((1, tm, 1), lambda b, i: (b, i, 0)),
                  pl.BlockSpec((1, 1, D), lambda b, i: (b, 0, 0)),
                  pl.BlockSpec((1, 1, D), lambda b, i: (b, 0, 0)),
                  const((D, 4 * FOX_W)), const((D, 3 * DIL_W)), const((D, LANES)),
                  const((1, LANES)), const((1, LANES)), const((1, LANES))],
        out_specs=[pl.BlockSpec((1, tm, 4 * FOX_W), lambda b, i: (b, i, 0)),
                   pl.BlockSpec((1, tm, 3 * DIL_W), lambda b, i: (b, i, 0)),
                   pl.BlockSpec((1, tm, LANES), lambda b, i: (b, i, 0))],
        out_shape=[jax.ShapeDtypeStruct((B, S, 4 * FOX_W), BF16),
                   jax.ShapeDtypeStruct((B, S, 3 * DIL_W), F32),
                   jax.ShapeDtypeStruct((B, S, LANES), F32)],
        compiler_params=_params("arbitrary", "arbitrary"),
        name="proj",
    )(x, positions.reshape(B, S, 1), scale1.reshape(B, 1, D), shift1.reshape(B, 1, D),
      w_fox, w_dil, w_ff, b_f, invf, sgn)


def _cum_kernel(lf_ref, c_ref, ct_ref):
    S = lf_ref.shape[1]
    r = lax.broadcasted_iota(jnp.int32, (LANES, LANES), 0)
    c = lax.broadcasted_iota(jnp.int32, (LANES, LANES), 1)
    tri = (c <= r).astype(F32)

    def body(j, carry):
        off = pl.multiple_of(j * LANES, LANES)
        cum = jnp.dot(tri, lf_ref[0, pl.ds(off, LANES), :], precision=HIGHEST,
                      preferred_element_type=F32) + carry
        c_ref[0, pl.ds(off, LANES), :] = cum
        ct_ref[0, :, pl.ds(off, LANES)] = cum.T[0:FOX_HEADS, :]
        return cum[LANES - 1:LANES, :]

    lax.fori_loop(0, S // LANES, body, jnp.zeros((1, LANES), F32))


def _cumulative_gate(log_f):
    B, S, _ = log_f.shape
    return pl.pallas_call(
        _cum_kernel,
        grid=(B,),
        in_specs=[pl.BlockSpec((1, S, LANES), lambda b: (b, 0, 0))],
        out_specs=[pl.BlockSpec((1, S, LANES), lambda b: (b, 0, 0)),
                   pl.BlockSpec((1, FOX_HEADS, S), lambda b: (b, 0, 0))],
        out_shape=[jax.ShapeDtypeStruct((B, S, LANES), F32),
                   jax.ShapeDtypeStruct((B, FOX_HEADS, S), F32)],
        compiler_params=_params("arbitrary"),
        name="cum",
    )(log_f)


def _fox_kernel(q_ref, k_ref, v_ref, g_ref, cq_ref, ck_ref, o_ref):
    tq = q_ref.shape[1]
    i = pl.program_id(2)
    q = q_ref[0]
    lane = lax.broadcasted_iota(jnp.int32, (tq, LANES), 1)
    row = lax.broadcasted_iota(jnp.int32, (tq, tq), 0)
    col = lax.broadcasted_iota(jnp.int32, (tq, tq), 1)
    causal = col <= row
    outs = []
    for hh in range(HEADS_PER_VREG):
        qh = jnp.where((lane // HEAD_DIM) == hh, q, jnp.zeros_like(q))
        cq = cq_ref[0, 0, :, hh:hh + 1]

        def step(j, carry, masked, qh=qh, cq=cq, hh=hh):
            m, l, acc = carry
            off = pl.multiple_of(j * tq, tq)
            kb = k_ref[0, pl.ds(off, tq), :]
            vb = v_ref[0, pl.ds(off, tq), :]
            ck = ck_ref[0, 0, hh:hh + 1, pl.ds(off, tq)]
            s = lax.dot_general(qh, kb, NT_DIMS, preferred_element_type=F32) + (cq - ck)
            if masked:
                s = jnp.where(causal, s, NEG)
            m_new = jnp.maximum(m, jnp.max(s, axis=1, keepdims=True))
            alpha = jnp.exp(m - m_new)
            p = jnp.exp(s - m_new)
            l = alpha * l + jnp.sum(p, axis=1, keepdims=True)
            acc = alpha * acc + jnp.dot(p.astype(BF16), vb, preferred_element_type=F32)
            return m_new, l, acc

        init = (jnp.full((tq, 1), NEG, F32), jnp.zeros((tq, 1), F32), jnp.zeros((tq, LANES), F32))
        carry = lax.fori_loop(0, i, functools.partial(step, masked=False), init)
        _, l, acc = step(i, carry, masked=True)
        outs.append(acc * (1.0 / l))
    o = jnp.where(lane < HEAD_DIM, outs[0], outs[1])
    o_ref[0] = (o * _sigmoid(g_ref[0].astype(F32))).astype(BF16)


def _fox_attention(pf, cq, ck, tq):
    B, S, _ = pf.shape
    nhp = FOX_W // LANES
    return pl.pallas_call(
        _fox_kernel,
        grid=(B, nhp, S // tq),
        in_specs=[pl.BlockSpec((1, tq, LANES), lambda b, h, i: (b, i, h)),
                  pl.BlockSpec((1, S, LANES), lambda b, h, i: (b, 0, nhp + h)),
                  pl.BlockSpec((1, S, LANES), lambda b, h, i: (b, 0, 2 * nhp + h)),
                  pl.BlockSpec((1, tq, LANES), lambda b, h, i: (b, i, 3 * nhp + h)),
                  pl.BlockSpec((1, 1, tq, HEADS_PER_VREG), lambda b, h, i: (b, h, i, 0)),
                  pl.BlockSpec((1, 1, HEADS_PER_VREG, S), lambda b, h, i: (b, h, 0, 0))],
        out_specs=pl.BlockSpec((1, tq, LANES), lambda b, h, i: (b, i, h)),
        out_shape=jax.ShapeDtypeStruct((B, S, FOX_W), BF16),
        compiler_params=_params("arbitrary", "arbitrary", "arbitrary"),
        name="fox",
    )(pf, pf, pf, pf, cq, ck)


def _dil_kernel(q_ref, k_ref, v_ref, o_ref, m_s, l_s, a_s):
    S = q_ref.shape[1]
    W = LANES
    lane = lax.broadcasted_iota(jnp.int32, (W, LANES), 1)
    head0 = lane < HEAD_DIM
    ri = lax.broadcasted_iota(jnp.int32, (W, 2 * W), 0)
    ci = lax.broadcasted_iota(jnp.int32, (W, 2 * W), 1)
    bias_rest = jnp.where((ci >= ri) & (ci <= ri + W), 0.0, NEG)
    bias_first = jnp.where((ci < W) & (ci <= ri), 0.0, NEG)

    for p, (window, d) in enumerate(DILATED_PATTERNS):
        assert window // d == W and S % (d * W) == 0 and S // (d * W) >= 2
        nb = S // (d * W)

        def unit(u, _, p=p, d=d, nb=nb):
            r = u // nb
            n = u % nb
            qidx = pl.ds(n * (W * d) + r, W, stride=d)
            kidx = pl.ds(jnp.maximum(n - 1, 0) * (W * d) + r, 2 * W, stride=d)
            q = q_ref[0, qidx, :]
            kb = k_ref[0, kidx, :].astype(BF16)
            vb = v_ref[0, kidx, :].astype(BF16)
            bias = jnp.where(n == 0, bias_first, bias_rest)
            stats = []
            for hh in range(HEADS_PER_VREG):
                qh = jnp.where((lane // HEAD_DIM) == hh, q, 0.0).astype(BF16)
                s = lax.dot_general(qh, kb, NT_DIMS, preferred_element_type=F32) + bias
                mu = jnp.max(s, axis=1, keepdims=True)
                pu = jnp.exp(s - mu)
                lu = jnp.sum(pu, axis=1, keepdims=True)
                au = jnp.dot(pu.astype(BF16), vb, preferred_element_type=F32)
                stats.append((mu, lu, au))
            m_u = jnp.where(head0, stats[0][0], stats[1][0])
            l_u = jnp.where(head0, stats[0][1], stats[1][1])
            a_u = jnp.where(head0, stats[0][2], stats[1][2])
            if p == 0:
                m_s[qidx, :] = m_u
                l_s[qidx, :] = l_u
                a_s[qidx, :] = a_u
            else:
                m_o = m_s[qidx, :]
                m_n = jnp.maximum(m_o, m_u)
                e_o = jnp.exp(m_o - m_n)
                e_u = jnp.exp(m_u - m_n)
                m_s[qidx, :] = m_n
                l_s[qidx, :] = l_s[qidx, :] * e_o + l_u * e_u
                a_s[qidx, :] = a_s[qidx, :] * e_o + a_u * e_u
            return 0

        lax.fori_loop(0, S // W, unit, 0)

    o_ref[0] = (a_s[...] * (1.0 / l_s[...])).astype(BF16)


def _dilated_attention(pd):
    B, S, _ = pd.shape
    nhp = DIL_W // LANES
    return pl.pallas_call(
        _dil_kernel,
        grid=(B, nhp),
        in_specs=[pl.BlockSpec((1, S, LANES), lambda b, h: (b, 0, h)),
                  pl.BlockSpec((1, S, LANES), lambda b, h: (b, 0, nhp + h)),
                  pl.BlockSpec((1, S, LANES), lambda b, h: (b, 0, 2 * nhp + h))],
        out_specs=pl.BlockSpec((1, S, LANES), lambda b, h: (b, 0, h)),
        out_shape=jax.ShapeDtypeStruct((B, S, DIL_W), BF16),
        scratch_shapes=[pltpu.VMEM((S, LANES), F32)] * 3,
        compiler_params=_params("arbitrary", "arbitrary"),
        name="dil",
    )(pd, pd, pd)


def _out_kernel(of_ref, od_ref, x_ref, g1_ref, sc2_ref, sh2_ref, wo_ref, lng_ref, lnb_ref, wr_ref, br_ref,
                x1_ref, h2_ref, rt_ref, cnt_ref):
    tm = x_ref.shape[1]
    first_step = (pl.program_id(0) == 0) & (pl.program_id(1) == 0)

    @pl.when(first_step)
    def _():
        cnt_ref[...] = jnp.zeros_like(cnt_ref)

    y = (jnp.dot(of_ref[0], wo_ref[0:FOX_W, :], preferred_element_type=F32)
         + jnp.dot(od_ref[0], wo_ref[FOX_W:FOX_W + DIL_W, :], preferred_element_type=F32))
    x1 = _layer_norm(DEEPNORM_ALPHA * x_ref[0] + g1_ref[0] * y, lng_ref[...], lnb_ref[...])
    x1_ref[0] = x1
    h2 = x1 * (1.0 + sc2_ref[0]) + sh2_ref[0]
    h2_ref[0] = h2

    lg = jnp.dot(h2, wr_ref[...], precision=HIGHEST, preferred_element_type=F32) + br_ref[...]
    lane = lax.broadcasted_iota(jnp.int32, (tm, LANES), 1)
    lanef = lane.astype(F32)
    far = float(LANES)
    is_g = lane < N_GROUPS
    gl = jnp.where(is_g, lg, NEG)
    gmax = jnp.max(gl, axis=1, keepdims=True)
    gidx = jnp.min(jnp.where(gl == gmax, lanef, far), axis=1, keepdims=True)
    p_g = 1.0 / jnp.sum(jnp.where(is_g, jnp.exp(gl - gmax), 0.0), axis=1, keepdims=True)
    e_lo = N_GROUPS + gidx * EXPERTS_PER_GROUP
    in_group = (lanef >= e_lo) & (lanef < e_lo + EXPERTS_PER_GROUP)
    el = jnp.where(in_group, lg, NEG)
    v1 = jnp.max(el, axis=1, keepdims=True)
    i1 = jnp.min(jnp.where(in_group & (el == v1), lanef, far), axis=1, keepdims=True)
    rest = in_group & (lanef != i1)
    el2 = jnp.where(rest, lg, NEG)
    v2 = jnp.max(el2, axis=1, keepdims=True)
    i2 = jnp.min(jnp.where(rest & (el2 == v2), lanef, far), axis=1, keepdims=True)
    e21 = jnp.exp(v2 - v1)
    w1 = p_g / (1.0 + e21)
    w2 = p_g * e21 / (1.0 + e21)

    pick1 = lanef == i1
    pick2 = lanef == i2
    onehot = jnp.where(pick1 | pick2, 1.0, 0.0)
    rr = lax.broadcasted_iota(jnp.int32, (tm, tm), 0)
    cc = lax.broadcasted_iota(jnp.int32, (tm, tm), 1)
    strict_lower = jnp.where(cc < rr, 1.0, 0.0).astype(BF16)
    before = jnp.dot(strict_lower, onehot.astype(BF16), preferred_element_type=F32) + cnt_ref[...]
    r1 = jnp.sum(jnp.where(pick1, before, 0.0), axis=1, keepdims=True)
    r2 = jnp.sum(jnp.where(pick2, before, 0.0), axis=1, keepdims=True)
    cnt_ref[...] += jnp.sum(onehot, axis=0, keepdims=True)

    packed = jnp.zeros((tm, LANES), F32)
    for k, val in enumerate((i1 - N_GROUPS, i2 - N_GROUPS, w1, w2, r1, r2)):
        packed = jnp.where(lane == k, val, packed)
    rt_ref[0] = packed


def _out_and_route(of, od, x, gate1, scale2, shift2, w_out, ln_g, ln_b, w_rg, b_rg, w_re, b_re, tm):
    B, S, D = x.shape
    n_r = N_GROUPS + N_EXPERTS
    w_r = jnp.pad(jnp.concatenate([w_rg, w_re], axis=1), ((0, 0), (0, LANES - n_r)))
    b_r = jnp.pad(jnp.concatenate([b_rg, b_re]), (0, LANES - n_r)).reshape(1, LANES)
    tile = lambda w: pl.BlockSpec((1, tm, w), lambda b, i: (b, i, 0))
    per_batch = pl.BlockSpec((1, 1, D), lambda b, i: (b, 0, 0))
    const = lambda shape: pl.BlockSpec(shape, lambda b, i: (0,) * len(shape))
    return pl.pallas_call(
        _out_kernel,
        grid=(B, S // tm),
        in_specs=[tile(FOX_W), tile(DIL_W), tile(D), per_batch, per_batch, per_batch,
                  const((D, D)), const((1, D)), const((1, D)), const((D, LANES)), const((1, LANES))],
        out_specs=[tile(D), tile(D), tile(LANES), const((1, LANES))],
        out_shape=[jax.ShapeDtypeStruct((B, S, D), F32), jax.ShapeDtypeStruct((B, S, D), F32),
                   jax.ShapeDtypeStruct((B, S, LANES), F32), jax.ShapeDtypeStruct((1, LANES), F32)],
        compiler_params=_params("arbitrary", "arbitrary"),
        name="out",
    )(of, od, x, gate1.reshape(B, 1, D), scale2.reshape(B, 1, D), shift2.reshape(B, 1, D),
      w_out.astype(BF16), ln_g.reshape(1, D), ln_b.reshape(1, D), w_r, b_r)


def _row_copy(src_ref, dst_ref, sem, src_row, dst_row):
    return pltpu.make_async_copy(src_ref.at[pl.ds(src_row, 1), :], dst_ref.at[pl.ds(dst_row, 1), :], sem)


def _moe_kernel(blk_e_ref, row_tok_ref, h_ref, wg_ref, wu_ref, wd_ref, y_ref, xbuf, sem):
    del blk_e_ref
    base = pl.program_id(0) * MOE_BLOCK

    def issue(r, _):
        _row_copy(h_ref, xbuf, sem, row_tok_ref[base + r], r).start()
        return 0

    lax.fori_loop(0, MOE_BLOCK, issue, 0)
    pltpu.make_async_copy(h_ref.at[pl.ds(0, MOE_BLOCK), :], xbuf, sem).wait()
    xb = xbuf[...].astype(BF16)
    g = jnp.dot(xb, wg_ref[0], preferred_element_type=F32)
    u = jnp.dot(xb, wu_ref[0], preferred_element_type=F32)
    mid = (g * _sigmoid(g) * u).astype(BF16)
    y_ref[...] = jnp.dot(mid, wd_ref[0], preferred_element_type=F32)


def _expert_mlp(h2, row_tok, blk_e, w_gate, w_up, w_down):
    N, D = h2.shape
    nblk = blk_e.shape[0]
    return pl.pallas_call(
        _moe_kernel,
        grid_spec=pltpu.PrefetchScalarGridSpec(
            num_scalar_prefetch=2,
            grid=(nblk,),
            in_specs=[pl.BlockSpec(memory_space=pl.ANY),
                      pl.BlockSpec((1, D, D_EXPERT), lambda i, be, rt: (be[i], 0, 0)),
                      pl.BlockSpec((1, D, D_EXPERT), lambda i, be, rt: (be[i], 0, 0)),
                      pl.BlockSpec((1, D_EXPERT, D), lambda i, be, rt: (be[i], 0, 0))],
            out_specs=pl.BlockSpec((MOE_BLOCK, D), lambda i, be, rt: (i, 0)),
            scratch_shapes=[pltpu.VMEM((MOE_BLOCK, D), F32), pltpu.SemaphoreType.DMA(())]),
        out_shape=jax.ShapeDtypeStruct((nblk * MOE_BLOCK, D), F32),
        compiler_params=_params("arbitrary"),
        name="moe",
    )(blk_e, row_tok, h2, w_gate.astype(BF16), w_up.astype(BF16), w_down.astype(BF16))


def _final_kernel(dest_ref, y_ref, x1_ref, rt_ref, g2_ref, lng_ref, lnb_ref, o_ref, ybuf, sem):
    tm = x1_ref.shape[1]
    base = (pl.program_id(0) * pl.num_programs(1) + pl.program_id(1)) * tm

    def issue(r, _):
        for k in range(TOP_K):
            _row_copy(y_ref, ybuf.at[k], sem.at[k], dest_ref[(base + r) * TOP_K + k], r).start()
        return 0

    lax.fori_loop(0, tm, issue, 0)
    for k in range(TOP_K):
        pltpu.make_async_copy(y_ref.at[pl.ds(0, tm), :], ybuf.at[k], sem.at[k]).wait()
    rt = rt_ref[0]
    y = rt[:, 2:3] * ybuf[0] + rt[:, 3:4] * ybuf[1]
    o_ref[0] = _layer_norm(DEEPNORM_ALPHA * x1_ref[0] + g2_ref[0] * y, lng_ref[...], lnb_ref[...])


def _combine_and_norm(dest, y_rows, x1, route, gate2, ln_g, ln_b, tm):
    B, S, D = x1.shape
    return pl.pallas_call(
        _final_kernel,
        grid_spec=pltpu.PrefetchScalarGridSpec(
            num_scalar_prefetch=1,
            grid=(B, S // tm),
            in_specs=[pl.BlockSpec(memory_space=pl.ANY),
                      pl.BlockSpec((1, tm, D), lambda b, i, d: (b, i, 0)),
                      pl.BlockSpec((1, tm, LANES), lambda b, i, d: (b, i, 0)),
                      pl.BlockSpec((1, 1, D), lambda b, i, d: (b, 0, 0)),
                      pl.BlockSpec((1, D), lambda b, i, d: (0, 0)),
                      pl.BlockSpec((1, D), lambda b, i, d: (0, 0))],
            out_specs=pl.BlockSpec((1, tm, D), lambda b, i, d: (b, i, 0)),
            scratch_shapes=[pltpu.VMEM((TOP_K, tm, D), F32), pltpu.SemaphoreType.DMA((TOP_K,))]),
        out_shape=jax.ShapeDtypeStruct((B, S, D), F32),
        compiler_params=_params("arbitrary", "arbitrary"),
        name="final",
    )(dest, y_rows, x1, route, gate2.reshape(B, 1, D), ln_g.reshape(1, D), ln_b.reshape(1, D))


def kernel(x, c, positions, w_ada, b_ada, w_in, b_forget, w_out, ln1_g, ln1_b, w_router_group, b_router_group,
           w_router_expert, b_router_expert, w_up, w_gate, w_down, ln2_g, ln2_b):
    B, S, D = x.shape
    N = B * S
    assert D == FOX_W + DIL_W and S % 2048 == 0
    mod = _modulation(c, w_ada, b_ada)
    shift1, scale1, gate1, shift2, scale2, gate2 = jnp.split(mod, 6, axis=-1)

    pf, pd, log_f = _projection(x, positions, scale1, shift1, w_in, b_forget, tm=512)
    cum, cum_t = _cumulative_gate(log_f)
    nhp = FOX_W // LANES
    cq = cum[..., :FOX_HEADS].reshape(B, S, nhp, HEADS_PER_VREG).transpose(0, 2, 1, 3)
    ck = cum_t.reshape(B, nhp, HEADS_PER_VREG, S)
    of = _fox_attention(pf, cq, ck, tq=512)
    od = _dilated_attention(pd)

    x1, h2, route, counts = _out_and_route(of, od, x, gate1, scale2, shift2, w_out, ln1_g, ln1_b,
                                           w_router_group, b_router_group, w_router_expert, b_router_expert,
                                           tm=512)

    route = route.reshape(N, LANES)
    eid = route[:, 0:TOP_K].astype(jnp.int32)
    rank = route[:, 4:4 + TOP_K].astype(jnp.int32)
    cnt = counts[0, N_GROUPS:N_GROUPS + N_EXPERTS].astype(jnp.int32)
    pcnt = (cnt + MOE_BLOCK - 1) // MOE_BLOCK * MOE_BLOCK
    pend = jnp.cumsum(pcnt)
    dest = ((pend - pcnt)[eid] + rank).reshape(N * TOP_K)
    nblk = (N * TOP_K) // MOE_BLOCK + N_EXPERTS
    tok = jnp.repeat(jnp.arange(N, dtype=jnp.int32), TOP_K)
    row_tok = jnp.full((nblk * MOE_BLOCK,), N - 1, jnp.int32).at[dest].set(tok)
    blk_e = jnp.minimum(jnp.searchsorted(pend, jnp.arange(nblk, dtype=jnp.int32) * MOE_BLOCK, side='right'),
                        N_EXPERTS - 1).astype(jnp.int32)

    y_rows = _expert_mlp(h2.reshape(N, D), row_tok, blk_e, w_gate, w_up, w_down)
    return _combine_and_norm(dest, y_rows, x1, route.reshape(B, S, LANES), gate2, ln2_g, ln2_b, tm=256)
```

```python
import functools

import jax
import jax.numpy as jnp
import numpy as np
from jax import lax
from jax.experimental import pallas as pl
from jax.experimental.pallas import tpu as pltpu

HEAD_DIM = 64
FOX_HEADS = 8
DIL_HEADS = 8
FOX_W = FOX_HEADS * HEAD_DIM
DIL_W = DIL_HEADS * HEAD_DIM
DILATED_PATTERNS = ((128, 1), (512, 4), (2048, 16))
ROPE_THETA = 500000.0
ROT_DIM = HEAD_DIM // 4
N_GROUPS = 4
EXPERTS_PER_GROUP = 8
N_EXPERTS = N_GROUPS * EXPERTS_PER_GROUP
TOP_K = 2
D_EXPERT = 512
MOE_BLOCK = 256
LN_EPS = 1e-5
NEG = -1e30
DEPTH = 1
DEEPNORM_ALPHA = (2 * DEPTH) ** 0.25
QK_SCALE = HEAD_DIM ** -0.5
LOG2E = 1.4426950408889634

LANES = 128
BF16_SUBLANES = 16
HEADS_PER_VREG = LANES // HEAD_DIM
V_ROWS = HEAD_DIM + BF16_SUBLANES
FOX_TILE = 512
DIL_UNROLL = 4
VMEM_LIMIT = 56 * 1024 * 1024

F32 = jnp.float32
BF16 = jnp.bfloat16
HIGHEST = lax.Precision.HIGHEST
NT_DIMS = (((1,), (1,)), ((), ()))


def _params(*sem):
    return pltpu.CompilerParams(dimension_semantics=sem, vmem_limit_bytes=VMEM_LIMIT)


def _sigmoid(v):
    return 1.0 / (1.0 + jnp.exp(-v))


def _layer_norm(v, g, b):
    mu = jnp.mean(v, axis=-1, keepdims=True)
    d = v - mu
    var = jnp.mean(d * d, axis=-1, keepdims=True)
    return d * lax.rsqrt(var + LN_EPS) * g + b


def _mod_kernel(c_ref, w_ref, b_ref, o_ref):
    c = c_ref[...]
    o_ref[...] = jnp.dot(c * _sigmoid(c), w_ref[...], precision=HIGHEST,
                         preferred_element_type=F32) + b_ref[...]


def _modulation(c, w_ada, b_ada):
    B, D = c.shape
    cols = w_ada.shape[1]
    tn = 1024
    return pl.pallas_call(
        _mod_kernel,
        grid=(cols // tn,),
        in_specs=[pl.BlockSpec((B, D), lambda j: (0, 0)),
                  pl.BlockSpec((D, tn), lambda j: (0, j)),
                  pl.BlockSpec((1, tn), lambda j: (0, j))],
        out_specs=pl.BlockSpec((B, tn), lambda j: (0, j)),
        out_shape=jax.ShapeDtypeStruct((B, cols), F32),
        compiler_params=_params("arbitrary"),
        name="mod",
    )(c, w_ada, b_ada.reshape(1, cols))


def _proj_kernel(x_ref, pos_ref, sc_ref, sh_ref, wf_ref, wvt_ref, one_ref, wd_ref, wff_ref, bf_ref, invf_ref,
                 sgn_ref, pf_ref, vt_ref, pd_ref, lf_ref):
    tm = x_ref.shape[1]
    h = (x_ref[0] * (1.0 + sc_ref[0]) + sh_ref[0]).astype(BF16)

    for ci in range(3):
        acc = jnp.dot(h, wf_ref[:, ci * FOX_W:(ci + 1) * FOX_W], preferred_element_type=F32)
        if ci == 0:
            acc = acc * (QK_SCALE * LOG2E)
        pf_ref[0, :, ci * FOX_W:(ci + 1) * FOX_W] = acc.astype(BF16)
    vt = lax.dot_general(wvt_ref[...], h, NT_DIMS, preferred_element_type=F32) + one_ref[...]
    vt_ref[0] = vt.astype(BF16)

    z = jnp.dot(h, wff_ref[...], preferred_element_type=F32) + bf_ref[...]
    lf_ref[0] = (jnp.minimum(z, 0.0) - jnp.log1p(jnp.exp(-jnp.abs(z)))) * LOG2E

    ang = pos_ref[0].astype(F32) * invf_ref[...]
    cs = jnp.cos(ang)
    sn = jnp.sin(ang) * sgn_ref[...]
    lane = lax.broadcasted_iota(jnp.int32, (tm, LANES), 1)
    first_half = (lane % HEAD_DIM) < (ROT_DIM // 2)
    for ci in range(3):
        acc = jnp.dot(h, wd_ref[:, ci * DIL_W:(ci + 1) * DIL_W], preferred_element_type=F32)
        if ci == 2:
            pd_ref[0, :, ci * DIL_W:(ci + 1) * DIL_W] = acc
            continue
        for j in range(DIL_W // LANES):
            t = acc[:, j * LANES:(j + 1) * LANES]
            partner = jnp.where(first_half, pltpu.roll(t, LANES - ROT_DIM // 2, 1),
                                pltpu.roll(t, ROT_DIM // 2, 1))
            r = t * cs + partner * sn
            if ci == 0:
                r = r * (QK_SCALE * LOG2E)
            pd_ref[0, :, ci * DIL_W + j * LANES:ci * DIL_W + (j + 1) * LANES] = r


def _projection(x, positions, scale1, shift1, w_in, b_forget, tm):
    B, S, D = x.shape
    o = np.cumsum((0, FOX_W, FOX_W, FOX_W, FOX_W, FOX_HEADS, DIL_W, DIL_W, DIL_W))
    w_fox = jnp.concatenate([w_in[:, o[0]:o[2]], w_in[:, o[3]:o[4]]], axis=1).astype(BF16)
    w_vt = jnp.pad(w_in[:, o[2]:o[3]].T.reshape(FOX_HEADS, HEAD_DIM, D),
                   ((0, 0), (0, V_ROWS - HEAD_DIM), (0, 0))).reshape(FOX_HEADS * V_ROWS, D).astype(BF16)
    ones_row = jnp.asarray((np.arange(FOX_HEADS * V_ROWS) % V_ROWS == HEAD_DIM).astype(np.float32)
                           ).reshape(FOX_HEADS * V_ROWS, 1)
    w_ff = jnp.pad(w_in[:, o[4]:o[5]], ((0, 0), (0, LANES - FOX_HEADS))).astype(BF16)
    w_dil = w_in[:, o[5]:o[8]].astype(BF16)
    b_f = jnp.pad(b_forget, (0, LANES - FOX_HEADS)).reshape(1, LANES)
    e = np.arange(LANES) % HEAD_DIM
    inv_freq = ROPE_THETA ** (-jnp.arange(0, ROT_DIM, 2, dtype=F32) / ROT_DIM)
    invf = jnp.where(e < ROT_DIM, jnp.tile(inv_freq, LANES // (ROT_DIM // 2)), 0.0).reshape(1, LANES)
    sgn = jnp.asarray(np.where(e < ROT_DIM // 2, -1.0, np.where(e < ROT_DIM, 1.0, 0.0)), F32).reshape(1, LANES)
    const = lambda shape: pl.BlockSpec(shape, lambda b, i: (0,) * len(shape))
    return pl.pallas_call(
        _proj_kernel,
        grid=(B, S // tm),
        in_specs=[pl.BlockSpec((1, tm, D), lambda b, i: (b, i, 0)),
                  pl.BlockSpec((1, tm, 1), lambda b, i: (b, i, 0)),
                  pl.BlockSpec((1, 1, D), lambda b, i: (b, 0, 0)),
                  pl.BlockSpec((1, 1, D), lambda b, i: (b, 0, 0)),
                  const((D, 3 * FOX_W)), const((FOX_HEADS * V_ROWS, D)), const((FOX_HEADS * V_ROWS, 1)),
                  const((D, 3 * DIL_W)), const((D, LANES)),
                  const((1, LANES)), const((1, LANES)), const((1, LANES))],
        out_specs=[pl.BlockSpec((1, tm, 3 * FOX_W), lambda b, i: (b, i, 0)),
                   pl.BlockSpec((1, FOX_HEADS * V_ROWS, tm), lambda b, i: (b, 0, i)),
                   pl.BlockSpec((1, tm, 3 * DIL_W), lambda b, i: (b, i, 0)),
                   pl.BlockSpec((1, tm, LANES), lambda b, i: (b, i, 0))],
        out_shape=[jax.ShapeDtypeStruct((B, S, 3 * FOX_W), BF16),
                   jax.ShapeDtypeStruct((B, FOX_HEADS * V_ROWS, S), BF16),
                   jax.ShapeDtypeStruct((B, S, 3 * DIL_W), F32),
                   jax.ShapeDtypeStruct((B, S, LANES), F32)],
        compiler_params=_params("arbitrary", "arbitrary"),
        name="proj",
    )(x, positions.reshape(B, S, 1), scale1.reshape(B, 1, D), shift1.reshape(B, 1, D),
      w_fox, w_vt, ones_row, w_dil, w_ff, b_f, invf, sgn)


def _cum_kernel(lf_ref, c_ref, ct_ref):
    S = lf_ref.shape[1]
    r = lax.broadcasted_iota(jnp.int32, (LANES, LANES), 0)
    c = lax.broadcasted_iota(jnp.int32, (LANES, LANES), 1)
    tri = (c <= r).astype(F32)

    def body(j, carry):
        off = pl.multiple_of(j * LANES, LANES)
        cum = jnp.dot(tri, lf_ref[0, pl.ds(off, LANES), :], precision=HIGHEST,
                      preferred_element_type=F32) + carry
        c_ref[0, pl.ds(off, LANES), :] = cum
        ct_ref[0, :, pl.ds(off, LANES)] = cum.T[0:FOX_HEADS, :]
        return cum[LANES - 1:LANES, :]

    lax.fori_loop(0, S // LANES, body, jnp.zeros((1, LANES), F32))


def _cumulative_gate(log_f):
    B, S, _ = log_f.shape
    return pl.pallas_call(
        _cum_kernel,
        grid=(B,),
        in_specs=[pl.BlockSpec((1, S, LANES), lambda b: (b, 0, 0))],
        out_specs=[pl.BlockSpec((1, S, LANES), lambda b: (b, 0, 0)),
                   pl.BlockSpec((1, FOX_HEADS, S), lambda b: (b, 0, 0))],
        out_shape=[jax.ShapeDtypeStruct((B, S, LANES), F32),
                   jax.ShapeDtypeStruct((B, FOX_HEADS, S), F32)],
        compiler_params=_params("arbitrary"),
        name="cum",
    )(log_f)


def _fox_kernel(q_ref, k_ref, vt_ref, g_ref, ck_ref, cq_ref, o_ref, s_scr):
    S = q_ref.shape[1]
    T = FOX_TILE
    lane = lax.broadcasted_iota(jnp.int32, (T, LANES), 1)
    causal = (lax.broadcasted_iota(jnp.int32, (T, T), 0) <= lax.broadcasted_iota(jnp.int32, (T, T), 1))

    def q_block(i, _):
        qoff = pl.multiple_of(i * T, T)
        q = q_ref[0, pl.ds(qoff, T), :]
        qh = [jnp.where((lane // HEAD_DIM) == hh, q, jnp.zeros_like(q)) for hh in range(HEADS_PER_VREG)]
        cq = [cq_ref[0, 0, hh:hh + 1, pl.ds(qoff, T)] for hh in range(HEADS_PER_VREG)]

        def scores_into(buf, b):
            kb = k_ref[0, pl.ds(pl.multiple_of(b * T, T), T), :]
            for hh in range(HEADS_PER_VREG):
                s_scr[buf, hh] = lax.dot_general(kb, qh[hh], NT_DIMS, preferred_element_type=F32)

        def attend(buf, b, carry, masked):
            koff = pl.multiple_of(b * T, T)
            new = []
            for hh in range(HEADS_PER_VREG):
                m, acc = carry[hh]
                s = s_scr[buf, hh] - ck_ref[0, 0, pl.ds(koff, T), hh:hh + 1]
                if masked:
                    s = jnp.where(causal, s, NEG)
                m_new = jnp.maximum(m, cq[hh] + jnp.max(s, axis=0, keepdims=True))
                p = jnp.exp2((s + (cq[hh] - m_new)).astype(BF16))
                vt = vt_ref[0, hh * V_ROWS:(hh + 1) * V_ROWS, pl.ds(koff, T)]
                acc = jnp.exp2(m - m_new) * acc + jnp.dot(vt, p, preferred_element_type=F32)
                new.append((m_new, acc))
            return tuple(new)

        def pair(t, carry):
            scores_into(1, 2 * t + 1)
            carry = attend(0, 2 * t, carry, False)
            scores_into(0, 2 * t + 2)
            return attend(1, 2 * t + 1, carry, False)

        def odd_tail(carry):
            scores_into(1, i)
            return attend(1, i, attend(0, i - 1, carry, False), True)

        def even_tail(carry):
            return attend(0, i, carry, True)

        init = tuple((jnp.full((1, T), NEG, F32), jnp.zeros((V_ROWS, T), F32)) for _ in range(HEADS_PER_VREG))
        scores_into(0, 0)
        carry = lax.fori_loop(0, i // 2, pair, init)
        carry = lax.cond(i % 2 == 1, odd_tail, even_tail, carry)
        o_t = jnp.concatenate([acc[0:HEAD_DIM] * (1.0 / acc[HEAD_DIM:HEAD_DIM + 1]) for (_, acc) in carry], axis=0)
        gate = _sigmoid(g_ref[0, pl.ds(qoff, T), :].astype(F32))
        o_ref[0, pl.ds(qoff, T), :] = (o_t.T * gate).astype(BF16)
        return 0

    lax.fori_loop(0, S // T, q_block, 0)


def _fox_attention(pf, vt, ck, cq):
    B, S, _ = pf.shape
    nhp = FOX_W // LANES
    assert S % FOX_TILE == 0
    return pl.pallas_call(
        _fox_kernel,
        grid=(B, nhp),
        in_specs=[pl.BlockSpec((1, S, LANES), lambda b, h: (b, 0, h)),
                  pl.BlockSpec((1, S, LANES), lambda b, h: (b, 0, nhp + h)),
                  pl.BlockSpec((1, HEADS_PER_VREG * V_ROWS, S), lambda b, h: (b, h, 0)),
                  pl.BlockSpec((1, S, LANES), lambda b, h: (b, 0, 2 * nhp + h)),
                  pl.BlockSpec((1, 1, S, HEADS_PER_VREG), lambda b, h: (b, h, 0, 0)),
                  pl.BlockSpec((1, 1, HEADS_PER_VREG, S), lambda b, h: (b, h, 0, 0))],
        out_specs=pl.BlockSpec((1, S, LANES), lambda b, h: (b, 0, h)),
        out_shape=jax.ShapeDtypeStruct((B, S, FOX_W), BF16),
        scratch_shapes=[pltpu.VMEM((2, HEADS_PER_VREG, FOX_TILE, FOX_TILE), F32)],
        compiler_params=_params("arbitrary", "arbitrary"),
        name="fox",
    )(pf, pf, vt, pf, ck, cq)


def _dil_kernel(q_ref, k_ref, v_ref, o_ref, m_s, l_s, a_s):
    S = q_ref.shape[1]
    W = LANES
    lane = lax.broadcasted_iota(jnp.int32, (W, LANES), 1)
    head0 = lane < HEAD_DIM
    ri = lax.broadcasted_iota(jnp.int32, (2 * W, 2 * W), 0) % W
    ci = lax.broadcasted_iota(jnp.int32, (2 * W, 2 * W), 1)
    bias_rest = jnp.where((ci >= ri) & (ci <= ri + W), 0.0, NEG)
    bias_first = jnp.where((ci < W) & (ci <= ri), 0.0, NEG)
    ones = jnp.ones((2 * W, LANES), BF16)

    for p, (window, d) in enumerate(DILATED_PATTERNS):
        assert window // d == W and S % (d * W) == 0 and S // (d * W) >= 2 and (S // W) % DIL_UNROLL == 0
        nb = S // (d * W)

        def unit(u, p=p, d=d, nb=nb):
            r = u // nb
            n = u % nb
            kstart = jnp.maximum(n - 1, 0) * (W * d) + r
            if d == 1:
                qidx = pl.ds(pl.multiple_of(u * W, W), W)
                kidx = pl.ds(pl.multiple_of(kstart, W), 2 * W)
            else:
                qidx = pl.ds(n * (W * d) + r, W, stride=d)
                kidx = pl.ds(kstart, 2 * W, stride=d)
            q = q_ref[0, qidx, :]
            qs = jnp.concatenate([jnp.where(head0, q, 0.0), jnp.where(head0, 0.0, q)], axis=0).astype(BF16)
            kb = k_ref[0, kidx, :].astype(BF16)
            vb = jnp.concatenate([v_ref[0, kidx, :].astype(BF16), ones], axis=1)
            s = lax.dot_general(qs, kb, NT_DIMS, preferred_element_type=F32)
            s = s + jnp.where(n == 0, bias_first, bias_rest)
            m2 = jnp.max(s, axis=1, keepdims=True)
            out = jnp.dot(jnp.exp2((s - m2).astype(BF16)), vb, preferred_element_type=F32)
            a_u = jnp.where(head0, out[0:W, 0:LANES], out[W:2 * W, 0:LANES])
            l_u = jnp.where(head0, out[0:W, LANES:2 * LANES], out[W:2 * W, LANES:2 * LANES])
            m_u = jnp.where(head0, m2[0:W], m2[W:2 * W])
            if p == 0:
                m_s[qidx, :] = m_u
                l_s[qidx, :] = l_u
                a_s[qidx, :] = a_u
            else:
                m_o = m_s[qidx, :]
                m_n = jnp.maximum(m_o, m_u)
                e_o = jnp.exp2(m_o - m_n)
                e_u = jnp.exp2(m_u - m_n)
                m_s[qidx, :] = m_n
                l_s[qidx, :] = l_s[qidx, :] * e_o + l_u * e_u
                a_s[qidx, :] = a_s[qidx, :] * e_o + a_u * e_u

        def group(t, _, unit=unit):
            for uu in range(DIL_UNROLL):
                unit(t * DIL_UNROLL + uu)
            return 0

        lax.fori_loop(0, S // W // DIL_UNROLL, group, 0)

    o_ref[0] = (a_s[...] * (1.0 / l_s[...])).astype(BF16)


def _dilated_attention(pd):
    B, S, _ = pd.shape
    nhp = DIL_W // LANES
    return pl.pallas_call(
        _dil_kernel,
        grid=(B, nhp),
        in_specs=[pl.BlockSpec((1, S, LANES), lambda b, h: (b, 0, h)),
                  pl.BlockSpec((1, S, LANES), lambda b, h: (b, 0, nhp + h)),
                  pl.BlockSpec((1, S, LANES), lambda b, h: (b, 0, 2 * nhp + h))],
        out_specs=pl.BlockSpec((1, S, LANES), lambda b, h: (b, 0, h)),
        out_shape=jax.ShapeDtypeStruct((B, S, DIL_W), BF16),
        scratch_shapes=[pltpu.VMEM((S, LANES), F32)] * 3,
        compiler_params=_params("arbitrary", "arbitrary"),
        name="dil",
    )(pd, pd, pd)


def _out_kernel(of_ref, od_ref, x_ref, g1_ref, sc2_ref, sh2_ref, wo_ref, lng_ref, lnb_ref, wr_ref, br_ref,
                x1_ref, h2_ref, rt_ref, cnt_ref):
    tm = x_ref.shape[1]
    first_step = (pl.program_id(0) == 0) & (pl.program_id(1) == 0)

    @pl.when(first_step)
    def _():
        cnt_ref[...] = jnp.zeros_like(cnt_ref)

    y = (jnp.dot(of_ref[0], wo_ref[0:FOX_W, :], preferred_element_type=F32)
         + jnp.dot(od_ref[0], wo_ref[FOX_W:FOX_W + DIL_W, :], preferred_element_type=F32))
    x1 = _layer_norm(DEEPNORM_ALPHA * x_ref[0] + g1_ref[0] * y, lng_ref[...], lnb_ref[...])
    x1_ref[0] = x1
    h2 = x1 * (1.0 + sc2_ref[0]) + sh2_ref[0]
    h2_ref[0] = h2

    lg = jnp.dot(h2, wr_ref[...], precision=HIGHEST, preferred_element_type=F32) + br_ref[...]
    lane = lax.broadcasted_iota(jnp.int32, (tm, LANES), 1)
    lanef = lane.astype(F32)
    far = float(LANES)
    is_g = lane < N_GROUPS
    gl = jnp.where(is_g, lg, NEG)
    gmax = jnp.max(gl, axis=1, keepdims=True)
    gidx = jnp.min(jnp.where(gl == gmax, lanef, far), axis=1, keepdims=True)
    p_g = 1.0 / jnp.sum(jnp.where(is_g, jnp.exp(gl - gmax), 0.0), axis=1, keepdims=True)
    e_lo = N_GROUPS + gidx * EXPERTS_PER_GROUP
    in_group = (lanef >= e_lo) & (lanef < e_lo + EXPERTS_PER_GROUP)
    el = jnp.where(in_group, lg, NEG)
    v1 = jnp.max(el, axis=1, keepdims=True)
    i1 = jnp.min(jnp.where(in_group & (el == v1), lanef, far), axis=1, keepdims=True)
    rest = in_group & (lanef != i1)
    el2 = jnp.where(rest, lg, NEG)
    v2 = jnp.max(el2, axis=1, keepdims=True)
    i2 = jnp.min(jnp.where(rest & (el2 == v2), lanef, far), axis=1, keepdims=True)
    e21 = jnp.exp(v2 - v1)
    w1 = p_g / (1.0 + e21)
    w2 = p_g * e21 / (1.0 + e21)

    pick1 = lanef == i1
    pick2 = lanef == i2
    onehot = jnp.where(pick1 | pick2, 1.0, 0.0)
    rr = lax.broadcasted_iota(jnp.int32, (tm, tm), 0)
    cc = lax.broadcasted_iota(jnp.int32, (tm, tm), 1)
    strict_lower = jnp.where(cc < rr, 1.0, 0.0).astype(BF16)
    before = jnp.dot(strict_lower, onehot.astype(BF16), preferred_element_type=F32) + cnt_ref[...]
    r1 = jnp.sum(jnp.where(pick1, before, 0.0), axis=1, keepdims=True)
    r2 = jnp.sum(jnp.where(pick2, before, 0.0), axis=1, keepdims=True)
    cnt_ref[...] += jnp.sum(onehot, axis=0, keepdims=True)

    packed = jnp.zeros((tm, LANES), F32)
    for k, val in enumerate((i1 - N_GROUPS, i2 - N_GROUPS, w1, w2, r1, r2)):
        packed = jnp.where(lane == k, val, packed)
    rt_ref[0] = packed


def _out_and_route(of, od, x, gate1, scale2, shift2, w_out, ln_g, ln_b, w_rg, b_rg, w_re, b_re, tm):
    B, S, D = x.shape
    n_r = N_GROUPS + N_EXPERTS
    w_r = jnp.pad(jnp.concatenate([w_rg, w_re], axis=1), ((0, 0), (0, LANES - n_r)))
    b_r = jnp.pad(jnp.concatenate([b_rg, b_re]), (0, LANES - n_r)).reshape(1, LANES)
    tile = lambda w: pl.BlockSpec((1, tm, w), lambda b, i: (b, i, 0))
    per_batch = pl.BlockSpec((1, 1, D), lambda b, i: (b, 0, 0))
    const = lambda shape: pl.BlockSpec(shape, lambda b, i: (0,) * len(shape))
    return pl.pallas_call(
        _out_kernel,
        grid=(B, S // tm),
        in_specs=[tile(FOX_W), tile(DIL_W), tile(D), per_batch, per_batch, per_batch,
                  const((D, D)), const((1, D)), const((1, D)), const((D, LANES)), const((1, LANES))],
        out_specs=[tile(D), tile(D), tile(LANES), const((1, LANES))],
        out_shape=[jax.ShapeDtypeStruct((B, S, D), F32), jax.ShapeDtypeStruct((B, S, D), F32),
                   jax.ShapeDtypeStruct((B, S, LANES), F32), jax.ShapeDtypeStruct((1, LANES), F32)],
        compiler_params=_params("arbitrary", "arbitrary"),
        name="out",
    )(of, od, x, gate1.reshape(B, 1, D), scale2.reshape(B, 1, D), shift2.reshape(B, 1, D),
      w_out.astype(BF16), ln_g.reshape(1, D), ln_b.reshape(1, D), w_r, b_r)


def _row_copy(src_ref, dst_ref, sem, src_row, dst_row):
    return pltpu.make_async_copy(src_ref.at[pl.ds(src_row, 1), :], dst_ref.at[pl.ds(dst_row, 1), :], sem)


def _moe_kernel(blk_e_ref, row_tok_ref, h_ref, wg_ref, wu_ref, wd_ref, y_ref, xbuf, sem):
    del blk_e_ref
    base = pl.program_id(0) * MOE_BLOCK

    def issue(r, _):
        _row_copy(h_ref, xbuf, sem, row_tok_ref[base + r], r).start()
        return 0

    lax.fori_loop(0, MOE_BLOCK, issue, 0)
    pltpu.make_async_copy(h_ref.at[pl.ds(0, MOE_BLOCK), :], xbuf, sem).wait()
    xb = xbuf[...].astype(BF16)
    g = jnp.dot(xb, wg_ref[0], preferred_element_type=F32)
    u = jnp.dot(xb, wu_ref[0], preferred_element_type=F32)
    mid = (g * _sigmoid(g) * u).astype(BF16)
    y_ref[...] = jnp.dot(mid, wd_ref[0], preferred_element_type=F32)


def _expert_mlp(h2, row_tok, blk_e, w_gate, w_up, w_down):
    N, D = h2.shape
    nblk = blk_e.shape[0]
    return pl.pallas_call(
        _moe_kernel,
        grid_spec=pltpu.PrefetchScalarGridSpec(
            num_scalar_prefetch=2,
            grid=(nblk,),
            in_specs=[pl.BlockSpec(memory_space=pl.ANY),
                      pl.BlockSpec((1, D, D_EXPERT), lambda i, be, rt: (be[i], 0, 0)),
                      pl.BlockSpec((1, D, D_EXPERT), lambda i, be, rt: (be[i], 0, 0)),
                      pl.BlockSpec((1, D_EXPERT, D), lambda i, be, rt: (be[i], 0, 0))],
            out_specs=pl.BlockSpec((MOE_BLOCK, D), lambda i, be, rt: (i, 0)),
            scratch_shapes=[pltpu.VMEM((MOE_BLOCK, D), F32), pltpu.SemaphoreType.DMA(())]),
        out_shape=jax.ShapeDtypeStruct((nblk * MOE_BLOCK, D), F32),
        compiler_params=_params("arbitrary"),
        name="moe",
    )(blk_e, row_tok, h2, w_gate.astype(BF16), w_up.astype(BF16), w_down.astype(BF16))


def _final_kernel(dest_ref, y_ref, x1_ref, rt_ref, g2_ref, lng_ref, lnb_ref, o_ref, ybuf, sem):
    tm = x1_ref.shape[1]
    base = (pl.program_id(0) * pl.num_programs(1) + pl.program_id(1)) * tm

    def issue(r, _):
        for k in range(TOP_K):
            _row_copy(y_ref, ybuf.at[k], sem.at[k], dest_ref[(base + r) * TOP_K + k], r).start()
        return 0

    lax.fori_loop(0, tm, issue, 0)
    for k in range(TOP_K):
        pltpu.make_async_copy(y_ref.at[pl.ds(0, tm), :], ybuf.at[k], sem.at[k]).wait()
    rt = rt_ref[0]
    y = rt[:, 2:3] * ybuf[0] + rt[:, 3:4] * ybuf[1]
    o_ref[0] = _layer_norm(DEEPNORM_ALPHA * x1_ref[0] + g2_ref[0] * y, lng_ref[...], lnb_ref[...])


def _combine_and_norm(dest, y_rows, x1, route, gate2, ln_g, ln_b, tm):
    B, S, D = x1.shape
    return pl.pallas_call(
        _final_kernel,
        grid_spec=pltpu.PrefetchScalarGridSpec(
            num_scalar_prefetch=1,
            grid=(B, S // tm),
            in_specs=[pl.BlockSpec(memory_space=pl.ANY),
                      pl.BlockSpec((1, tm, D), lambda b, i, d: (b, i, 0)),
                      pl.BlockSpec((1, tm, LANES), lambda b, i, d: (b, i, 0)),
                      pl.BlockSpec((1, 1, D), lambda b, i, d: (b, 0, 0)),
                      pl.BlockSpec((1, D), lambda b, i, d: (0, 0)),
                      pl.BlockSpec((1, D), lambda b, i, d: (0, 0))],
            out_specs=pl.BlockSpec((1, tm, D), lambda b, i, d: (b, i, 0)),
            scratch_shapes=[pltpu.VMEM((TOP_K, tm, D), F32), pltpu.SemaphoreType.DMA((TOP_K,))]),
        out_shape=jax.ShapeDtypeStruct((B, S, D), F32),
        compiler_params=_params("arbitrary", "arbitrary"),
        name="final",
    )(dest, y_rows, x1, route, gate2.reshape(B, 1, D), ln_g.reshape(1, D), ln_b.reshape(1, D))


def kernel(x, c, positions, w_ada, b_ada, w_in, b_forget, w_out, ln1_g, ln1_b, w_router_group, b_router_group,
           w_router_expert, b_router_expert, w_up, w_gate, w_down, ln2_g, ln2_b):
    B, S, D = x.shape
    N = B * S
    assert D == FOX_W + DIL_W and S % 2048 == 0
    mod = _modulation(c, w_ada, b_ada)
    shift1, scale1, gate1, shift2, scale2, gate2 = jnp.split(mod, 6, axis=-1)

    pf, vt, pd, log_f = _projection(x, positions, scale1, shift1, w_in, b_forget, tm=512)
    cum, cum_t = _cumulative_gate(log_f)
    nhp = FOX_W // LANES
    ck = cum[..., :FOX_HEADS].reshape(B, S, nhp, HEADS_PER_VREG).transpose(0, 2, 1, 3)
    cq = cum_t.reshape(B, nhp, HEADS_PER_VREG, S)
    of = _fox_attention(pf, vt, ck, cq)
    od = _dilated_attention(pd)

    x1, h2, route, counts = _out_and_route(of, od, x, gate1, scale2, shift2, w_out, ln1_g, ln1_b,
                                           w_router_group, b_router_group, w_router_expert, b_router_expert,
                                           tm=512)

    route = route.reshape(N, LANES)
    eid = route[:, 0:TOP_K].astype(jnp.int32)
    rank = route[:, 4:4 + TOP_K].astype(jnp.int32)
    cnt = counts[0, N_GROUPS:N_GROUPS + N_EXPERTS].astype(jnp.int32)
    pcnt = (cnt + MOE_BLOCK - 1) // MOE_BLOCK * MOE_BLOCK
    pend = jnp.cumsum(pcnt)
    dest = ((pend - pcnt)[eid] + rank).reshape(N * TOP_K)
    nblk = (N * TOP_K) // MOE_BLOCK + N_EXPERTS
    tok = jnp.repeat(jnp.arange(N, dtype=jnp.int32), TOP_K)
    row_tok = jnp.full((nblk * MOE_BLOCK,), N - 1, jnp.int32).at[dest].set(tok)
    blk_e = jnp.minimum(jnp.searchsorted(pend, jnp.arange(nblk, dtype=jnp.int32) * MOE_BLOCK, side='right'),
                        N_EXPERTS - 1).astype(jnp.int32)

    y_rows = _expert_mlp(h2.reshape(N, D), row_tok, blk_e, w_gate, w_up, w_down)
    return _combine_and_norm(dest, y_rows, x1, route.reshape(B, S, LANES), gate2, ln2_g, ln2_b, tm=256)
```

```python
import functools

import jax
import jax.numpy as jnp
import numpy as np
from jax import lax
from jax.experimental import pallas as pl
from jax.experimental.pallas import tpu as pltpu

HEAD_DIM = 64
FOX_HEADS = 8
DIL_HEADS = 8
FOX_W = FOX_HEADS * HEAD_DIM
DIL_W = DIL_HEADS * HEAD_DIM
DILATED_PATTERNS = ((128, 1), (512, 4), (2048, 16))
ROPE_THETA = 500000.0
ROT_DIM = HEAD_DIM // 4
N_GROUPS = 4
EXPERTS_PER_GROUP = 8
N_EXPERTS = N_GROUPS * EXPERTS_PER_GROUP
TOP_K = 2
D_EXPERT = 512
MOE_BLOCK = 256
LN_EPS = 1e-5
NEG = -1e30
DEPTH = 1
DEEPNORM_ALPHA = (2 * DEPTH) ** 0.25
QK_SCALE = HEAD_DIM ** -0.5
LOG2E = 1.4426950408889634

LANES = 128
SUBLANES = 8
BF16_SUBLANES = 16
MXU_DIM = 256
HEADS_PER_VREG = LANES // HEAD_DIM
V_ROWS = HEAD_DIM + BF16_SUBLANES
FOX_TILE = 512
DIL_UNROLL = 4
VMEM_LIMIT = 56 * 1024 * 1024

F32 = jnp.float32
BF16 = jnp.bfloat16
HIGHEST = lax.Precision.HIGHEST
NT_DIMS = (((1,), (1,)), ((), ()))


def _params(*sem):
    return pltpu.CompilerParams(dimension_semantics=sem, vmem_limit_bytes=VMEM_LIMIT)


def _sigmoid(v):
    return 1.0 / (1.0 + jnp.exp(-v))


def _layer_norm(v, g, b):
    mu = jnp.mean(v, axis=-1, keepdims=True)
    d = v - mu
    var = jnp.mean(d * d, axis=-1, keepdims=True)
    return d * lax.rsqrt(var + LN_EPS) * g + b


def _mod_kernel(c_ref, w_ref, b_ref, o_ref):
    c = c_ref[...]
    o_ref[...] = jnp.dot(c * _sigmoid(c), w_ref[...], precision=HIGHEST,
                         preferred_element_type=F32) + b_ref[...]


def _modulation(c, w_ada, b_ada):
    B, D = c.shape
    cols = w_ada.shape[1]
    tn = 1024
    return pl.pallas_call(
        _mod_kernel,
        grid=(cols // tn,),
        in_specs=[pl.BlockSpec((B, D), lambda j: (0, 0)),
                  pl.BlockSpec((D, tn), lambda j: (0, j)),
                  pl.BlockSpec((1, tn), lambda j: (0, j))],
        out_specs=pl.BlockSpec((B, tn), lambda j: (0, j)),
        out_shape=jax.ShapeDtypeStruct((B, cols), F32),
        compiler_params=_params("arbitrary"),
        name="mod",
    )(c, w_ada, b_ada.reshape(1, cols))


def _proj_kernel(x_ref, pos_ref, sc_ref, sh_ref, wf_ref, wvt_ref, one_ref, wd_ref, wff_ref, bf_ref, invf_ref,
                 sgn_ref, pf_ref, vt_ref, pd_ref, lf_ref):
    tm = x_ref.shape[1]
    h = (x_ref[0] * (1.0 + sc_ref[0]) + sh_ref[0]).astype(BF16)

    for ci in range(3):
        acc = jnp.dot(h, wf_ref[:, ci * FOX_W:(ci + 1) * FOX_W], preferred_element_type=F32)
        if ci == 0:
            acc = acc * (QK_SCALE * LOG2E)
        pf_ref[0, :, ci * FOX_W:(ci + 1) * FOX_W] = acc.astype(BF16)
    vt = lax.dot_general(wvt_ref[...], h, NT_DIMS, preferred_element_type=F32) + one_ref[...]
    vt_ref[0] = vt.astype(BF16)

    z = jnp.dot(h, wff_ref[...], preferred_element_type=F32) + bf_ref[...]
    lf_ref[0] = (jnp.minimum(z, 0.0) - jnp.log1p(jnp.exp(-jnp.abs(z)))) * LOG2E

    ang = pos_ref[0].astype(F32) * invf_ref[...]
    cs = jnp.cos(ang)
    sn = jnp.sin(ang) * sgn_ref[...]
    lane = lax.broadcasted_iota(jnp.int32, (tm, LANES), 1)
    first_half = (lane % HEAD_DIM) < (ROT_DIM // 2)
    for ci in range(3):
        acc = jnp.dot(h, wd_ref[:, ci * DIL_W:(ci + 1) * DIL_W], preferred_element_type=F32)
        if ci == 2:
            pd_ref[0, :, ci * DIL_W:(ci + 1) * DIL_W] = acc
            continue
        for j in range(DIL_W // LANES):
            t = acc[:, j * LANES:(j + 1) * LANES]
            partner = jnp.where(first_half, pltpu.roll(t, LANES - ROT_DIM // 2, 1),
                                pltpu.roll(t, ROT_DIM // 2, 1))
            r = t * cs + partner * sn
            if ci == 0:
                r = r * (QK_SCALE * LOG2E)
            pd_ref[0, :, ci * DIL_W + j * LANES:ci * DIL_W + (j + 1) * LANES] = r


def _projection(x, positions, scale1, shift1, w_in, b_forget, tm):
    B, S, D = x.shape
    o = np.cumsum((0, FOX_W, FOX_W, FOX_W, FOX_W, FOX_HEADS, DIL_W, DIL_W, DIL_W))
    w_fox = jnp.concatenate([w_in[:, o[0]:o[2]], w_in[:, o[3]:o[4]]], axis=1).astype(BF16)
    w_vt = jnp.pad(w_in[:, o[2]:o[3]].T.reshape(FOX_HEADS, HEAD_DIM, D),
                   ((0, 0), (0, V_ROWS - HEAD_DIM), (0, 0))).reshape(FOX_HEADS * V_ROWS, D).astype(BF16)
    ones_row = jnp.asarray((np.arange(FOX_HEADS * V_ROWS) % V_ROWS == HEAD_DIM).astype(np.float32)
                           ).reshape(FOX_HEADS * V_ROWS, 1)
    w_ff = jnp.pad(w_in[:, o[4]:o[5]], ((0, 0), (0, LANES - FOX_HEADS))).astype(BF16)
    w_dil = w_in[:, o[5]:o[8]].astype(BF16)
    b_f = jnp.pad(b_forget, (0, LANES - FOX_HEADS)).reshape(1, LANES)
    e = np.arange(LANES) % HEAD_DIM
    inv_freq = ROPE_THETA ** (-jnp.arange(0, ROT_DIM, 2, dtype=F32) / ROT_DIM)
    invf = jnp.where(e < ROT_DIM, jnp.tile(inv_freq, LANES // (ROT_DIM // 2)), 0.0).reshape(1, LANES)
    sgn = jnp.asarray(np.where(e < ROT_DIM // 2, -1.0, np.where(e < ROT_DIM, 1.0, 0.0)), F32).reshape(1, LANES)
    const = lambda shape: pl.BlockSpec(shape, lambda b, i: (0,) * len(shape))
    return pl.pallas_call(
        _proj_kernel,
        grid=(B, S // tm),
        in_specs=[pl.BlockSpec((1, tm, D), lambda b, i: (b, i, 0)),
                  pl.BlockSpec((1, tm, 1), lambda b, i: (b, i, 0)),
                  pl.BlockSpec((1, 1, D), lambda b, i: (b, 0, 0)),
                  pl.BlockSpec((1, 1, D), lambda b, i: (b, 0, 0)),
                  const((D, 3 * FOX_W)), const((FOX_HEADS * V_ROWS, D)), const((FOX_HEADS * V_ROWS, 1)),
                  const((D, 3 * DIL_W)), const((D, LANES)),
                  const((1, LANES)), const((1, LANES)), const((1, LANES))],
        out_specs=[pl.BlockSpec((1, tm, 3 * FOX_W), lambda b, i: (b, i, 0)),
                   pl.BlockSpec((1, FOX_HEADS * V_ROWS, tm), lambda b, i: (b, 0, i)),
                   pl.BlockSpec((1, tm, 3 * DIL_W), lambda b, i: (b, i, 0)),
                   pl.BlockSpec((1, tm, LANES), lambda b, i: (b, i, 0))],
        out_shape=[jax.ShapeDtypeStruct((B, S, 3 * FOX_W), BF16),
                   jax.ShapeDtypeStruct((B, FOX_HEADS * V_ROWS, S), BF16),
                   jax.ShapeDtypeStruct((B, S, 3 * DIL_W), F32),
                   jax.ShapeDtypeStruct((B, S, LANES), F32)],
        compiler_params=_params("arbitrary", "arbitrary"),
        name="proj",
    )(x, positions.reshape(B, S, 1), scale1.reshape(B, 1, D), shift1.reshape(B, 1, D),
      w_fox, w_vt, ones_row, w_dil, w_ff, b_f, invf, sgn)


def _cum_kernel(lf_ref, c_ref, ct_ref):
    S = lf_ref.shape[1]
    r = lax.broadcasted_iota(jnp.int32, (LANES, LANES), 0)
    c = lax.broadcasted_iota(jnp.int32, (LANES, LANES), 1)
    tri = (c <= r).astype(F32)

    def body(j, carry):
        off = pl.multiple_of(j * LANES, LANES)
        cum = jnp.dot(tri, lf_ref[0, pl.ds(off, LANES), :], precision=HIGHEST,
                      preferred_element_type=F32) + carry
        c_ref[0, pl.ds(off, LANES), :] = cum
        ct_ref[0, :, pl.ds(off, LANES)] = cum.T[0:FOX_HEADS, :]
        return cum[LANES - 1:LANES, :]

    lax.fori_loop(0, S // LANES, body, jnp.zeros((1, LANES), F32))


def _cumulative_gate(log_f):
    B, S, _ = log_f.shape
    return pl.pallas_call(
        _cum_kernel,
        grid=(B,),
        in_specs=[pl.BlockSpec((1, S, LANES), lambda b: (b, 0, 0))],
        out_specs=[pl.BlockSpec((1, S, LANES), lambda b: (b, 0, 0)),
                   pl.BlockSpec((1, FOX_HEADS, S), lambda b: (b, 0, 0))],
        out_shape=[jax.ShapeDtypeStruct((B, S, LANES), F32),
                   jax.ShapeDtypeStruct((B, FOX_HEADS, S), F32)],
        compiler_params=_params("arbitrary"),
        name="cum",
    )(log_f)


def _fox_kernel(q_ref, k_ref, vt_ref, g_ref, ck_ref, cq_ref, o_ref, s_scr):
    S = q_ref.shape[1]
    T = FOX_TILE
    lane = lax.broadcasted_iota(jnp.int32, (T, LANES), 1)
    causal = (lax.broadcasted_iota(jnp.int32, (T, T), 0) <= lax.broadcasted_iota(jnp.int32, (T, T), 1))

    def q_block(i, _):
        qoff = pl.multiple_of(i * T, T)
        q = q_ref[0, pl.ds(qoff, T), :]
        qh = [jnp.where((lane // HEAD_DIM) == hh, q, jnp.zeros_like(q)) for hh in range(HEADS_PER_VREG)]
        cq = [cq_ref[0, 0, hh:hh + 1, pl.ds(qoff, T)] for hh in range(HEADS_PER_VREG)]

        def scores_into(buf, b):
            kb = k_ref[0, pl.ds(pl.multiple_of(b * T, T), T), :]
            for hh in range(HEADS_PER_VREG):
                s_scr[buf, hh] = lax.dot_general(kb, qh[hh], NT_DIMS, preferred_element_type=F32)

        def attend(buf, b, carry, masked):
            koff = pl.multiple_of(b * T, T)
            new = []
            for hh in range(HEADS_PER_VREG):
                m, acc = carry[hh]
                s = s_scr[buf, hh] - ck_ref[0, 0, pl.ds(koff, T), hh:hh + 1]
                if masked:
                    s = jnp.where(causal, s, NEG)
                m_new = jnp.maximum(m, cq[hh] + jnp.max(s, axis=0, keepdims=True))
                p = jnp.exp2((s + (cq[hh] - m_new)).astype(BF16))
                vt = vt_ref[0, hh * V_ROWS:(hh + 1) * V_ROWS, pl.ds(koff, T)]
                acc = jnp.exp2(m - m_new) * acc + jnp.dot(vt, p, preferred_element_type=F32)
                new.append((m_new, acc))
            return tuple(new)

        def pair(t, carry):
            scores_into(1, 2 * t + 1)
            carry = attend(0, 2 * t, carry, False)
            scores_into(0, 2 * t + 2)
            return attend(1, 2 * t + 1, carry, False)

        def odd_tail(carry):
            scores_into(1, i)
            return attend(1, i, attend(0, i - 1, carry, False), True)

        def even_tail(carry):
            return attend(0, i, carry, True)

        init = tuple((jnp.full((1, T), NEG, F32), jnp.zeros((V_ROWS, T), F32)) for _ in range(HEADS_PER_VREG))
        scores_into(0, 0)
        carry = lax.fori_loop(0, i // 2, pair, init)
        carry = lax.cond(i % 2 == 1, odd_tail, even_tail, carry)
        o_t = jnp.concatenate([acc[0:HEAD_DIM] * (1.0 / acc[HEAD_DIM:HEAD_DIM + 1]) for (_, acc) in carry], axis=0)
        gate = _sigmoid(g_ref[0, pl.ds(qoff, T), :].astype(F32))
        o_ref[0, pl.ds(qoff, T), :] = (o_t.T * gate).astype(BF16)
        return 0

    lax.fori_loop(0, S // T, q_block, 0)


def _fox_attention(pf, vt, ck, cq):
    B, S, _ = pf.shape
    nhp = FOX_W // LANES
    assert S % FOX_TILE == 0
    return pl.pallas_call(
        _fox_kernel,
        grid=(B, nhp),
        in_specs=[pl.BlockSpec((1, S, LANES), lambda b, h: (b, 0, h)),
                  pl.BlockSpec((1, S, LANES), lambda b, h: (b, 0, nhp + h)),
                  pl.BlockSpec((1, HEADS_PER_VREG * V_ROWS, S), lambda b, h: (b, h, 0)),
                  pl.BlockSpec((1, S, LANES), lambda b, h: (b, 0, 2 * nhp + h)),
                  pl.BlockSpec((1, 1, S, HEADS_PER_VREG), lambda b, h: (b, h, 0, 0)),
                  pl.BlockSpec((1, 1, HEADS_PER_VREG, S), lambda b, h: (b, h, 0, 0))],
        out_specs=pl.BlockSpec((1, S, LANES), lambda b, h: (b, 0, h)),
        out_shape=jax.ShapeDtypeStruct((B, S, FOX_W), BF16),
        scratch_shapes=[pltpu.VMEM((2, HEADS_PER_VREG, FOX_TILE, FOX_TILE), F32)],
        compiler_params=_params("arbitrary", "arbitrary"),
        name="fox",
    )(pf, pf, vt, pf, ck, cq)


def _dil_kernel(q_ref, k_ref, v_ref, o_ref, m_s, l_s, a_s):
    S = q_ref.shape[1]
    W = LANES
    lane = lax.broadcasted_iota(jnp.int32, (W, LANES), 1)
    head0 = lane < HEAD_DIM
    ri = lax.broadcasted_iota(jnp.int32, (2 * W, 2 * W), 0) % W
    ci = lax.broadcasted_iota(jnp.int32, (2 * W, 2 * W), 1)
    bias_rest = jnp.where((ci >= ri) & (ci <= ri + W), 0.0, NEG)
    bias_first = jnp.where((ci < W) & (ci <= ri), 0.0, NEG)
    ones = jnp.ones((2 * W, LANES), BF16)

    for p, (window, d) in enumerate(DILATED_PATTERNS):
        assert window // d == W and S % (d * W) == 0 and S // (d * W) >= 2 and (S // W) % DIL_UNROLL == 0
        nb = S // (d * W)

        def unit(u, p=p, d=d, nb=nb):
            r = u // nb
            n = u % nb
            kstart = jnp.maximum(n - 1, 0) * (W * d) + r
            if d == 1:
                qidx = pl.ds(pl.multiple_of(u * W, W), W)
                kidx = pl.ds(pl.multiple_of(kstart, W), 2 * W)
            else:
                qidx = pl.ds(n * (W * d) + r, W, stride=d)
                kidx = pl.ds(kstart, 2 * W, stride=d)
            q = q_ref[0, qidx, :]
            qs = jnp.concatenate([jnp.where(head0, q, 0.0), jnp.where(head0, 0.0, q)], axis=0).astype(BF16)
            kb = k_ref[0, kidx, :].astype(BF16)
            vb = jnp.concatenate([v_ref[0, kidx, :].astype(BF16), ones], axis=1)
            s = lax.dot_general(qs, kb, NT_DIMS, preferred_element_type=F32)
            s = s + jnp.where(n == 0, bias_first, bias_rest)
            m2 = jnp.max(s, axis=1, keepdims=True)
            out = jnp.dot(jnp.exp2((s - m2).astype(BF16)), vb, preferred_element_type=F32)
            a_u = jnp.where(head0, out[0:W, 0:LANES], out[W:2 * W, 0:LANES])
            l_u = jnp.where(head0, out[0:W, LANES:2 * LANES], out[W:2 * W, LANES:2 * LANES])
            m_u = jnp.where(head0, m2[0:W], m2[W:2 * W])
            if p == 0:
                m_s[qidx, :] = m_u
                l_s[qidx, :] = l_u
                a_s[qidx, :] = a_u
            else:
                m_o = m_s[qidx, :]
                m_n = jnp.maximum(m_o, m_u)
                e_o = jnp.exp2(m_o - m_n)
                e_u = jnp.exp2(m_u - m_n)
                m_s[qidx, :] = m_n
                l_s[qidx, :] = l_s[qidx, :] * e_o + l_u * e_u
                a_s[qidx, :] = a_s[qidx, :] * e_o + a_u * e_u

        def group(t, _, unit=unit):
            for uu in range(DIL_UNROLL):
                unit(t * DIL_UNROLL + uu)
            return 0

        lax.fori_loop(0, S // W // DIL_UNROLL, group, 0)

    o_ref[0] = (a_s[...] * (1.0 / l_s[...])).astype(BF16)


def _dilated_attention(pd):
    B, S, _ = pd.shape
    nhp = DIL_W // LANES
    return pl.pallas_call(
        _dil_kernel,
        grid=(B, nhp),
        in_specs=[pl.BlockSpec((1, S, LANES), lambda b, h: (b, 0, h)),
                  pl.BlockSpec((1, S, LANES), lambda b, h: (b, 0, nhp + h)),
                  pl.BlockSpec((1, S, LANES), lambda b, h: (b, 0, 2 * nhp + h))],
        out_specs=pl.BlockSpec((1, S, LANES), lambda b, h: (b, 0, h)),
        out_shape=jax.ShapeDtypeStruct((B, S, DIL_W), BF16),
        scratch_shapes=[pltpu.VMEM((S, LANES), F32)] * 3,
        compiler_params=_params("arbitrary", "arbitrary"),
        name="dil",
    )(pd, pd, pd)


def _out_kernel(of_ref, od_ref, x_ref, g1_ref, sc2_ref, sh2_ref, wo_ref, lng_ref, lnb_ref, wr_ref, br_ref,
                x1_ref, h2_ref, rt_ref, cnt_ref):
    tm = x_ref.shape[1]
    first_step = (pl.program_id(0) == 0) & (pl.program_id(1) == 0)

    @pl.when(first_step)
    def _():
        cnt_ref[...] = jnp.zeros_like(cnt_ref)

    y = (jnp.dot(of_ref[0], wo_ref[0:FOX_W, :], preferred_element_type=F32)
         + jnp.dot(od_ref[0], wo_ref[FOX_W:FOX_W + DIL_W, :], preferred_element_type=F32))
    x1 = _layer_norm(DEEPNORM_ALPHA * x_ref[0] + g1_ref[0] * y, lng_ref[...], lnb_ref[...])
    x1_ref[0] = x1
    h2 = x1 * (1.0 + sc2_ref[0]) + sh2_ref[0]
    for s in range(SUBLANES):
        h2_ref[pl.ds(s, tm, stride=SUBLANES), :] = h2[:, s * LANES:(s + 1) * LANES]

    lg = jnp.dot(h2, wr_ref[...], precision=HIGHEST, preferred_element_type=F32) + br_ref[...]
    lane = lax.broadcasted_iota(jnp.int32, (tm, LANES), 1)
    lanef = lane.astype(F32)
    far = float(LANES)
    is_g = lane < N_GROUPS
    gl = jnp.where(is_g, lg, NEG)
    gmax = jnp.max(gl, axis=1, keepdims=True)
    gidx = jnp.min(jnp.where(gl == gmax, lanef, far), axis=1, keepdims=True)
    p_g = 1.0 / jnp.sum(jnp.where(is_g, jnp.exp(gl - gmax), 0.0), axis=1, keepdims=True)
    e_lo = N_GROUPS + gidx * EXPERTS_PER_GROUP
    in_group = (lanef >= e_lo) & (lanef < e_lo + EXPERTS_PER_GROUP)
    el = jnp.where(in_group, lg, NEG)
    v1 = jnp.max(el, axis=1, keepdims=True)
    i1 = jnp.min(jnp.where(in_group & (el == v1), lanef, far), axis=1, keepdims=True)
    rest = in_group & (lanef != i1)
    el2 = jnp.where(rest, lg, NEG)
    v2 = jnp.max(el2, axis=1, keepdims=True)
    i2 = jnp.min(jnp.where(rest & (el2 == v2), lanef, far), axis=1, keepdims=True)
    e21 = jnp.exp(v2 - v1)
    w1 = p_g / (1.0 + e21)
    w2 = p_g * e21 / (1.0 + e21)

    pick1 = lanef == i1
    pick2 = lanef == i2
    onehot = jnp.where(pick1 | pick2, 1.0, 0.0)
    rr = lax.broadcasted_iota(jnp.int32, (tm, tm), 0)
    cc = lax.broadcasted_iota(jnp.int32, (tm, tm), 1)
    strict_lower = jnp.where(cc < rr, 1.0, 0.0).astype(BF16)
    before = jnp.dot(strict_lower, onehot.astype(BF16), preferred_element_type=F32) + cnt_ref[...]
    r1 = jnp.sum(jnp.where(pick1, before, 0.0), axis=1, keepdims=True)
    r2 = jnp.sum(jnp.where(pick2, before, 0.0), axis=1, keepdims=True)
    cnt_ref[...] += jnp.sum(onehot, axis=0, keepdims=True)

    packed = jnp.zeros((tm, LANES), F32)
    for k, val in enumerate((i1 - N_GROUPS, i2 - N_GROUPS, w1, w2, r1, r2)):
        packed = jnp.where(lane == k, val, packed)
    rt_ref[0] = packed


def _out_and_route(of, od, x, gate1, scale2, shift2, w_out, ln_g, ln_b, w_rg, b_rg, w_re, b_re, tm):
    B, S, D = x.shape
    n_r = N_GROUPS + N_EXPERTS
    w_r = jnp.pad(jnp.concatenate([w_rg, w_re], axis=1), ((0, 0), (0, LANES - n_r)))
    b_r = jnp.pad(jnp.concatenate([b_rg, b_re]), (0, LANES - n_r)).reshape(1, LANES)
    tile = lambda w: pl.BlockSpec((1, tm, w), lambda b, i: (b, i, 0))
    per_batch = pl.BlockSpec((1, 1, D), lambda b, i: (b, 0, 0))
    const = lambda shape: pl.BlockSpec(shape, lambda b, i: (0,) * len(shape))
    return pl.pallas_call(
        _out_kernel,
        grid=(B, S // tm),
        in_specs=[tile(FOX_W), tile(DIL_W), tile(D), per_batch, per_batch, per_batch,
                  const((D, D)), const((1, D)), const((1, D)), const((D, LANES)), const((1, LANES))],
        out_specs=[tile(D), pl.BlockSpec((tm * SUBLANES, LANES), lambda b, i: (b * (S // tm) + i, 0)),
                   tile(LANES), const((1, LANES))],
        out_shape=[jax.ShapeDtypeStruct((B, S, D), F32), jax.ShapeDtypeStruct((B * S * SUBLANES, LANES), F32),
                   jax.ShapeDtypeStruct((B, S, LANES), F32), jax.ShapeDtypeStruct((1, LANES), F32)],
        compiler_params=_params("arbitrary", "arbitrary"),
        name="out",
    )(of, od, x, gate1.reshape(B, 1, D), scale2.reshape(B, 1, D), shift2.reshape(B, 1, D),
      w_out.astype(BF16), ln_g.reshape(1, D), ln_b.reshape(1, D), w_r, b_r)


def _row_copy(src_ref, dst_ref, sem, src_row, dst_row):
    return pltpu.make_async_copy(src_ref.at[pl.ds(pl.multiple_of(src_row * SUBLANES, SUBLANES), SUBLANES), :],
                                 dst_ref.at[pl.ds(pl.multiple_of(dst_row * SUBLANES, SUBLANES), SUBLANES), :], sem)


def _moe_kernel(blk_e_ref, nused_ref, row_tok_ref, row_slot_ref, h_ref, wg_ref, wu_ref, wd_ref, yt_ref,
                xbuf0, xbuf1, ybuf0, ybuf1, mid_s, gsem, ssem):
    del blk_e_ref
    i = pl.program_id(0)
    nused = nused_ref[0]
    D = xbuf0.shape[0] // MOE_BLOCK * LANES
    nk = D // MXU_DIM

    def gather_rows(block, xdst, s):
        base = block * MOE_BLOCK
        for r in range(MOE_BLOCK):
            _row_copy(h_ref, xdst, gsem.at[s], row_tok_ref[base + r], r).start(priority=r % 2)

    def scatter_rows(block, ysrc, s):
        base = block * MOE_BLOCK
        for r in range(MOE_BLOCK):
            _row_copy(ysrc, yt_ref, ssem.at[s], r, row_slot_ref[base + r]).start(priority=r % 2)

    def wait_gather(xdst, s):
        pltpu.make_async_copy(h_ref.at[pl.ds(0, MOE_BLOCK * SUBLANES), :], xdst, gsem.at[s]).wait()

    def wait_scatter(ysrc, s):
        pltpu.make_async_copy(ysrc, yt_ref.at[pl.ds(0, MOE_BLOCK * SUBLANES), :], ssem.at[s]).wait()

    def compute(xcur, ycur):
        xk = [jnp.concatenate([xcur[pl.ds(2 * j, MOE_BLOCK, stride=SUBLANES), :],
                               xcur[pl.ds(2 * j + 1, MOE_BLOCK, stride=SUBLANES), :]], axis=1).astype(BF16)
              for j in range(nk)]
        for c in range(D_EXPERT // MXU_DIM):
            cols = slice(c * MXU_DIM, (c + 1) * MXU_DIM)
            g = sum(jnp.dot(xk[j], wg_ref[0, MXU_DIM * j:MXU_DIM * (j + 1), cols].astype(BF16),
                            preferred_element_type=F32) for j in range(nk))
            u = sum(jnp.dot(xk[j], wu_ref[0, MXU_DIM * j:MXU_DIM * (j + 1), cols].astype(BF16),
                            preferred_element_type=F32) for j in range(nk))
            mid_s[:, cols] = (g * _sigmoid(g) * u).astype(BF16)
        mid = mid_s[...]
        for c in range(D // MXU_DIM):
            y = jnp.dot(mid, wd_ref[0, :, c * MXU_DIM:(c + 1) * MXU_DIM].astype(BF16), preferred_element_type=F32)
            ycur[pl.ds(2 * c, MOE_BLOCK, stride=SUBLANES), :] = y[:, 0:LANES]
            ycur[pl.ds(2 * c + 1, MOE_BLOCK, stride=SUBLANES), :] = y[:, LANES:2 * LANES]

    def step(s, xcur, xnxt, ycur, yprv):
        o = 1 - s
        wait_gather(xcur, s)

        @pl.when(i >= 2)
        def _():
            wait_scatter(ycur, s)

        @pl.when(i + 1 < nused)
        def _():
            gather_rows(i + 1, xnxt, o)

        @pl.when(i >= 1)
        def _():
            scatter_rows(i - 1, yprv, o)

        compute(xcur, ycur)

        @pl.when(i + 1 == nused)
        def _():
            scatter_rows(i, ycur, s)

            @pl.when(i >= 1)
            def _():
                wait_scatter(yprv, o)

            wait_scatter(ycur, s)

    @pl.when(i == 0)
    def _():
        gather_rows(0, xbuf0, 0)

    @pl.when((i < nused) & (i % 2 == 0))
    def _():
        step(0, xbuf0, xbuf1, ybuf0, ybuf1)

    @pl.when((i < nused) & (i % 2 == 1))
    def _():
        step(1, xbuf1, xbuf0, ybuf1, ybuf0)

    @pl.when(i >= nused)
    def _():
        ybuf0[...] = jnp.zeros_like(ybuf0)
        first = pl.multiple_of(row_slot_ref[i * MOE_BLOCK] * SUBLANES, SUBLANES)
        fill = pltpu.make_async_copy(ybuf0, yt_ref.at[pl.ds(first, MOE_BLOCK * SUBLANES), :], ssem.at[0])
        fill.start()
        fill.wait()


def _expert_mlp(h2_rows, row_tok, row_slot, blk_e, nused, w_gate, w_up, w_down, n_out_rows):
    D = w_gate.shape[1]
    nblk = blk_e.shape[0]
    wmap = lambda i, be, nu, rt, rs: (be[i], 0, 0)
    return pl.pallas_call(
        _moe_kernel,
        grid_spec=pltpu.PrefetchScalarGridSpec(
            num_scalar_prefetch=4,
            grid=(nblk,),
            in_specs=[pl.BlockSpec(memory_space=pl.ANY),
                      pl.BlockSpec((1, D, D_EXPERT), wmap),
                      pl.BlockSpec((1, D, D_EXPERT), wmap),
                      pl.BlockSpec((1, D_EXPERT, D), wmap)],
            out_specs=pl.BlockSpec(memory_space=pl.ANY),
            scratch_shapes=[pltpu.VMEM((MOE_BLOCK * SUBLANES, LANES), F32)] * 4
            + [pltpu.VMEM((MOE_BLOCK, D_EXPERT), BF16),
               pltpu.SemaphoreType.DMA((2,)), pltpu.SemaphoreType.DMA((2,))]),
        out_shape=jax.ShapeDtypeStruct((n_out_rows * SUBLANES, LANES), F32),
        compiler_params=_params("arbitrary"),
        name="moe",
    )(blk_e, nused, row_tok, row_slot, h2_rows, w_gate, w_up, w_down)


def _final_kernel(y0_ref, y1_ref, x1_ref, rt_ref, g2_ref, lng_ref, lnb_ref, o_ref):
    tm = x1_ref.shape[1]
    rt = rt_ref[0]
    w0 = rt[:, 2:3]
    w1 = rt[:, 3:4]
    y = jnp.concatenate([w0 * y0_ref[pl.ds(s, tm, stride=SUBLANES), :] + w1 * y1_ref[pl.ds(s, tm, stride=SUBLANES), :]
                         for s in range(SUBLANES)], axis=1)
    o_ref[0] = _layer_norm(DEEPNORM_ALPHA * x1_ref[0] + g2_ref[0] * y, lng_ref[...], lnb_ref[...])


def _combine_and_norm(y_rows, x1, route, gate2, ln_g, ln_b, tm):
    B, S, D = x1.shape
    nt = S // tm
    return pl.pallas_call(
        _final_kernel,
        grid=(B, nt),
        in_specs=[pl.BlockSpec((tm * SUBLANES, LANES), lambda b, i: (b * nt + i, 0)),
                  pl.BlockSpec((tm * SUBLANES, LANES), lambda b, i: (B * nt + b * nt + i, 0)),
                  pl.BlockSpec((1, tm, D), lambda b, i: (b, i, 0)),
                  pl.BlockSpec((1, tm, LANES), lambda b, i: (b, i, 0)),
                  pl.BlockSpec((1, 1, D), lambda b, i: (b, 0, 0)),
                  pl.BlockSpec((1, D), lambda b, i: (0, 0)),
                  pl.BlockSpec((1, D), lambda b, i: (0, 0))],
        out_specs=pl.BlockSpec((1, tm, D), lambda b, i: (b, i, 0)),
        out_shape=jax.ShapeDtypeStruct((B, S, D), F32),
        compiler_params=_params("arbitrary", "arbitrary"),
        name="final",
    )(y_rows, y_rows, x1, route, gate2.reshape(B, 1, D), ln_g.reshape(1, D), ln_b.reshape(1, D))


def kernel(x, c, positions, w_ada, b_ada, w_in, b_forget, w_out, ln1_g, ln1_b, w_router_group, b_router_group,
           w_router_expert, b_router_expert, w_up, w_gate, w_down, ln2_g, ln2_b):
    B, S, D = x.shape
    N = B * S
    assert D == FOX_W + DIL_W and S % 2048 == 0
    mod = _modulation(c, w_ada, b_ada)
    shift1, scale1, gate1, shift2, scale2, gate2 = jnp.split(mod, 6, axis=-1)

    pf, vt, pd, log_f = _projection(x, positions, scale1, shift1, w_in, b_forget, tm=512)
    cum, cum_t = _cumulative_gate(log_f)
    nhp = FOX_W // LANES
    ck = cum[..., :FOX_HEADS].reshape(B, S, nhp, HEADS_PER_VREG).transpose(0, 2, 1, 3)
    cq = cum_t.reshape(B, nhp, HEADS_PER_VREG, S)
    of = _fox_attention(pf, vt, ck, cq)
    od = _dilated_attention(pd)

    x1, h2, route, counts = _out_and_route(of, od, x, gate1, scale2, shift2, w_out, ln1_g, ln1_b,
                                           w_router_group, b_router_group, w_router_expert, b_router_expert,
                                           tm=512)

    route = route.reshape(N, LANES)
    eid = route[:, 0:TOP_K].astype(jnp.int32)
    rank = route[:, 4:4 + TOP_K].astype(jnp.int32)
    cnt = counts[0, N_GROUPS:N_GROUPS + N_EXPERTS].astype(jnp.int32)
    pcnt = (cnt + MOE_BLOCK - 1) // MOE_BLOCK * MOE_BLOCK
    pend = jnp.cumsum(pcnt)
    experts = jnp.arange(N_EXPERTS, dtype=jnp.int32)
    poff = jnp.sum(jnp.where(eid[..., None] == experts, pend - pcnt, 0), axis=-1)
    dest = (poff + rank).reshape(N * TOP_K)
    nblk = (N * TOP_K) // MOE_BLOCK + N_EXPERTS
    n_rows = nblk * MOE_BLOCK
    slot = (jnp.arange(N, dtype=jnp.int32)[:, None] + N * jnp.arange(TOP_K, dtype=jnp.int32)[None, :]).reshape(-1)
    row_slot = jnp.full((n_rows,), -1, jnp.int32).at[dest].set(slot)
    is_pad = row_slot < 0
    row_slot = jnp.where(is_pad, N * TOP_K - 1 + jnp.cumsum(is_pad.astype(jnp.int32)), row_slot)
    row_tok = jnp.where(is_pad, N - 1, row_slot % N)
    nused = pend[-1:] // MOE_BLOCK
    blk_start = jnp.arange(nblk, dtype=jnp.int32) * MOE_BLOCK
    blk_e = jnp.sum((pend[None, :] <= jnp.minimum(blk_start, pend[-1] - 1)[:, None]).astype(jnp.int32), axis=1)

    y_rows = _expert_mlp(h2, row_tok, row_slot, blk_e, nused, w_gate, w_up, w_down, n_rows)
    return _combine_and_norm(y_rows, x1, route.reshape(B, S, LANES), gate2, ln2_g, ln2_b, tm=512)
```

```python
import functools

import jax
import jax.numpy as jnp
import numpy as np
from jax import lax
from jax.experimental import pallas as pl
from jax.experimental.pallas import tpu as pltpu

HEAD_DIM = 64
FOX_HEADS = 8
DIL_HEADS = 8
FOX_W = FOX_HEADS * HEAD_DIM
DIL_W = DIL_HEADS * HEAD_DIM
DILATED_PATTERNS = ((128, 1), (512, 4), (2048, 16))
ROPE_THETA = 500000.0
ROT_DIM = HEAD_DIM // 4
N_GROUPS = 4
EXPERTS_PER_GROUP = 8
N_EXPERTS = N_GROUPS * EXPERTS_PER_GROUP
TOP_K = 2
D_EXPERT = 512
MOE_BLOCK = 256
LN_EPS = 1e-5
NEG = -1e30
DEPTH = 1
DEEPNORM_ALPHA = (2 * DEPTH) ** 0.25
QK_SCALE = HEAD_DIM ** -0.5
LOG2E = 1.4426950408889634

LANES = 128
SUBLANES = 8
BF16_SUBLANES = 16
MXU_DIM = 256
HEADS_PER_VREG = LANES // HEAD_DIM
V_ROWS = HEAD_DIM + BF16_SUBLANES
FOX_TILE = 512
DIL_UNROLL = 4
VMEM_LIMIT = 56 * 1024 * 1024

F32 = jnp.float32
BF16 = jnp.bfloat16
HIGHEST = lax.Precision.HIGHEST
NT_DIMS = (((1,), (1,)), ((), ()))


def _params(*sem):
    return pltpu.CompilerParams(dimension_semantics=sem, vmem_limit_bytes=VMEM_LIMIT)


def _sigmoid(v):
    return 1.0 / (1.0 + jnp.exp(-v))


def _layer_norm(v, g, b):
    mu = jnp.mean(v, axis=-1, keepdims=True)
    d = v - mu
    var = jnp.mean(d * d, axis=-1, keepdims=True)
    return d * lax.rsqrt(var + LN_EPS) * g + b


def _mod_kernel(c_ref, w_ref, b_ref, o_ref):
    c = c_ref[...]
    o_ref[...] = jnp.dot(c * _sigmoid(c), w_ref[...], precision=HIGHEST,
                         preferred_element_type=F32) + b_ref[...]


def _modulation(c, w_ada, b_ada):
    B, D = c.shape
    cols = w_ada.shape[1]
    tn = 1024
    return pl.pallas_call(
        _mod_kernel,
        grid=(cols // tn,),
        in_specs=[pl.BlockSpec((B, D), lambda j: (0, 0)),
                  pl.BlockSpec((D, tn), lambda j: (0, j)),
                  pl.BlockSpec((1, tn), lambda j: (0, j))],
        out_specs=pl.BlockSpec((B, tn), lambda j: (0, j)),
        out_shape=jax.ShapeDtypeStruct((B, cols), F32),
        compiler_params=_params("arbitrary"),
        name="mod",
    )(c, w_ada, b_ada.reshape(1, cols))


def _proj_kernel(x_ref, pos_ref, sc_ref, sh_ref, wf_ref, wvt_ref, one_ref, wd_ref, wff_ref, bf_ref, invf_ref,
                 sgn_ref, pf_ref, vt_ref, pd_ref, lf_ref):
    tm = x_ref.shape[1]
    h = (x_ref[0] * (1.0 + sc_ref[0]) + sh_ref[0]).astype(BF16)

    for ci in range(3):
        acc = jnp.dot(h, wf_ref[:, ci * FOX_W:(ci + 1) * FOX_W], preferred_element_type=F32)
        if ci == 0:
            acc = acc * (QK_SCALE * LOG2E)
        pf_ref[0, :, ci * FOX_W:(ci + 1) * FOX_W] = acc.astype(BF16)
    vt = lax.dot_general(wvt_ref[...], h, NT_DIMS, preferred_element_type=F32) + one_ref[...]
    vt_ref[0] = vt.astype(BF16)

    z = jnp.dot(h, wff_ref[...], preferred_element_type=F32) + bf_ref[...]
    lf_ref[0] = (jnp.minimum(z, 0.0) - jnp.log1p(jnp.exp(-jnp.abs(z)))) * LOG2E

    ang = pos_ref[0].astype(F32) * invf_ref[...]
    cs = jnp.cos(ang)
    sn = jnp.sin(ang) * sgn_ref[...]
    lane = lax.broadcasted_iota(jnp.int32, (tm, LANES), 1)
    first_half = (lane % HEAD_DIM) < (ROT_DIM // 2)
    for ci in range(3):
        acc = jnp.dot(h, wd_ref[:, ci * DIL_W:(ci + 1) * DIL_W], preferred_element_type=F32)
        if ci == 2:
            pd_ref[0, :, ci * DIL_W:(ci + 1) * DIL_W] = acc
            continue
        for j in range(DIL_W // LANES):
            t = acc[:, j * LANES:(j + 1) * LANES]
            partner = jnp.where(first_half, pltpu.roll(t, LANES - ROT_DIM // 2, 1),
                                pltpu.roll(t, ROT_DIM // 2, 1))
            r = t * cs + partner * sn
            if ci == 0:
                r = r * (QK_SCALE * LOG2E)
            pd_ref[0, :, ci * DIL_W + j * LANES:ci * DIL_W + (j + 1) * LANES] = r


def _projection(x, positions, scale1, shift1, w_in, b_forget, tm):
    B, S, D = x.shape
    o = np.cumsum((0, FOX_W, FOX_W, FOX_W, FOX_W, FOX_HEADS, DIL_W, DIL_W, DIL_W))
    w_fox = jnp.concatenate([w_in[:, o[0]:o[2]], w_in[:, o[3]:o[4]]], axis=1).astype(BF16)
    w_vt = jnp.pad(w_in[:, o[2]:o[3]].T.reshape(FOX_HEADS, HEAD_DIM, D),
                   ((0, 0), (0, V_ROWS - HEAD_DIM), (0, 0))).reshape(FOX_HEADS * V_ROWS, D).astype(BF16)
    ones_row = jnp.asarray((np.arange(FOX_HEADS * V_ROWS) % V_ROWS == HEAD_DIM).astype(np.float32)
                           ).reshape(FOX_HEADS * V_ROWS, 1)
    w_ff = jnp.pad(w_in[:, o[4]:o[5]], ((0, 0), (0, LANES - FOX_HEADS))).astype(BF16)
    w_dil = w_in[:, o[5]:o[8]].astype(BF16)
    b_f = jnp.pad(b_forget, (0, LANES - FOX_HEADS)).reshape(1, LANES)
    e = np.arange(LANES) % HEAD_DIM
    inv_freq = ROPE_THETA ** (-jnp.arange(0, ROT_DIM, 2, dtype=F32) / ROT_DIM)
    invf = jnp.where(e < ROT_DIM, jnp.tile(inv_freq, LANES // (ROT_DIM // 2)), 0.0).reshape(1, LANES)
    sgn = jnp.asarray(np.where(e < ROT_DIM // 2, -1.0, np.where(e < ROT_DIM, 1.0, 0.0)), F32).reshape(1, LANES)
    const = lambda shape: pl.BlockSpec(shape, lambda b, i: (0,) * len(shape))
    return pl.pallas_call(
        _proj_kernel,
        grid=(B, S // tm),
        in_specs=[pl.BlockSpec((1, tm, D), lambda b, i: (b, i, 0)),
                  pl.BlockSpec((1, tm, 1), lambda b, i: (b, i, 0)),
                  pl.BlockSpec((1, 1, D), lambda b, i: (b, 0, 0)),
                  pl.BlockSpec((1, 1, D), lambda b, i: (b, 0, 0)),
                  const((D, 3 * FOX_W)), const((FOX_HEADS * V_ROWS, D)), const((FOX_HEADS * V_ROWS, 1)),
                  const((D, 3 * DIL_W)), const((D, LANES)),
                  const((1, LANES)), const((1, LANES)), const((1, LANES))],
        out_specs=[pl.BlockSpec((1, tm, 3 * FOX_W), lambda b, i: (b, i, 0)),
                   pl.BlockSpec((1, FOX_HEADS * V_ROWS, tm), lambda b, i: (b, 0, i)),
                   pl.BlockSpec((1, tm, 3 * DIL_W), lambda b, i: (b, i, 0)),
                   pl.BlockSpec((1, tm, LANES), lambda b, i: (b, i, 0))],
        out_shape=[jax.ShapeDtypeStruct((B, S, 3 * FOX_W), BF16),
                   jax.ShapeDtypeStruct((B, FOX_HEADS * V_ROWS, S), BF16),
                   jax.ShapeDtypeStruct((B, S, 3 * DIL_W), F32),
                   jax.ShapeDtypeStruct((B, S, LANES), F32)],
        compiler_params=_params("arbitrary", "arbitrary"),
        name="proj",
    )(x, positions.reshape(B, S, 1), scale1.reshape(B, 1, D), shift1.reshape(B, 1, D),
      w_fox, w_vt, ones_row, w_dil, w_ff, b_f, invf, sgn)


def _cum_kernel(lf_ref, c_ref, ct_ref):
    S = lf_ref.shape[1]
    r = lax.broadcasted_iota(jnp.int32, (LANES, LANES), 0)
    c = lax.broadcasted_iota(jnp.int32, (LANES, LANES), 1)
    tri = (c <= r).astype(F32)

    def body(j, carry):
        off = pl.multiple_of(j * LANES, LANES)
        cum = jnp.dot(tri, lf_ref[0, pl.ds(off, LANES), :], precision=HIGHEST,
                      preferred_element_type=F32) + carry
        c_ref[0, pl.ds(off, LANES), :] = cum
        ct_ref[0, :, pl.ds(off, LANES)] = cum.T[0:FOX_HEADS, :]
        return cum[LANES - 1:LANES, :]

    lax.fori_loop(0, S // LANES, body, jnp.zeros((1, LANES), F32))


def _cumulative_gate(log_f):
    B, S, _ = log_f.shape
    return pl.pallas_call(
        _cum_kernel,
        grid=(B,),
        in_specs=[pl.BlockSpec((1, S, LANES), lambda b: (b, 0, 0))],
        out_specs=[pl.BlockSpec((1, S, LANES), lambda b: (b, 0, 0)),
                   pl.BlockSpec((1, FOX_HEADS, S), lambda b: (b, 0, 0))],
        out_shape=[jax.ShapeDtypeStruct((B, S, LANES), F32),
                   jax.ShapeDtypeStruct((B, FOX_HEADS, S), F32)],
        compiler_params=_params("arbitrary"),
        name="cum",
    )(log_f)


def _fox_kernel(q_ref, k_ref, vt_ref, g_ref, ck_ref, cq_ref, o_ref, s_scr):
    S = q_ref.shape[1]
    T = FOX_TILE
    lane = lax.broadcasted_iota(jnp.int32, (T, LANES), 1)
    causal = (lax.broadcasted_iota(jnp.int32, (T, T), 0) <= lax.broadcasted_iota(jnp.int32, (T, T), 1))

    def q_block(i, _):
        qoff = pl.multiple_of(i * T, T)
        q = q_ref[0, pl.ds(qoff, T), :]
        qh = [jnp.where((lane // HEAD_DIM) == hh, q, jnp.zeros_like(q)) for hh in range(HEADS_PER_VREG)]
        cq = [cq_ref[0, 0, hh:hh + 1, pl.ds(qoff, T)] for hh in range(HEADS_PER_VREG)]

        def scores_into(buf, b):
            kb = k_ref[0, pl.ds(pl.multiple_of(b * T, T), T), :]
            for hh in range(HEADS_PER_VREG):
                s_scr[buf, hh] = lax.dot_general(kb, qh[hh], NT_DIMS, preferred_element_type=F32)

        def attend(buf, b, carry, masked):
            koff = pl.multiple_of(b * T, T)
            new = []
            for hh in range(HEADS_PER_VREG):
                m, acc = carry[hh]
                s = s_scr[buf, hh] - ck_ref[0, 0, pl.ds(koff, T), hh:hh + 1]
                if masked:
                    s = jnp.where(causal, s, NEG)
                m_new = jnp.maximum(m, cq[hh] + jnp.max(s, axis=0, keepdims=True))
                p = jnp.exp2((s + (cq[hh] - m_new)).astype(BF16))
                vt = vt_ref[0, hh * V_ROWS:(hh + 1) * V_ROWS, pl.ds(koff, T)]
                acc = jnp.exp2(m - m_new) * acc + jnp.dot(vt, p, preferred_element_type=F32)
                new.append((m_new, acc))
            return tuple(new)

        def pair(t, carry):
            scores_into(1, 2 * t + 1)
            carry = attend(0, 2 * t, carry, False)
            scores_into(0, 2 * t + 2)
            return attend(1, 2 * t + 1, carry, False)

        def odd_tail(carry):
            scores_into(1, i)
            return attend(1, i, attend(0, i - 1, carry, False), True)

        def even_tail(carry):
            return attend(0, i, carry, True)

        init = tuple((jnp.full((1, T), NEG, F32), jnp.zeros((V_ROWS, T), F32)) for _ in range(HEADS_PER_VREG))
        scores_into(0, 0)
        carry = lax.fori_loop(0, i // 2, pair, init)
        carry = lax.cond(i % 2 == 1, odd_tail, even_tail, carry)
        o_t = jnp.concatenate([acc[0:HEAD_DIM] * (1.0 / acc[HEAD_DIM:HEAD_DIM + 1]) for (_, acc) in carry], axis=0)
        gate = _sigmoid(g_ref[0, pl.ds(qoff, T), :].astype(F32))
        o_ref[0, pl.ds(qoff, T), :] = (o_t.T * gate).astype(BF16)
        return 0

    lax.fori_loop(0, S // T, q_block, 0)


def _fox_attention(pf, vt, ck, cq):
    B, S, _ = pf.shape
    nhp = FOX_W // LANES
    assert S % FOX_TILE == 0
    return pl.pallas_call(
        _fox_kernel,
        grid=(B, nhp),
        in_specs=[pl.BlockSpec((1, S, LANES), lambda b, h: (b, 0, h)),
                  pl.BlockSpec((1, S, LANES), lambda b, h: (b, 0, nhp + h)),
                  pl.BlockSpec((1, HEADS_PER_VREG * V_ROWS, S), lambda b, h: (b, h, 0)),
                  pl.BlockSpec((1, S, LANES), lambda b, h: (b, 0, 2 * nhp + h)),
                  pl.BlockSpec((1, 1, S, HEADS_PER_VREG), lambda b, h: (b, h, 0, 0)),
                  pl.BlockSpec((1, 1, HEADS_PER_VREG, S), lambda b, h: (b, h, 0, 0))],
        out_specs=pl.BlockSpec((1, S, LANES), lambda b, h: (b, 0, h)),
        out_shape=jax.ShapeDtypeStruct((B, S, FOX_W), BF16),
        scratch_shapes=[pltpu.VMEM((2, HEADS_PER_VREG, FOX_TILE, FOX_TILE), F32)],
        compiler_params=_params("arbitrary", "arbitrary"),
        name="fox",
    )(pf, pf, vt, pf, ck, cq)


def _dil_kernel(q_ref, k_ref, v_ref, o_ref, m_s, l_s, a_s):
    S = q_ref.shape[1]
    W = LANES
    lane = lax.broadcasted_iota(jnp.int32, (W, LANES), 1)
    head0 = lane < HEAD_DIM
    ri = lax.broadcasted_iota(jnp.int32, (2 * W, 2 * W), 0) % W
    ci = lax.broadcasted_iota(jnp.int32, (2 * W, 2 * W), 1)
    bias_rest = jnp.where((ci >= ri) & (ci <= ri + W), 0.0, NEG)
    bias_first = jnp.where((ci < W) & (ci <= ri), 0.0, NEG)
    ones = jnp.ones((2 * W, LANES), BF16)

    for p, (window, d) in enumerate(DILATED_PATTERNS):
        assert window // d == W and S % (d * W) == 0 and S // (d * W) >= 2 and (S // W) % DIL_UNROLL == 0
        nb = S // (d * W)

        def unit(u, p=p, d=d, nb=nb):
            r = u // nb
            n = u % nb
            kstart = jnp.maximum(n - 1, 0) * (W * d) + r
            if d == 1:
                qidx = pl.ds(pl.multiple_of(u * W, W), W)
                kidx = pl.ds(pl.multiple_of(kstart, W), 2 * W)
            else:
                qidx = pl.ds(n * (W * d) + r, W, stride=d)
                kidx = pl.ds(kstart, 2 * W, stride=d)
            q = q_ref[0, qidx, :]
            qs = jnp.concatenate([jnp.where(head0, q, 0.0), jnp.where(head0, 0.0, q)], axis=0).astype(BF16)
            kb = k_ref[0, kidx, :].astype(BF16)
            vb = jnp.concatenate([v_ref[0, kidx, :].astype(BF16), ones], axis=1)
            s = lax.dot_general(qs, kb, NT_DIMS, preferred_element_type=F32)
            s = s + jnp.where(n == 0, bias_first, bias_rest)
            m2 = jnp.max(s, axis=1, keepdims=True)
            out = jnp.dot(jnp.exp2((s - m2).astype(BF16)), vb, preferred_element_type=F32)
            a_u = jnp.where(head0, out[0:W, 0:LANES], out[W:2 * W, 0:LANES])
            l_u = jnp.where(head0, out[0:W, LANES:2 * LANES], out[W:2 * W, LANES:2 * LANES])
            m_u = jnp.where(head0, m2[0:W], m2[W:2 * W])
            if p == 0:
                m_s[qidx, :] = m_u
                l_s[qidx, :] = l_u
                a_s[qidx, :] = a_u
            else:
                m_o = m_s[qidx, :]
                m_n = jnp.maximum(m_o, m_u)
                e_o = jnp.exp2(m_o - m_n)
                e_u = jnp.exp2(m_u - m_n)
                m_s[qidx, :] = m_n
                l_s[qidx, :] = l_s[qidx, :] * e_o + l_u * e_u
                a_s[qidx, :] = a_s[qidx, :] * e_o + a_u * e_u

        def group(t, _, unit=unit):
            for uu in range(DIL_UNROLL):
                unit(t * DIL_UNROLL + uu)
            return 0

        lax.fori_loop(0, S // W // DIL_UNROLL, group, 0)

    o_ref[0] = (a_s[...] * (1.0 / l_s[...])).astype(BF16)


def _dilated_attention(pd):
    B, S, _ = pd.shape
    nhp = DIL_W // LANES
    return pl.pallas_call(
        _dil_kernel,
        grid=(B, nhp),
        in_specs=[pl.BlockSpec((1, S, LANES), lambda b, h: (b, 0, h)),
                  pl.BlockSpec((1, S, LANES), lambda b, h: (b, 0, nhp + h)),
                  pl.BlockSpec((1, S, LANES), lambda b, h: (b, 0, 2 * nhp + h))],
        out_specs=pl.BlockSpec((1, S, LANES), lambda b, h: (b, 0, h)),
        out_shape=jax.ShapeDtypeStruct((B, S, DIL_W), BF16),
        scratch_shapes=[pltpu.VMEM((S, LANES), F32)] * 3,
        compiler_params=_params("arbitrary", "arbitrary"),
        name="dil",
    )(pd, pd, pd)


def _out_kernel(of_ref, od_ref, x_ref, g1_ref, sc2_ref, sh2_ref, wo_ref, lng_ref, lnb_ref, wrh_ref, wrl_ref, br_ref,
                x1_ref, h2_ref, rt_ref, cnt_ref):
    tm = x_ref.shape[1]
    first_step = (pl.program_id(0) == 0) & (pl.program_id(1) == 0)

    @pl.when(first_step)
    def _():
        cnt_ref[...] = jnp.zeros_like(cnt_ref)

    y = (jnp.dot(of_ref[0], wo_ref[0:FOX_W, :], preferred_element_type=F32)
         + jnp.dot(od_ref[0], wo_ref[FOX_W:FOX_W + DIL_W, :], preferred_element_type=F32))
    x1 = _layer_norm(DEEPNORM_ALPHA * x_ref[0] + g1_ref[0] * y, lng_ref[...], lnb_ref[...])
    x1_ref[0] = x1
    h2 = x1 * (1.0 + sc2_ref[0]) + sh2_ref[0]
    for s in range(SUBLANES):
        h2_ref[pl.ds(s, tm, stride=SUBLANES), :] = h2[:, s * LANES:(s + 1) * LANES]

    h_hi = h2.astype(BF16)
    h_lo = (h2 - h_hi.astype(F32)).astype(BF16)
    lg = (jnp.dot(h_hi, wrh_ref[...], preferred_element_type=F32)
          + jnp.dot(h_lo, wrh_ref[...], preferred_element_type=F32)
          + jnp.dot(h_hi, wrl_ref[...], preferred_element_type=F32)) + br_ref[...]
    lane = lax.broadcasted_iota(jnp.int32, (tm, LANES), 1)
    lanef = lane.astype(F32)
    far = float(LANES)
    is_g = lane < N_GROUPS
    gl = jnp.where(is_g, lg, NEG)
    gmax = jnp.max(gl, axis=1, keepdims=True)
    gidx = jnp.min(jnp.where(gl == gmax, lanef, far), axis=1, keepdims=True)
    p_g = 1.0 / jnp.sum(jnp.where(is_g, jnp.exp(gl - gmax), 0.0), axis=1, keepdims=True)
    e_lo = N_GROUPS + gidx * EXPERTS_PER_GROUP
    in_group = (lanef >= e_lo) & (lanef < e_lo + EXPERTS_PER_GROUP)
    el = jnp.where(in_group, lg, NEG)
    v1 = jnp.max(el, axis=1, keepdims=True)
    i1 = jnp.min(jnp.where(in_group & (el == v1), lanef, far), axis=1, keepdims=True)
    rest = in_group & (lanef != i1)
    el2 = jnp.where(rest, lg, NEG)
    v2 = jnp.max(el2, axis=1, keepdims=True)
    i2 = jnp.min(jnp.where(rest & (el2 == v2), lanef, far), axis=1, keepdims=True)
    e21 = jnp.exp(v2 - v1)
    w1 = p_g / (1.0 + e21)
    w2 = p_g * e21 / (1.0 + e21)

    pick1 = lanef == i1
    pick2 = lanef == i2
    onehot = jnp.where(pick1 | pick2, 1.0, 0.0)
    rr = lax.broadcasted_iota(jnp.int32, (tm, tm), 0)
    cc = lax.broadcasted_iota(jnp.int32, (tm, tm), 1)
    strict_lower = jnp.where(cc < rr, 1.0, 0.0).astype(BF16)
    before = jnp.dot(strict_lower, onehot.astype(BF16), preferred_element_type=F32) + cnt_ref[...]
    r1 = jnp.sum(jnp.where(pick1, before, 0.0), axis=1, keepdims=True)
    r2 = jnp.sum(jnp.where(pick2, before, 0.0), axis=1, keepdims=True)
    cnt_ref[...] += jnp.sum(onehot, axis=0, keepdims=True)

    packed = jnp.zeros((tm, LANES), F32)
    for k, val in enumerate((i1 - N_GROUPS, i2 - N_GROUPS, w1, w2, r1, r2)):
        packed = jnp.where(lane == k, val, packed)
    rt_ref[0] = packed


def _out_and_route(of, od, x, gate1, scale2, shift2, w_out, ln_g, ln_b, w_rg, b_rg, w_re, b_re, tm):
    B, S, D = x.shape
    n_r = N_GROUPS + N_EXPERTS
    w_r = jnp.pad(jnp.concatenate([w_rg, w_re], axis=1), ((0, 0), (0, LANES - n_r)))
    b_r = jnp.pad(jnp.concatenate([b_rg, b_re]), (0, LANES - n_r)).reshape(1, LANES)
    w_r_hi = w_r.astype(BF16)
    w_r_lo = (w_r - w_r_hi.astype(F32)).astype(BF16)
    tile = lambda w: pl.BlockSpec((1, tm, w), lambda b, i: (b, i, 0))
    per_batch = pl.BlockSpec((1, 1, D), lambda b, i: (b, 0, 0))
    const = lambda shape: pl.BlockSpec(shape, lambda b, i: (0,) * len(shape))
    return pl.pallas_call(
        _out_kernel,
        grid=(B, S // tm),
        in_specs=[tile(FOX_W), tile(DIL_W), tile(D), per_batch, per_batch, per_batch,
                  const((D, D)), const((1, D)), const((1, D)), const((D, LANES)), const((D, LANES)),
                  const((1, LANES))],
        out_specs=[tile(D), pl.BlockSpec((tm * SUBLANES, LANES), lambda b, i: (b * (S // tm) + i, 0)),
                   tile(LANES), const((1, LANES))],
        out_shape=[jax.ShapeDtypeStruct((B, S, D), F32), jax.ShapeDtypeStruct((B * S * SUBLANES, LANES), F32),
                   jax.ShapeDtypeStruct((B, S, LANES), F32), jax.ShapeDtypeStruct((1, LANES), F32)],
        compiler_params=_params("arbitrary", "arbitrary"),
        name="out",
    )(of, od, x, gate1.reshape(B, 1, D), scale2.reshape(B, 1, D), shift2.reshape(B, 1, D),
      w_out.astype(BF16), ln_g.reshape(1, D), ln_b.reshape(1, D), w_r_hi, w_r_lo, b_r)


def _row_copy(src_ref, dst_ref, sem, src_row, dst_row):
    return pltpu.make_async_copy(src_ref.at[pl.ds(pl.multiple_of(src_row * SUBLANES, SUBLANES), SUBLANES), :],
                                 dst_ref.at[pl.ds(pl.multiple_of(dst_row * SUBLANES, SUBLANES), SUBLANES), :], sem)


def _moe_kernel(blk_e_ref, nused_ref, row_tok_ref, row_slot_ref, h_ref, wg_ref, wu_ref, wd_ref, yt_ref,
                xbuf0, xbuf1, ybuf0, ybuf1, mid_s, gsem, ssem):
    del blk_e_ref
    i = pl.program_id(0)
    nused = nused_ref[0]
    D = xbuf0.shape[0] // MOE_BLOCK * LANES
    nk = D // MXU_DIM

    def gather_rows(block, xdst, s):
        base = block * MOE_BLOCK
        for r in range(MOE_BLOCK):
            _row_copy(h_ref, xdst, gsem.at[s], row_tok_ref[base + r], r).start(priority=r % 2)

    def scatter_rows(block, ysrc, s):
        base = block * MOE_BLOCK
        for r in range(MOE_BLOCK):
            _row_copy(ysrc, yt_ref, ssem.at[s], r, row_slot_ref[base + r]).start(priority=r % 2)

    def wait_gather(xdst, s):
        pltpu.make_async_copy(h_ref.at[pl.ds(0, MOE_BLOCK * SUBLANES), :], xdst, gsem.at[s]).wait()

    def wait_scatter(ysrc, s):
        pltpu.make_async_copy(ysrc, yt_ref.at[pl.ds(0, MOE_BLOCK * SUBLANES), :], ssem.at[s]).wait()

    def compute(xcur, ycur):
        xk = [jnp.concatenate([xcur[pl.ds(2 * j, MOE_BLOCK, stride=SUBLANES), :],
                               xcur[pl.ds(2 * j + 1, MOE_BLOCK, stride=SUBLANES), :]], axis=1).astype(BF16)
              for j in range(nk)]
        for c in range(D_EXPERT // MXU_DIM):
            cols = slice(c * MXU_DIM, (c + 1) * MXU_DIM)
            g = sum(jnp.dot(xk[j], wg_ref[0, MXU_DIM * j:MXU_DIM * (j + 1), cols].astype(BF16),
                            preferred_element_type=F32) for j in range(nk))
            u = sum(jnp.dot(xk[j], wu_ref[0, MXU_DIM * j:MXU_DIM * (j + 1), cols].astype(BF16),
                            preferred_element_type=F32) for j in range(nk))
            mid_s[:, cols] = (g * _sigmoid(g) * u).astype(BF16)
        mid = mid_s[...]
        for c in range(D // MXU_DIM):
            y = jnp.dot(mid, wd_ref[0, :, c * MXU_DIM:(c + 1) * MXU_DIM].astype(BF16), preferred_element_type=F32)
            ycur[pl.ds(2 * c, MOE_BLOCK, stride=SUBLANES), :] = y[:, 0:LANES]
            ycur[pl.ds(2 * c + 1, MOE_BLOCK, stride=SUBLANES), :] = y[:, LANES:2 * LANES]

    def step(s, xcur, xnxt, ycur, yprv):
        o = 1 - s
        wait_gather(xcur, s)

        @pl.when(i >= 2)
        def _():
            wait_scatter(ycur, s)

        @pl.when(i + 1 < nused)
        def _():
            gather_rows(i + 1, xnxt, o)

        @pl.when(i >= 1)
        def _():
            scatter_rows(i - 1, yprv, o)

        compute(xcur, ycur)

        @pl.when(i + 1 == nused)
        def _():
            scatter_rows(i, ycur, s)

            @pl.when(i >= 1)
            def _():
                wait_scatter(yprv, o)

            wait_scatter(ycur, s)

    @pl.when(i == 0)
    def _():
        gather_rows(0, xbuf0, 0)

    @pl.when((i < nused) & (i % 2 == 0))
    def _():
        step(0, xbuf0, xbuf1, ybuf0, ybuf1)

    @pl.when((i < nused) & (i % 2 == 1))
    def _():
        step(1, xbuf1, xbuf0, ybuf1, ybuf0)

    @pl.when(i >= nused)
    def _():
        ybuf0[...] = jnp.zeros_like(ybuf0)
        first = pl.multiple_of(row_slot_ref[i * MOE_BLOCK] * SUBLANES, SUBLANES)
        fill = pltpu.make_async_copy(ybuf0, yt_ref.at[pl.ds(first, MOE_BLOCK * SUBLANES), :], ssem.at[0])
        fill.start()
        fill.wait()


def _expert_mlp(h2_rows, row_tok, row_slot, blk_e, nused, w_gate, w_up, w_down, n_out_rows):
    D = w_gate.shape[1]
    nblk = blk_e.shape[0]
    wmap = lambda i, be, nu, rt, rs: (be[i], 0, 0)
    return pl.pallas_call(
        _moe_kernel,
        grid_spec=pltpu.PrefetchScalarGridSpec(
            num_scalar_prefetch=4,
            grid=(nblk,),
            in_specs=[pl.BlockSpec(memory_space=pl.ANY),
                      pl.BlockSpec((1, D, D_EXPERT), wmap),
                      pl.BlockSpec((1, D, D_EXPERT), wmap),
                      pl.BlockSpec((1, D_EXPERT, D), wmap)],
            out_specs=pl.BlockSpec(memory_space=pl.ANY),
            scratch_shapes=[pltpu.VMEM((MOE_BLOCK * SUBLANES, LANES), F32)] * 4
            + [pltpu.VMEM((MOE_BLOCK, D_EXPERT), BF16),
               pltpu.SemaphoreType.DMA((2,)), pltpu.SemaphoreType.DMA((2,))]),
        out_shape=jax.ShapeDtypeStruct((n_out_rows * SUBLANES, LANES), F32),
        compiler_params=_params("arbitrary"),
        name="moe",
    )(blk_e, nused, row_tok, row_slot, h2_rows, w_gate, w_up, w_down)


PLAN_CHUNK = 8192
PLAN_UNROLL = 8


def _plan_kernel(cnt_ref, poff_ref, dest_ref, slot_ref):
    g = pl.program_id(0)
    n_assign = pl.num_programs(0) * PLAN_CHUNK

    @pl.when(g == 0)
    def _():
        def expert(e, nxt):
            lo = poff_ref[e] + cnt_ref[e]

            def pad_row(p, _):
                slot_ref[p] = nxt + (p - lo)
                return 0

            lax.fori_loop(lo, poff_ref[e + 1], pad_row, 0)
            return nxt + (poff_ref[e + 1] - lo)

        lax.fori_loop(0, N_EXPERTS, expert, n_assign)

    def body(j, _):
        for u in range(PLAN_UNROLL):
            a = j * PLAN_UNROLL + u
            slot_ref[dest_ref[a]] = g * PLAN_CHUNK + a
        return 0

    lax.fori_loop(0, PLAN_CHUNK // PLAN_UNROLL, body, 0)


def _dispatch_plan(dest, cnt, seg_start, n_rows):
    n_assign = dest.shape[0]
    assert n_assign % PLAN_CHUNK == 0
    return pl.pallas_call(
        _plan_kernel,
        grid_spec=pltpu.PrefetchScalarGridSpec(
            num_scalar_prefetch=2,
            grid=(n_assign // PLAN_CHUNK,),
            in_specs=[pl.BlockSpec((PLAN_CHUNK,), lambda g, c, p: (g,), memory_space=pltpu.SMEM)],
            out_specs=pl.BlockSpec((n_rows,), lambda g, c, p: (0,), memory_space=pltpu.SMEM)),
        out_shape=jax.ShapeDtypeStruct((n_rows,), jnp.int32),
        compiler_params=_params("arbitrary"),
        name="plan",
    )(cnt, seg_start, dest)


def _final_kernel(y0_ref, y1_ref, x1_ref, rt_ref, g2_ref, lng_ref, lnb_ref, o_ref):
    tm = x1_ref.shape[1]
    rt = rt_ref[0]
    w0 = rt[:, 2:3]
    w1 = rt[:, 3:4]
    y = jnp.concatenate([w0 * y0_ref[pl.ds(s, tm, stride=SUBLANES), :] + w1 * y1_ref[pl.ds(s, tm, stride=SUBLANES), :]
                         for s in range(SUBLANES)], axis=1)
    o_ref[0] = _layer_norm(DEEPNORM_ALPHA * x1_ref[0] + g2_ref[0] * y, lng_ref[...], lnb_ref[...])


def _combine_and_norm(y_rows, x1, route, gate2, ln_g, ln_b, tm):
    B, S, D = x1.shape
    nt = S // tm
    return pl.pallas_call(
        _final_kernel,
        grid=(B, nt),
        in_specs=[pl.BlockSpec((tm * SUBLANES, LANES), lambda b, i: (b * nt + i, 0)),
                  pl.BlockSpec((tm * SUBLANES, LANES), lambda b, i: (B * nt + b * nt + i, 0)),
                  pl.BlockSpec((1, tm, D), lambda b, i: (b, i, 0)),
                  pl.BlockSpec((1, tm, LANES), lambda b, i: (b, i, 0)),
                  pl.BlockSpec((1, 1, D), lambda b, i: (b, 0, 0)),
                  pl.BlockSpec((1, D), lambda b, i: (0, 0)),
                  pl.BlockSpec((1, D), lambda b, i: (0, 0))],
        out_specs=pl.BlockSpec((1, tm, D), lambda b, i: (b, i, 0)),
        out_shape=jax.ShapeDtypeStruct((B, S, D), F32),
        compiler_params=_params("arbitrary", "arbitrary"),
        name="final",
    )(y_rows, y_rows, x1, route, gate2.reshape(B, 1, D), ln_g.reshape(1, D), ln_b.reshape(1, D))


def kernel(x, c, positions, w_ada, b_ada, w_in, b_forget, w_out, ln1_g, ln1_b, w_router_group, b_router_group,
           w_router_expert, b_router_expert, w_up, w_gate, w_down, ln2_g, ln2_b):
    B, S, D = x.shape
    N = B * S
    assert D == FOX_W + DIL_W and S % 2048 == 0
    mod = _modulation(c, w_ada, b_ada)
    shift1, scale1, gate1, shift2, scale2, gate2 = jnp.split(mod, 6, axis=-1)

    pf, vt, pd, log_f = _projection(x, positions, scale1, shift1, w_in, b_forget, tm=512)
    cum, cum_t = _cumulative_gate(log_f)
    nhp = FOX_W // LANES
    ck = cum[..., :FOX_HEADS].reshape(B, S, nhp, HEADS_PER_VREG).transpose(0, 2, 1, 3)
    cq = cum_t.reshape(B, nhp, HEADS_PER_VREG, S)
    of = _fox_attention(pf, vt, ck, cq)
    od = _dilated_attention(pd)

    x1, h2, route, counts = _out_and_route(of, od, x, gate1, scale2, shift2, w_out, ln1_g, ln1_b,
                                           w_router_group, b_router_group, w_router_expert, b_router_expert,
                                           tm=512)

    route = route.reshape(N, LANES)
    eid = route[:, 0:TOP_K].astype(jnp.int32)
    rank = route[:, 4:4 + TOP_K].astype(jnp.int32)
    cnt = counts[0, N_GROUPS:N_GROUPS + N_EXPERTS].astype(jnp.int32)
    pcnt = (cnt + MOE_BLOCK - 1) // MOE_BLOCK * MOE_BLOCK
    pend = jnp.cumsum(pcnt)
    experts = jnp.arange(N_EXPERTS, dtype=jnp.int32)
    poff = jnp.sum(jnp.where(eid[..., None] == experts, pend - pcnt, 0), axis=-1)
    dest = (poff + rank).T.reshape(N * TOP_K)
    nblk = (N * TOP_K) // MOE_BLOCK + N_EXPERTS
    n_rows = nblk * MOE_BLOCK
    seg_start = jnp.concatenate([pend - pcnt, jnp.full((1,), n_rows, jnp.int32)])
    row_slot = _dispatch_plan(dest, cnt, seg_start, n_rows)
    row_tok = jnp.where(row_slot >= N * TOP_K, N - 1, row_slot % N)
    nused = pend[-1:] // MOE_BLOCK
    blk_start = jnp.arange(nblk, dtype=jnp.int32) * MOE_BLOCK
    blk_e = jnp.sum((pend[None, :] <= jnp.minimum(blk_start, pend[-1] - 1)[:, None]).astype(jnp.int32), axis=1)

    y_rows = _expert_mlp(h2, row_tok, row_slot, blk_e, nused, w_gate, w_up, w_down, n_rows)
    return _combine_and_norm(y_rows, x1, route.reshape(B, S, LANES), gate2, ln2_g, ln2_b, tm=512)
```

```python
import functools

import jax
import jax.numpy as jnp
import numpy as np
from jax import lax
from jax.experimental import pallas as pl
from jax.experimental.pallas import tpu as pltpu

HEAD_DIM = 64
FOX_HEADS = 8
DIL_HEADS = 8
FOX_W = FOX_HEADS * HEAD_DIM
DIL_W = DIL_HEADS * HEAD_DIM
DILATED_PATTERNS = ((128, 1), (512, 4), (2048, 16))
ROPE_THETA = 500000.0
ROT_DIM = HEAD_DIM // 4
N_GROUPS = 4
EXPERTS_PER_GROUP = 8
N_EXPERTS = N_GROUPS * EXPERTS_PER_GROUP
TOP_K = 2
D_EXPERT = 512
MOE_BLOCK = 256
LN_EPS = 1e-5
NEG = -1e30
DEPTH = 1
DEEPNORM_ALPHA = (2 * DEPTH) ** 0.25
QK_SCALE = HEAD_DIM ** -0.5
LOG2E = 1.4426950408889634

LANES = 128
SUBLANES = 8
BF16_SUBLANES = 16
MXU_DIM = 256
HEADS_PER_VREG = LANES // HEAD_DIM
V_ROWS = HEAD_DIM + BF16_SUBLANES
FOX_TILE = 512
DIL_UNROLL = 4
VMEM_LIMIT = 56 * 1024 * 1024

F32 = jnp.float32
BF16 = jnp.bfloat16
HIGHEST = lax.Precision.HIGHEST
NT_DIMS = (((1,), (1,)), ((), ()))


def _params(*sem):
    return pltpu.CompilerParams(dimension_semantics=sem, vmem_limit_bytes=VMEM_LIMIT)


def _sigmoid(v):
    return 1.0 / (1.0 + jnp.exp(-v))


def _layer_norm(v, g, b):
    mu = jnp.mean(v, axis=-1, keepdims=True)
    d = v - mu
    var = jnp.mean(d * d, axis=-1, keepdims=True)
    return d * lax.rsqrt(var + LN_EPS) * g + b


def _mod_kernel(c_ref, w_ref, b_ref, o_ref):
    c = c_ref[...]
    o_ref[...] = jnp.dot(c * _sigmoid(c), w_ref[...], precision=HIGHEST,
                         preferred_element_type=F32) + b_ref[...]


def _modulation(c, w_ada, b_ada):
    B, D = c.shape
    cols = w_ada.shape[1]
    tn = 1024
    return pl.pallas_call(
        _mod_kernel,
        grid=(cols // tn,),
        in_specs=[pl.BlockSpec((B, D), lambda j: (0, 0)),
                  pl.BlockSpec((D, tn), lambda j: (0, j)),
                  pl.BlockSpec((1, tn), lambda j: (0, j))],
        out_specs=pl.BlockSpec((B, tn), lambda j: (0, j)),
        out_shape=jax.ShapeDtypeStruct((B, cols), F32),
        compiler_params=_params("arbitrary"),
        name="mod",
    )(c, w_ada, b_ada.reshape(1, cols))


def _proj_kernel(x_ref, pos_ref, sc_ref, sh_ref, wf_ref, wvt_ref, one_ref, wd_ref, wff_ref, bf_ref, invf_ref,
                 sgn_ref, pf_ref, vt_ref, pd_ref, lf_ref):
    tm = x_ref.shape[1]
    h = (x_ref[0] * (1.0 + sc_ref[0]) + sh_ref[0]).astype(BF16)

    for ci in range(3):
        acc = jnp.dot(h, wf_ref[:, ci * FOX_W:(ci + 1) * FOX_W], preferred_element_type=F32)
        if ci == 0:
            acc = acc * (QK_SCALE * LOG2E)
        pf_ref[0, :, ci * FOX_W:(ci + 1) * FOX_W] = acc.astype(BF16)
    vt = lax.dot_general(wvt_ref[...], h, NT_DIMS, preferred_element_type=F32) + one_ref[...]
    vt_ref[0] = vt.astype(BF16)

    z = jnp.dot(h, wff_ref[...], preferred_element_type=F32) + bf_ref[...]
    lf_ref[0] = (jnp.minimum(z, 0.0) - jnp.log1p(jnp.exp(-jnp.abs(z)))) * LOG2E

    ang = pos_ref[0].astype(F32) * invf_ref[...]
    cs = jnp.cos(ang)
    sn = jnp.sin(ang) * sgn_ref[...]
    lane = lax.broadcasted_iota(jnp.int32, (tm, LANES), 1)
    first_half = (lane % HEAD_DIM) < (ROT_DIM // 2)
    for ci in range(3):
        acc = jnp.dot(h, wd_ref[:, ci * DIL_W:(ci + 1) * DIL_W], preferred_element_type=F32)
        if ci == 2:
            pd_ref[0, :, ci * DIL_W:(ci + 1) * DIL_W] = acc
            continue
        for j in range(DIL_W // LANES):
            t = acc[:, j * LANES:(j + 1) * LANES]
            partner = jnp.where(first_half, pltpu.roll(t, LANES - ROT_DIM // 2, 1),
                                pltpu.roll(t, ROT_DIM // 2, 1))
            r = t * cs + partner * sn
            if ci == 0:
                r = r * (QK_SCALE * LOG2E)
            pd_ref[0, :, ci * DIL_W + j * LANES:ci * DIL_W + (j + 1) * LANES] = r


def _projection(x, positions, scale1, shift1, w_in, b_forget, tm):
    B, S, D = x.shape
    o = np.cumsum((0, FOX_W, FOX_W, FOX_W, FOX_W, FOX_HEADS, DIL_W, DIL_W, DIL_W))
    w_fox = jnp.concatenate([w_in[:, o[0]:o[2]], w_in[:, o[3]:o[4]]], axis=1).astype(BF16)
    w_vt = jnp.pad(w_in[:, o[2]:o[3]].T.reshape(FOX_HEADS, HEAD_DIM, D),
                   ((0, 0), (0, V_ROWS - HEAD_DIM), (0, 0))).reshape(FOX_HEADS * V_ROWS, D).astype(BF16)
    ones_row = jnp.asarray((np.arange(FOX_HEADS * V_ROWS) % V_ROWS == HEAD_DIM).astype(np.float32)
                           ).reshape(FOX_HEADS * V_ROWS, 1)
    w_ff = jnp.pad(w_in[:, o[4]:o[5]], ((0, 0), (0, LANES - FOX_HEADS))).astype(BF16)
    w_dil = w_in[:, o[5]:o[8]].astype(BF16)
    b_f = jnp.pad(b_forget, (0, LANES - FOX_HEADS)).reshape(1, LANES)
    e = np.arange(LANES) % HEAD_DIM
    inv_freq = ROPE_THETA ** (-jnp.arange(0, ROT_DIM, 2, dtype=F32) / ROT_DIM)
    invf = jnp.where(e < ROT_DIM, jnp.tile(inv_freq, LANES // (ROT_DIM // 2)), 0.0).reshape(1, LANES)
    sgn = jnp.asarray(np.where(e < ROT_DIM // 2, -1.0, np.where(e < ROT_DIM, 1.0, 0.0)), F32).reshape(1, LANES)
    const = lambda shape: pl.BlockSpec(shape, lambda b, i: (0,) * len(shape))
    return pl.pallas_call(
        _proj_kernel,
        grid=(B, S // tm),
        in_specs=[pl.BlockSpec((1, tm, D), lambda b, i: (b, i, 0)),
                  pl.BlockSpec((1, tm, 1), lambda b, i: (b, i, 0)),
                  pl.BlockSpec((1, 1, D), lambda b, i: (b, 0, 0)),
                  pl.BlockSpec((1, 1, D), lambda b, i: (b, 0, 0)),
                  const((D, 3 * FOX_W)), const((FOX_HEADS * V_ROWS, D)), const((FOX_HEADS * V_ROWS, 1)),
                  const((D, 3 * DIL_W)), const((D, LANES)),
                  const((1, LANES)), const((1, LANES)), const((1, LANES))],
        out_specs=[pl.BlockSpec((1, tm, 3 * FOX_W), lambda b, i: (b, i, 0)),
                   pl.BlockSpec((1, FOX_HEADS * V_ROWS, tm), lambda b, i: (b, 0, i)),
                   pl.BlockSpec((1, tm, 3 * DIL_W), lambda b, i: (b, i, 0)),
                   pl.BlockSpec((1, tm, LANES), lambda b, i: (b, i, 0))],
        out_shape=[jax.ShapeDtypeStruct((B, S, 3 * FOX_W), BF16),
                   jax.ShapeDtypeStruct((B, FOX_HEADS * V_ROWS, S), BF16),
                   jax.ShapeDtypeStruct((B, S, 3 * DIL_W), F32),
                   jax.ShapeDtypeStruct((B, S, LANES), F32)],
        compiler_params=_params("arbitrary", "arbitrary"),
        name="proj",
    )(x, positions.reshape(B, S, 1), scale1.reshape(B, 1, D), shift1.reshape(B, 1, D),
      w_fox, w_vt, ones_row, w_dil, w_ff, b_f, invf, sgn)


def _cum_kernel(lf_ref, c_ref, ct_ref):
    S = lf_ref.shape[1]
    r = lax.broadcasted_iota(jnp.int32, (LANES, LANES), 0)
    c = lax.broadcasted_iota(jnp.int32, (LANES, LANES), 1)
    tri = (c <= r).astype(F32)

    def body(j, carry):
        off = pl.multiple_of(j * LANES, LANES)
        cum = jnp.dot(tri, lf_ref[0, pl.ds(off, LANES), :], precision=HIGHEST,
                      preferred_element_type=F32) + carry
        c_ref[0, pl.ds(off, LANES), :] = cum
        ct_ref[0, :, pl.ds(off, LANES)] = cum.T[0:FOX_HEADS, :]
        return cum[LANES - 1:LANES, :]

    lax.fori_loop(0, S // LANES, body, jnp.zeros((1, LANES), F32))


def _cumulative_gate(log_f):
    B, S, _ = log_f.shape
    return pl.pallas_call(
        _cum_kernel,
        grid=(B,),
        in_specs=[pl.BlockSpec((1, S, LANES), lambda b: (b, 0, 0))],
        out_specs=[pl.BlockSpec((1, S, LANES), lambda b: (b, 0, 0)),
                   pl.BlockSpec((1, FOX_HEADS, S), lambda b: (b, 0, 0))],
        out_shape=[jax.ShapeDtypeStruct((B, S, LANES), F32),
                   jax.ShapeDtypeStruct((B, FOX_HEADS, S), F32)],
        compiler_params=_params("arbitrary"),
        name="cum",
    )(log_f)


def _fox_kernel(q_ref, k_ref, vt_ref, g_ref, ck_ref, cq_ref, o_ref, s_scr):
    S = q_ref.shape[1]
    T = FOX_TILE
    lane = lax.broadcasted_iota(jnp.int32, (T, LANES), 1)

    def q_block(i, _):
        qoff = pl.multiple_of(i * T, T)
        q = q_ref[0, pl.ds(qoff, T), :]
        qh = [jnp.where((lane // HEAD_DIM) == hh, q, jnp.zeros_like(q)) for hh in range(HEADS_PER_VREG)]
        cq = [cq_ref[0, 0, hh:hh + 1, pl.ds(qoff, T)] for hh in range(HEADS_PER_VREG)]

        def scores_into(buf, b):
            kb = k_ref[0, pl.ds(pl.multiple_of(b * T, T), T), :]
            for hh in range(HEADS_PER_VREG):
                s_scr[buf, hh] = lax.dot_general(kb, qh[hh], NT_DIMS, preferred_element_type=F32)

        def attend(buf, b, carry, masked):
            koff = pl.multiple_of(b * T, T)
            new = []
            for hh in range(HEADS_PER_VREG):
                m, acc = carry[hh]

                def scores(c0, rows):
                    s = s_scr[buf, hh, c0:c0 + rows, :] - ck_ref[0, 0, pl.ds(koff + c0, rows), hh:hh + 1]
                    if masked:
                        s = jnp.where(lax.broadcasted_iota(jnp.int32, (rows, T), 0) + c0
                                      <= lax.broadcasted_iota(jnp.int32, (rows, T), 1), s, NEG)
                    return s

                m_new = jnp.maximum(m, cq[hh] + jnp.max(scores(0, T), axis=0, keepdims=True))
                row = cq[hh] - m_new
                acc = jnp.exp2(m - m_new) * acc
                for c0 in range(0, T, MXU_DIM):
                    p = jnp.exp2((scores(c0, MXU_DIM) + row).astype(BF16))
                    vt = vt_ref[0, hh * V_ROWS:(hh + 1) * V_ROWS, pl.ds(koff + c0, MXU_DIM)]
                    acc = acc + jnp.dot(vt, p, preferred_element_type=F32)
                new.append((m_new, acc))
            return tuple(new)

        def pair(t, carry):
            scores_into(1, 2 * t + 1)
            carry = attend(0, 2 * t, carry, False)
            scores_into(0, 2 * t + 2)
            return attend(1, 2 * t + 1, carry, False)

        def odd_tail(carry):
            scores_into(1, i)
            return attend(1, i, attend(0, i - 1, carry, False), True)

        def even_tail(carry):
            return attend(0, i, carry, True)

        init = tuple((jnp.full((1, T), NEG, F32), jnp.zeros((V_ROWS, T), F32)) for _ in range(HEADS_PER_VREG))
        scores_into(0, 0)
        carry = lax.fori_loop(0, i // 2, pair, init)
        carry = lax.cond(i % 2 == 1, odd_tail, even_tail, carry)
        o_t = jnp.concatenate([acc[0:HEAD_DIM] * (1.0 / acc[HEAD_DIM:HEAD_DIM + 1]) for (_, acc) in carry], axis=0)
        gate = _sigmoid(g_ref[0, pl.ds(qoff, T), :].astype(F32))
        o_ref[0, pl.ds(qoff, T), :] = (o_t.T * gate).astype(BF16)
        return 0

    lax.fori_loop(0, S // T, q_block, 0)


def _fox_attention(pf, vt, ck, cq):
    B, S, _ = pf.shape
    nhp = FOX_W // LANES
    assert S % FOX_TILE == 0
    return pl.pallas_call(
        _fox_kernel,
        grid=(B, nhp),
        in_specs=[pl.BlockSpec((1, S, LANES), lambda b, h: (b, 0, h)),
                  pl.BlockSpec((1, S, LANES), lambda b, h: (b, 0, nhp + h)),
                  pl.BlockSpec((1, HEADS_PER_VREG * V_ROWS, S), lambda b, h: (b, h, 0)),
                  pl.BlockSpec((1, S, LANES), lambda b, h: (b, 0, 2 * nhp + h)),
                  pl.BlockSpec((1, 1, S, HEADS_PER_VREG), lambda b, h: (b, h, 0, 0)),
                  pl.BlockSpec((1, 1, HEADS_PER_VREG, S), lambda b, h: (b, h, 0, 0))],
        out_specs=pl.BlockSpec((1, S, LANES), lambda b, h: (b, 0, h)),
        out_shape=jax.ShapeDtypeStruct((B, S, FOX_W), BF16),
        scratch_shapes=[pltpu.VMEM((2, HEADS_PER_VREG, FOX_TILE, FOX_TILE), F32)],
        compiler_params=_params("arbitrary", "arbitrary"),
        name="fox",
    )(pf, pf, vt, pf, ck, cq)


def _dil_kernel(q_ref, k_ref, v_ref, o_ref, m_s, l_s, a_s, q4, k4, v4, m4, l4, a4):
    S = q_ref.shape[1]
    W = LANES
    lane = lax.broadcasted_iota(jnp.int32, (W, LANES), 1)
    head0 = lane < HEAD_DIM
    ri = lax.broadcasted_iota(jnp.int32, (2 * W, 2 * W), 0) % W
    ci = lax.broadcasted_iota(jnp.int32, (2 * W, 2 * W), 1)
    bias_rest = jnp.where((ci >= ri) & (ci <= ri + W), 0.0, NEG)
    bias_first = jnp.where((ci < W) & (ci <= ri), 0.0, NEG)
    ones = jnp.ones((2 * W, LANES), BF16)

    assert DILATED_PATTERNS == ((W, 1), (4 * W, 4), (16 * W, 16)) and S % (16 * W * 2) == 0
    assert (S // W) % DIL_UNROLL == 0
    L4 = S // 4

    def block_stats(q, kb, vb, first):
        qs = jnp.concatenate([jnp.where(head0, q, 0.0), jnp.where(head0, 0.0, q)], axis=0).astype(BF16)
        vb2 = jnp.concatenate([vb.astype(BF16), ones], axis=1)
        s = lax.dot_general(qs, kb.astype(BF16), NT_DIMS, preferred_element_type=F32)
        s = s + jnp.where(first, bias_first, bias_rest)
        m2 = jnp.max(s, axis=1, keepdims=True)
        out = jnp.dot(jnp.exp2((s - m2).astype(BF16)), vb2, preferred_element_type=F32)
        a_u = jnp.where(head0, out[0:W, 0:LANES], out[W:2 * W, 0:LANES])
        l_u = jnp.where(head0, out[0:W, LANES:2 * LANES], out[W:2 * W, LANES:2 * LANES])
        m_u = jnp.where(head0, m2[0:W], m2[W:2 * W])
        return m_u, l_u, a_u

    def merge(idx, stats):
        m_u, l_u, a_u = stats
        m_o = m4[idx, :]
        m_n = jnp.maximum(m_o, m_u)
        e_o = jnp.exp2(m_o - m_n)
        e_u = jnp.exp2(m_u - m_n)
        m4[idx, :] = m_n
        l4[idx, :] = l4[idx, :] * e_o + l_u * e_u
        a4[idx, :] = a4[idx, :] * e_o + a_u * e_u

    def for_blocks(unit):
        def group(t, _):
            for uu in range(DIL_UNROLL):
                unit(t * DIL_UNROLL + uu)
            return 0
        lax.fori_loop(0, S // W // DIL_UNROLL, group, 0)

    def unit1(u):
        qidx = pl.ds(pl.multiple_of(u * W, W), W)
        kidx = pl.ds(pl.multiple_of(jnp.maximum(u - 1, 0) * W, W), 2 * W)
        m_s[qidx, :], l_s[qidx, :], a_s[qidx, :] = block_stats(q_ref[0, qidx, :], k_ref[0, kidx, :],
                                                               v_ref[0, kidx, :], u == 0)
    for_blocks(unit1)

    def reorder(c, _):
        for r in range(4):
            src = pl.ds(c * (4 * W) + r, W, stride=4)
            dst = pl.ds(pl.multiple_of(r * L4 + c * W, W), W)
            q4[dst, :] = q_ref[0, src, :]
            k4[dst, :] = k_ref[0, src, :]
            v4[dst, :] = v_ref[0, src, :]
            m4[dst, :] = m_s[src, :]
            l4[dst, :] = l_s[src, :]
            a4[dst, :] = a_s[src, :]
        return 0
    lax.fori_loop(0, L4 // W, reorder, 0)

    nb4 = L4 // W

    def unit4(u):
        n = u % nb4
        base = (u // nb4) * L4
        qidx = pl.ds(pl.multiple_of(base + n * W, W), W)
        kidx = pl.ds(pl.multiple_of(base + jnp.maximum(n - 1, 0) * W, W), 2 * W)
        merge(qidx, block_stats(q4[qidx, :], k4[kidx, :], v4[kidx, :], n == 0))
    for_blocks(unit4)

    nb16 = S // (16 * W)

    def unit16(u):
        n = u % nb16
        r16 = u // nb16
        base = (r16 % 4) * L4 + r16 // 4
        qidx = pl.ds(base + n * (4 * W), W, stride=4)
        kidx = pl.ds(base + jnp.maximum(n - 1, 0) * (4 * W), 2 * W, stride=4)
        merge(qidx, block_stats(q4[qidx, :], k4[kidx, :], v4[kidx, :], n == 0))
    for_blocks(unit16)

    def finish(c, _):
        for r in range(4):
            src = pl.ds(pl.multiple_of(r * L4 + c * W, W), W)
            a_s[pl.ds(c * (4 * W) + r, W, stride=4), :] = a4[src, :] * (1.0 / l4[src, :])
        return 0
    lax.fori_loop(0, L4 // W, finish, 0)
    o_ref[0] = a_s[...].astype(BF16)


def _dilated_attention(pd):
    B, S, _ = pd.shape
    nhp = DIL_W // LANES
    return pl.pallas_call(
        _dil_kernel,
        grid=(B, nhp),
        in_specs=[pl.BlockSpec((1, S, LANES), lambda b, h: (b, 0, h)),
                  pl.BlockSpec((1, S, LANES), lambda b, h: (b, 0, nhp + h)),
                  pl.BlockSpec((1, S, LANES), lambda b, h: (b, 0, 2 * nhp + h))],
        out_specs=pl.BlockSpec((1, S, LANES), lambda b, h: (b, 0, h)),
        out_shape=jax.ShapeDtypeStruct((B, S, DIL_W), BF16),
        scratch_shapes=[pltpu.VMEM((S, LANES), F32)] * 9,
        compiler_params=_params("arbitrary", "arbitrary"),
        name="dil",
    )(pd, pd, pd)


def _out_kernel(of_ref, od_ref, x_ref, g1_ref, sc2_ref, sh2_ref, wo_ref, lng_ref, lnb_ref, wrh_ref, wrl_ref, br_ref,
                x1_ref, h2_ref, rt_ref, cnt_ref):
    tm = x_ref.shape[1]
    first_step = (pl.program_id(0) == 0) & (pl.program_id(1) == 0)

    @pl.when(first_step)
    def _():
        cnt_ref[...] = jnp.zeros_like(cnt_ref)

    y = (jnp.dot(of_ref[0], wo_ref[0:FOX_W, :], preferred_element_type=F32)
         + jnp.dot(od_ref[0], wo_ref[FOX_W:FOX_W + DIL_W, :], preferred_element_type=F32))
    x1 = _layer_norm(DEEPNORM_ALPHA * x_ref[0] + g1_ref[0] * y, lng_ref[...], lnb_ref[...])
    x1_ref[0] = x1
    h2 = x1 * (1.0 + sc2_ref[0]) + sh2_ref[0]
    for s in range(SUBLANES):
        h2_ref[pl.ds(s, tm, stride=SUBLANES), :] = h2[:, s * LANES:(s + 1) * LANES]

    h_hi = h2.astype(BF16)
    h_lo = (h2 - h_hi.astype(F32)).astype(BF16)
    lg = (jnp.dot(h_hi, wrh_ref[...], preferred_element_type=F32)
          + jnp.dot(h_lo, wrh_ref[...], preferred_element_type=F32)
          + jnp.dot(h_hi, wrl_ref[...], preferred_element_type=F32)) + br_ref[...]
    lane = lax.broadcasted_iota(jnp.int32, (tm, LANES), 1)
    lanef = lane.astype(F32)
    far = float(LANES)
    is_g = lane < N_GROUPS
    gl = jnp.where(is_g, lg, NEG)
    gmax = jnp.max(gl, axis=1, keepdims=True)
    gidx = jnp.min(jnp.where(gl == gmax, lanef, far), axis=1, keepdims=True)
    p_g = 1.0 / jnp.sum(jnp.where(is_g, jnp.exp(gl - gmax), 0.0), axis=1, keepdims=True)
    e_lo = N_GROUPS + gidx * EXPERTS_PER_GROUP
    in_group = (lanef >= e_lo) & (lanef < e_lo + EXPERTS_PER_GROUP)
    el = jnp.where(in_group, lg, NEG)
    v1 = jnp.max(el, axis=1, keepdims=True)
    i1 = jnp.min(jnp.where(in_group & (el == v1), lanef, far), axis=1, keepdims=True)
    rest = in_group & (lanef != i1)
    el2 = jnp.where(rest, lg, NEG)
    v2 = jnp.max(el2, axis=1, keepdims=True)
    i2 = jnp.min(jnp.where(rest & (el2 == v2), lanef, far), axis=1, keepdims=True)
    e21 = jnp.exp(v2 - v1)
    w1 = p_g / (1.0 + e21)
    w2 = p_g * e21 / (1.0 + e21)

    pick1 = lanef == i1
    pick2 = lanef == i2
    onehot = jnp.where(pick1 | pick2, 1.0, 0.0)
    rr = lax.broadcasted_iota(jnp.int32, (tm, tm), 0)
    cc = lax.broadcasted_iota(jnp.int32, (tm, tm), 1)
    strict_lower = jnp.where(cc < rr, 1.0, 0.0).astype(BF16)
    before = jnp.dot(strict_lower, onehot.astype(BF16), preferred_element_type=F32) + cnt_ref[...]
    r1 = jnp.sum(jnp.where(pick1, before, 0.0), axis=1, keepdims=True)
    r2 = jnp.sum(jnp.where(pick2, before, 0.0), axis=1, keepdims=True)
    cnt_ref[...] += jnp.sum(onehot, axis=0, keepdims=True)

    packed = jnp.zeros((tm, LANES), F32)
    for k, val in enumerate((i1 - N_GROUPS, i2 - N_GROUPS, w1, w2, r1, r2)):
        packed = jnp.where(lane == k, val, packed)
    rt_ref[0] = packed


def _out_and_route(of, od, x, gate1, scale2, shift2, w_out, ln_g, ln_b, w_rg, b_rg, w_re, b_re, tm):
    B, S, D = x.shape
    n_r = N_GROUPS + N_EXPERTS
    w_r = jnp.pad(jnp.concatenate([w_rg, w_re], axis=1), ((0, 0), (0, LANES - n_r)))
    b_r = jnp.pad(jnp.concatenate([b_rg, b_re]), (0, LANES - n_r)).reshape(1, LANES)
    w_r_hi = w_r.astype(BF16)
    w_r_lo = (w_r - w_r_hi.astype(F32)).astype(BF16)
    tile = lambda w: pl.BlockSpec((1, tm, w), lambda b, i: (b, i, 0))
    per_batch = pl.BlockSpec((1, 1, D), lambda b, i: (b, 0, 0))
    const = lambda shape: pl.BlockSpec(shape, lambda b, i: (0,) * len(shape))
    return pl.pallas_call(
        _out_kernel,
        grid=(B, S // tm),
        in_specs=[tile(FOX_W), tile(DIL_W), tile(D), per_batch, per_batch, per_batch,
                  const((D, D)), const((1, D)), const((1, D)), const((D, LANES)), const((D, LANES)),
                  const((1, LANES))],
        out_specs=[tile(D), pl.BlockSpec((tm * SUBLANES, LANES), lambda b, i: (b * (S // tm) + i, 0)),
                   tile(LANES), const((1, LANES))],
        out_shape=[jax.ShapeDtypeStruct((B, S, D), F32), jax.ShapeDtypeStruct((B * S * SUBLANES, LANES), F32),
                   jax.ShapeDtypeStruct((B, S, LANES), F32), jax.ShapeDtypeStruct((1, LANES), F32)],
        compiler_params=_params("arbitrary", "arbitrary"),
        name="out",
    )(of, od, x, gate1.reshape(B, 1, D), scale2.reshape(B, 1, D), shift2.reshape(B, 1, D),
      w_out.astype(BF16), ln_g.reshape(1, D), ln_b.reshape(1, D), w_r_hi, w_r_lo, b_r)


def _row_copy(src_ref, dst_ref, sem, src_row, dst_row):
    return pltpu.make_async_copy(src_ref.at[pl.ds(pl.multiple_of(src_row * SUBLANES, SUBLANES), SUBLANES), :],
                                 dst_ref.at[pl.ds(pl.multiple_of(dst_row * SUBLANES, SUBLANES), SUBLANES), :], sem)


def _moe_kernel(blk_e_ref, nused_ref, row_tok_ref, row_slot_ref, h_ref, wg_ref, wu_ref, wd_ref, yt_ref,
                xbuf0, xbuf1, ybuf0, ybuf1, mid_s, gsem, ssem):
    del blk_e_ref
    i = pl.program_id(0)
    nused = nused_ref[0]
    D = xbuf0.shape[0] // MOE_BLOCK * LANES
    nk = D // MXU_DIM

    def gather_rows(block, xdst, s):
        base = block * MOE_BLOCK
        for r in range(MOE_BLOCK):
            _row_copy(h_ref, xdst, gsem.at[s], row_tok_ref[base + r], r).start(priority=r % 2)

    def scatter_rows(block, ysrc, s):
        base = block * MOE_BLOCK
        for r in range(MOE_BLOCK):
            _row_copy(ysrc, yt_ref, ssem.at[s], r, row_slot_ref[base + r]).start(priority=r % 2)

    def wait_gather(xdst, s):
        pltpu.make_async_copy(h_ref.at[pl.ds(0, MOE_BLOCK * SUBLANES), :], xdst, gsem.at[s]).wait()

    def wait_scatter(ysrc, s):
        pltpu.make_async_copy(ysrc, yt_ref.at[pl.ds(0, MOE_BLOCK * SUBLANES), :], ssem.at[s]).wait()

    def compute(xcur, ycur):
        xk = [jnp.concatenate([xcur[pl.ds(2 * j, MOE_BLOCK, stride=SUBLANES), :],
                               xcur[pl.ds(2 * j + 1, MOE_BLOCK, stride=SUBLANES), :]], axis=1).astype(BF16)
              for j in range(nk)]
        for c in range(D_EXPERT // MXU_DIM):
            cols = slice(c * MXU_DIM, (c + 1) * MXU_DIM)
            g = sum(jnp.dot(xk[j], wg_ref[0, MXU_DIM * j:MXU_DIM * (j + 1), cols].astype(BF16),
                            preferred_element_type=F32) for j in range(nk))
            u = sum(jnp.dot(xk[j], wu_ref[0, MXU_DIM * j:MXU_DIM * (j + 1), cols].astype(BF16),
                            preferred_element_type=F32) for j in range(nk))
            mid_s[:, cols] = (g * _sigmoid(g) * u).astype(BF16)
        mid = mid_s[...]
        for c in range(D // MXU_DIM):
            y = jnp.dot(mid, wd_ref[0, :, c * MXU_DIM:(c + 1) * MXU_DIM].astype(BF16), preferred_element_type=F32)
            ycur[pl.ds(2 * c, MOE_BLOCK, stride=SUBLANES), :] = y[:, 0:LANES]
            ycur[pl.ds(2 * c + 1, MOE_BLOCK, stride=SUBLANES), :] = y[:, LANES:2 * LANES]

    def step(s, xcur, xnxt, ycur, yprv):
        o = 1 - s
        wait_gather(xcur, s)

        @pl.when(i >= 2)
        def _():
            wait_scatter(ycur, s)

        @pl.when(i + 1 < nused)
        def _():
            gather_rows(i + 1, xnxt, o)

        @pl.when(i >= 1)
        def _():
            scatter_rows(i - 1, yprv, o)

        compute(xcur, ycur)

        @pl.when(i + 1 == nused)
        def _():
            scatter_rows(i, ycur, s)

            @pl.when(i >= 1)
            def _():
                wait_scatter(yprv, o)

            wait_scatter(ycur, s)

    @pl.when(i == 0)
    def _():
        gather_rows(0, xbuf0, 0)

    @pl.when((i < nused) & (i % 2 == 0))
    def _():
        step(0, xbuf0, xbuf1, ybuf0, ybuf1)

    @pl.when((i < nused) & (i % 2 == 1))
    def _():
        step(1, xbuf1, xbuf0, ybuf1, ybuf0)

    @pl.when(i >= nused)
    def _():
        ybuf0[...] = jnp.zeros_like(ybuf0)
        first = pl.multiple_of(row_slot_ref[i * MOE_BLOCK] * SUBLANES, SUBLANES)
        fill = pltpu.make_async_copy(ybuf0, yt_ref.at[pl.ds(first, MOE_BLOCK * SUBLANES), :], ssem.at[0])
        fill.start()
        fill.wait()


def _expert_mlp(h2_rows, row_tok, row_slot, blk_e, nused, w_gate, w_up, w_down, n_out_rows):
    D = w_gate.shape[1]
    nblk = blk_e.shape[0]
    wmap = lambda i, be, nu, rt, rs: (be[i], 0, 0)
    return pl.pallas_call(
        _moe_kernel,
        grid_spec=pltpu.PrefetchScalarGridSpec(
            num_scalar_prefetch=4,
            grid=(nblk,),
            in_specs=[pl.BlockSpec(memory_space=pl.ANY),
                      pl.BlockSpec((1, D, D_EXPERT), wmap),
                      pl.BlockSpec((1, D, D_EXPERT), wmap),
                      pl.BlockSpec((1, D_EXPERT, D), wmap)],
            out_specs=pl.BlockSpec(memory_space=pl.ANY),
            scratch_shapes=[pltpu.VMEM((MOE_BLOCK * SUBLANES, LANES), F32)] * 4
            + [pltpu.VMEM((MOE_BLOCK, D_EXPERT), BF16),
               pltpu.SemaphoreType.DMA((2,)), pltpu.SemaphoreType.DMA((2,))]),
        out_shape=jax.ShapeDtypeStruct((n_out_rows * SUBLANES, LANES), F32),
        compiler_params=_params("arbitrary"),
        name="moe",
    )(blk_e, nused, row_tok, row_slot, h2_rows, w_gate, w_up, w_down)


PLAN_CHUNK = 8192
PLAN_UNROLL = 8


def _plan_kernel(cnt_ref, poff_ref, dest_ref, slot_ref):
    g = pl.program_id(0)
    n_assign = pl.num_programs(0) * PLAN_CHUNK

    @pl.when(g == 0)
    def _():
        def expert(e, nxt):
            lo = poff_ref[e] + cnt_ref[e]

            def pad_row(p, _):
                slot_ref[p] = nxt + (p - lo)
                return 0

            lax.fori_loop(lo, poff_ref[e + 1], pad_row, 0)
            return nxt + (poff_ref[e + 1] - lo)

        lax.fori_loop(0, N_EXPERTS, expert, n_assign)

    def body(j, _):
        for u in range(PLAN_UNROLL):
            a = j * PLAN_UNROLL + u
            slot_ref[dest_ref[a]] = g * PLAN_CHUNK + a
        return 0

    lax.fori_loop(0, PLAN_CHUNK // PLAN_UNROLL, body, 0)


def _dispatch_plan(dest, cnt, seg_start, n_rows):
    n_assign = dest.shape[0]
    assert n_assign % PLAN_CHUNK == 0
    return pl.pallas_call(
        _plan_kernel,
        grid_spec=pltpu.PrefetchScalarGridSpec(
            num_scalar_prefetch=2,
            grid=(n_assign // PLAN_CHUNK,),
            in_specs=[pl.BlockSpec((PLAN_CHUNK,), lambda g, c, p: (g,), memory_space=pltpu.SMEM)],
            out_specs=pl.BlockSpec((n_rows,), lambda g, c, p: (0,), memory_space=pltpu.SMEM)),
        out_shape=jax.ShapeDtypeStruct((n_rows,), jnp.int32),
        compiler_params=_params("arbitrary"),
        name="plan",
    )(cnt, seg_start, dest)


def _final_kernel(y0_ref, y1_ref, x1_ref, rt_ref, g2_ref, lng_ref, lnb_ref, o_ref):
    tm = x1_ref.shape[1]
    rt = rt_ref[0]
    w0 = rt[:, 2:3]
    w1 = rt[:, 3:4]
    y = jnp.concatenate([w0 * y0_ref[pl.ds(s, tm, stride=SUBLANES), :] + w1 * y1_ref[pl.ds(s, tm, stride=SUBLANES), :]
                         for s in range(SUBLANES)], axis=1)
    o_ref[0] = _layer_norm(DEEPNORM_ALPHA * x1_ref[0] + g2_ref[0] * y, lng_ref[...], lnb_ref[...])


def _combine_and_norm(y_rows, x1, route, gate2, ln_g, ln_b, tm):
    B, S, D = x1.shape
    nt = S // tm
    return pl.pallas_call(
        _final_kernel,
        grid=(B, nt),
        in_specs=[pl.BlockSpec((tm * SUBLANES, LANES), lambda b, i: (b * nt + i, 0)),
                  pl.BlockSpec((tm * SUBLANES, LANES), lambda b, i: (B * nt + b * nt + i, 0)),
                  pl.BlockSpec((1, tm, D), lambda b, i: (b, i, 0)),
                  pl.BlockSpec((1, tm, LANES), lambda b, i: (b, i, 0)),
                  pl.BlockSpec((1, 1, D), lambda b, i: (b, 0, 0)),
                  pl.BlockSpec((1, D), lambda b, i: (0, 0)),
                  pl.BlockSpec((1, D), lambda b, i: (0, 0))],
        out_specs=pl.BlockSpec((1, tm, D), lambda b, i: (b, i, 0)),
        out_shape=jax.ShapeDtypeStruct((B, S, D), F32),
        compiler_params=_params("arbitrary", "arbitrary"),
        name="final",
    )(y_rows, y_rows, x1, route, gate2.reshape(B, 1, D), ln_g.reshape(1, D), ln_b.reshape(1, D))


def kernel(x, c, positions, w_ada, b_ada, w_in, b_forget, w_out, ln1_g, ln1_b, w_router_group, b_router_group,
           w_router_expert, b_router_expert, w_up, w_gate, w_down, ln2_g, ln2_b):
    B, S, D = x.shape
    N = B * S
    assert D == FOX_W + DIL_W and S % 2048 == 0
    mod = _modulation(c, w_ada, b_ada)
    shift1, scale1, gate1, shift2, scale2, gate2 = jnp.split(mod, 6, axis=-1)

    pf, vt, pd, log_f = _projection(x, positions, scale1, shift1, w_in, b_forget, tm=512)
    cum, cum_t = _cumulative_gate(log_f)
    nhp = FOX_W // LANES
    ck = cum[..., :FOX_HEADS].reshape(B, S, nhp, HEADS_PER_VREG).transpose(0, 2, 1, 3)
    cq = cum_t.reshape(B, nhp, HEADS_PER_VREG, S)
    of = _fox_attention(pf, vt, ck, cq)
    od = _dilated_attention(pd)

    x1, h2, route, counts = _out_and_route(of, od, x, gate1, scale2, shift2, w_out, ln1_g, ln1_b,
                                           w_router_group, b_router_group, w_router_expert, b_router_expert,
                                           tm=512)

    route = route.reshape(N, LANES)
    eid = route[:, 0:TOP_K].astype(jnp.int32)
    rank = route[:, 4:4 + TOP_K].astype(jnp.int32)
    cnt = counts[0, N_GROUPS:N_GROUPS + N_EXPERTS].astype(jnp.int32)
    pcnt = (cnt + MOE_BLOCK - 1) // MOE_BLOCK * MOE_BLOCK
    pend = jnp.cumsum(pcnt)
    experts = jnp.arange(N_EXPERTS, dtype=jnp.int32)
    poff = jnp.sum(jnp.where(eid[..., None] == experts, pend - pcnt, 0), axis=-1)
    dest = (poff + rank).T.reshape(N * TOP_K)
    nblk = (N * TOP_K) // MOE_BLOCK + N_EXPERTS
    n_rows = nblk * MOE_BLOCK
    seg_start = jnp.concatenate([pend - pcnt, jnp.full((1,), n_rows, jnp.int32)])
    row_slot = _dispatch_plan(dest, cnt, seg_start, n_rows)
    row_tok = jnp.where(row_slot >= N * TOP_K, N - 1, row_slot % N)
    nused = pend[-1:] // MOE_BLOCK
    blk_start = jnp.arange(nblk, dtype=jnp.int32) * MOE_BLOCK
    blk_e = jnp.sum((pend[None, :] <= jnp.minimum(blk_start, pend[-1] - 1)[:, None]).astype(jnp.int32), axis=1)

    y_rows = _expert_mlp(h2, row_tok, row_slot, blk_e, nused, w_gate, w_up, w_down, n_rows)
    return _combine_and_norm(y_rows, x1, route.reshape(B, S, LANES), gate2, ln2_g, ln2_b, tm=512)
```

```python
import functools

import jax
import jax.numpy as jnp
import numpy as np
from jax import lax
from jax.experimental import pallas as pl
from jax.experimental.pallas import tpu as pltpu

HEAD_DIM = 64
FOX_HEADS = 8
DIL_HEADS = 8
FOX_W = FOX_HEADS * HEAD_DIM
DIL_W = DIL_HEADS * HEAD_DIM
DILATED_PATTERNS = ((128, 1), (512, 4), (2048, 16))
ROPE_THETA = 500000.0
ROT_DIM = HEAD_DIM // 4
N_GROUPS = 4
EXPERTS_PER_GROUP = 8
N_EXPERTS = N_GROUPS * EXPERTS_PER_GROUP
TOP_K = 2
D_EXPERT = 512
MOE_BLOCK = 256
LN_EPS = 1e-5
NEG = -1e30
DEPTH = 1
DEEPNORM_ALPHA = (2 * DEPTH) ** 0.25
QK_SCALE = HEAD_DIM ** -0.5
LOG2E = 1.4426950408889634

LANES = 128
SUBLANES = 8
BF16_SUBLANES = 16
MXU_DIM = 256
HEADS_PER_VREG = LANES // HEAD_DIM
V_ROWS = HEAD_DIM + BF16_SUBLANES
FOX_TILE = 512
DIL_UNROLL = 4
VMEM_LIMIT = 56 * 1024 * 1024

F32 = jnp.float32
BF16 = jnp.bfloat16
HIGHEST = lax.Precision.HIGHEST
NT_DIMS = (((1,), (1,)), ((), ()))


def _params(*sem):
    return pltpu.CompilerParams(dimension_semantics=sem, vmem_limit_bytes=VMEM_LIMIT)


def _sigmoid(v):
    return 1.0 / (1.0 + jnp.exp(-v))


def _layer_norm(v, g, b):
    mu = jnp.mean(v, axis=-1, keepdims=True)
    d = v - mu
    var = jnp.mean(d * d, axis=-1, keepdims=True)
    return d * lax.rsqrt(var + LN_EPS) * g + b


def _mod_kernel(c_ref, w_ref, b_ref, o_ref):
    c = c_ref[...]
    o_ref[...] = jnp.dot(c * _sigmoid(c), w_ref[...], precision=HIGHEST,
                         preferred_element_type=F32) + b_ref[...]


def _modulation(c, w_ada, b_ada):
    B, D = c.shape
    cols = w_ada.shape[1]
    tn = 1024
    return pl.pallas_call(
        _mod_kernel,
        grid=(cols // tn,),
        in_specs=[pl.BlockSpec((B, D), lambda j: (0, 0)),
                  pl.BlockSpec((D, tn), lambda j: (0, j)),
                  pl.BlockSpec((1, tn), lambda j: (0, j))],
        out_specs=pl.BlockSpec((B, tn), lambda j: (0, j)),
        out_shape=jax.ShapeDtypeStruct((B, cols), F32),
        compiler_params=_params("arbitrary"),
        name="mod",
    )(c, w_ada, b_ada.reshape(1, cols))


def _proj_kernel(x_ref, pos_ref, sc_ref, sh_ref, wf_ref, wvt_ref, one_ref, wd_ref, wff_ref, bf_ref, invf_ref,
                 sgn_ref, pf_ref, vt_ref, pd_ref, lf_ref):
    tm = x_ref.shape[1]
    h = (x_ref[0] * (1.0 + sc_ref[0]) + sh_ref[0]).astype(BF16)

    for ci in range(3):
        acc = jnp.dot(h, wf_ref[:, ci * FOX_W:(ci + 1) * FOX_W], preferred_element_type=F32)
        if ci == 0:
            acc = acc * (QK_SCALE * LOG2E)
        pf_ref[0, :, ci * FOX_W:(ci + 1) * FOX_W] = acc.astype(BF16)
    vt = lax.dot_general(wvt_ref[...], h, NT_DIMS, preferred_element_type=F32) + one_ref[...]
    vt_ref[0] = vt.astype(BF16)

    z = jnp.dot(h, wff_ref[...], preferred_element_type=F32) + bf_ref[...]
    lf_ref[0] = (jnp.minimum(z, 0.0) - jnp.log1p(jnp.exp(-jnp.abs(z)))) * LOG2E

    ang = pos_ref[0].astype(F32) * invf_ref[...]
    cs = jnp.cos(ang)
    sn = jnp.sin(ang) * sgn_ref[...]
    lane = lax.broadcasted_iota(jnp.int32, (tm, LANES), 1)
    first_half = (lane % HEAD_DIM) < (ROT_DIM // 2)
    for ci in range(3):
        acc = jnp.dot(h, wd_ref[:, ci * DIL_W:(ci + 1) * DIL_W], preferred_element_type=F32)
        if ci == 2:
            pd_ref[0, :, ci * DIL_W:(ci + 1) * DIL_W] = acc
            continue
        for j in range(DIL_W // LANES):
            t = acc[:, j * LANES:(j + 1) * LANES]
            partner = jnp.where(first_half, pltpu.roll(t, LANES - ROT_DIM // 2, 1),
                                pltpu.roll(t, ROT_DIM // 2, 1))
            r = t * cs + partner * sn
            if ci == 0:
                r = r * (QK_SCALE * LOG2E)
            pd_ref[0, :, ci * DIL_W + j * LANES:ci * DIL_W + (j + 1) * LANES] = r


def _projection(x, positions, scale1, shift1, w_in, b_forget, tm):
    B, S, D = x.shape
    o = np.cumsum((0, FOX_W, FOX_W, FOX_W, FOX_W, FOX_HEADS, DIL_W, DIL_W, DIL_W))
    w_fox = jnp.concatenate([w_in[:, o[0]:o[2]], w_in[:, o[3]:o[4]]], axis=1).astype(BF16)
    w_vt = jnp.pad(w_in[:, o[2]:o[3]].T.reshape(FOX_HEADS, HEAD_DIM, D),
                   ((0, 0), (0, V_ROWS - HEAD_DIM), (0, 0))).reshape(FOX_HEADS * V_ROWS, D).astype(BF16)
    ones_row = jnp.asarray((np.arange(FOX_HEADS * V_ROWS) % V_ROWS == HEAD_DIM).astype(np.float32)
                           ).reshape(FOX_HEADS * V_ROWS, 1)
    w_ff = jnp.pad(w_in[:, o[4]:o[5]], ((0, 0), (0, LANES - FOX_HEADS))).astype(BF16)
    w_dil = w_in[:, o[5]:o[8]].astype(BF16)
    b_f = jnp.pad(b_forget, (0, LANES - FOX_HEADS)).reshape(1, LANES)
    e = np.arange(LANES) % HEAD_DIM
    inv_freq = ROPE_THETA ** (-jnp.arange(0, ROT_DIM, 2, dtype=F32) / ROT_DIM)
    invf = jnp.where(e < ROT_DIM, jnp.tile(inv_freq, LANES // (ROT_DIM // 2)), 0.0).reshape(1, LANES)
    sgn = jnp.asarray(np.where(e < ROT_DIM // 2, -1.0, np.where(e < ROT_DIM, 1.0, 0.0)), F32).reshape(1, LANES)
    const = lambda shape: pl.BlockSpec(shape, lambda b, i: (0,) * len(shape))
    return pl.pallas_call(
        _proj_kernel,
        grid=(B, S // tm),
        in_specs=[pl.BlockSpec((1, tm, D), lambda b, i: (b, i, 0)),
                  pl.BlockSpec((1, tm, 1), lambda b, i: (b, i, 0)),
                  pl.BlockSpec((1, 1, D), lambda b, i: (b, 0, 0)),
                  pl.BlockSpec((1, 1, D), lambda b, i: (b, 0, 0)),
                  const((D, 3 * FOX_W)), const((FOX_HEADS * V_ROWS, D)), const((FOX_HEADS * V_ROWS, 1)),
                  const((D, 3 * DIL_W)), const((D, LANES)),
                  const((1, LANES)), const((1, LANES)), const((1, LANES))],
        out_specs=[pl.BlockSpec((1, tm, 3 * FOX_W), lambda b, i: (b, i, 0)),
                   pl.BlockSpec((1, FOX_HEADS * V_ROWS, tm), lambda b, i: (b, 0, i)),
                   pl.BlockSpec((1, tm, 3 * DIL_W), lambda b, i: (b, i, 0)),
                   pl.BlockSpec((1, tm, LANES), lambda b, i: (b, i, 0))],
        out_shape=[jax.ShapeDtypeStruct((B, S, 3 * FOX_W), BF16),
                   jax.ShapeDtypeStruct((B, FOX_HEADS * V_ROWS, S), BF16),
                   jax.ShapeDtypeStruct((B, S, 3 * DIL_W), F32),
                   jax.ShapeDtypeStruct((B, S, LANES), F32)],
        compiler_params=_params("arbitrary", "arbitrary"),
        name="proj",
    )(x, positions.reshape(B, S, 1), scale1.reshape(B, 1, D), shift1.reshape(B, 1, D),
      w_fox, w_vt, ones_row, w_dil, w_ff, b_f, invf, sgn)


def _cum_kernel(lf_ref, c_ref, ct_ref):
    S = lf_ref.shape[1]
    r = lax.broadcasted_iota(jnp.int32, (LANES, LANES), 0)
    c = lax.broadcasted_iota(jnp.int32, (LANES, LANES), 1)
    tri = (c <= r).astype(F32)

    def body(j, carry):
        off = pl.multiple_of(j * LANES, LANES)
        cum = jnp.dot(tri, lf_ref[0, pl.ds(off, LANES), :], precision=HIGHEST,
                      preferred_element_type=F32) + carry
        c_ref[0, pl.ds(off, LANES), :] = cum
        ct_ref[0, :, pl.ds(off, LANES)] = cum.T[0:FOX_HEADS, :]
        return cum[LANES - 1:LANES, :]

    lax.fori_loop(0, S // LANES, body, jnp.zeros((1, LANES), F32))


def _cumulative_gate(log_f):
    B, S, _ = log_f.shape
    return pl.pallas_call(
        _cum_kernel,
        grid=(B,),
        in_specs=[pl.BlockSpec((1, S, LANES), lambda b: (b, 0, 0))],
        out_specs=[pl.BlockSpec((1, S, LANES), lambda b: (b, 0, 0)),
                   pl.BlockSpec((1, FOX_HEADS, S), lambda b: (b, 0, 0))],
        out_shape=[jax.ShapeDtypeStruct((B, S, LANES), F32),
                   jax.ShapeDtypeStruct((B, FOX_HEADS, S), F32)],
        compiler_params=_params("arbitrary"),
        name="cum",
    )(log_f)


def _fox_kernel(q_ref, k_ref, vt_ref, g_ref, ck_ref, cq_ref, o_ref, s_scr):
    S = q_ref.shape[1]
    T = FOX_TILE
    lane = lax.broadcasted_iota(jnp.int32, (T, LANES), 1)

    def q_block(i, _):
        qoff = pl.multiple_of(i * T, T)
        q = q_ref[0, pl.ds(qoff, T), :]
        qh = [jnp.where((lane // HEAD_DIM) == hh, q, jnp.zeros_like(q)) for hh in range(HEADS_PER_VREG)]
        cq = [cq_ref[0, 0, hh:hh + 1, pl.ds(qoff, T)] for hh in range(HEADS_PER_VREG)]

        def scores_into(buf, b):
            kb = k_ref[0, pl.ds(pl.multiple_of(b * T, T), T), :]
            for hh in range(HEADS_PER_VREG):
                s_scr[buf, hh] = lax.dot_general(kb, qh[hh], NT_DIMS, preferred_element_type=F32)

        def attend(buf, b, carry, masked):
            koff = pl.multiple_of(b * T, T)
            new = []
            for hh in range(HEADS_PER_VREG):
                m, acc = carry[hh]

                def scores(c0, rows):
                    s = s_scr[buf, hh, c0:c0 + rows, :] - ck_ref[0, 0, pl.ds(koff + c0, rows), hh:hh + 1]
                    if masked:
                        s = jnp.where(lax.broadcasted_iota(jnp.int32, (rows, T), 0) + c0
                                      <= lax.broadcasted_iota(jnp.int32, (rows, T), 1), s, NEG)
                    return s

                m_new = jnp.maximum(m, cq[hh] + jnp.max(scores(0, T), axis=0, keepdims=True))
                row = cq[hh] - m_new
                acc = jnp.exp2(m - m_new) * acc
                for c0 in range(0, T, MXU_DIM):
                    p = jnp.exp2((scores(c0, MXU_DIM) + row).astype(BF16))
                    vt = vt_ref[0, hh * V_ROWS:(hh + 1) * V_ROWS, pl.ds(koff + c0, MXU_DIM)]
                    acc = acc + jnp.dot(vt, p, preferred_element_type=F32)
                new.append((m_new, acc))
            return tuple(new)

        def pair(t, carry):
            scores_into(1, 2 * t + 1)
            carry = attend(0, 2 * t, carry, False)
            scores_into(0, 2 * t + 2)
            return attend(1, 2 * t + 1, carry, False)

        def odd_tail(carry):
            scores_into(1, i)
            return attend(1, i, attend(0, i - 1, carry, False), True)

        def even_tail(carry):
            return attend(0, i, carry, True)

        init = tuple((jnp.full((1, T), NEG, F32), jnp.zeros((V_ROWS, T), F32)) for _ in range(HEADS_PER_VREG))
        scores_into(0, 0)
        carry = lax.fori_loop(0, i // 2, pair, init)
        carry = lax.cond(i % 2 == 1, odd_tail, even_tail, carry)
        o_t = jnp.concatenate([acc[0:HEAD_DIM] * (1.0 / acc[HEAD_DIM:HEAD_DIM + 1]) for (_, acc) in carry], axis=0)
        gate = _sigmoid(g_ref[0, pl.ds(qoff, T), :].astype(F32))
        o_ref[0, pl.ds(qoff, T), :] = (o_t.T * gate).astype(BF16)
        return 0

    lax.fori_loop(0, S // T, q_block, 0)


def _fox_attention(pf, vt, ck, cq):
    B, S, _ = pf.shape
    nhp = FOX_W // LANES
    assert S % FOX_TILE == 0
    return pl.pallas_call(
        _fox_kernel,
        grid=(B, nhp),
        in_specs=[pl.BlockSpec((1, S, LANES), lambda b, h: (b, 0, h)),
                  pl.BlockSpec((1, S, LANES), lambda b, h: (b, 0, nhp + h)),
                  pl.BlockSpec((1, HEADS_PER_VREG * V_ROWS, S), lambda b, h: (b, h, 0)),
                  pl.BlockSpec((1, S, LANES), lambda b, h: (b, 0, 2 * nhp + h)),
                  pl.BlockSpec((1, 1, S, HEADS_PER_VREG), lambda b, h: (b, h, 0, 0)),
                  pl.BlockSpec((1, 1, HEADS_PER_VREG, S), lambda b, h: (b, h, 0, 0))],
        out_specs=pl.BlockSpec((1, S, LANES), lambda b, h: (b, 0, h)),
        out_shape=jax.ShapeDtypeStruct((B, S, FOX_W), BF16),
        scratch_shapes=[pltpu.VMEM((2, HEADS_PER_VREG, FOX_TILE, FOX_TILE), F32)],
        compiler_params=_params("arbitrary", "arbitrary"),
        name="fox",
    )(pf, pf, vt, pf, ck, cq)


def _dil_kernel(q_ref, k_ref, v_ref, o_ref, m_s, l_s, a_s, q4, k4, v4, m4, l4, a4):
    S = q_ref.shape[1]
    W = LANES
    lane = lax.broadcasted_iota(jnp.int32, (W, LANES), 1)
    head0 = lane < HEAD_DIM
    ri = lax.broadcasted_iota(jnp.int32, (2 * W, 2 * W), 0) % W
    ci = lax.broadcasted_iota(jnp.int32, (2 * W, 2 * W), 1)
    bias_rest = jnp.where((ci >= ri) & (ci <= ri + W), 0.0, NEG)
    bias_first = jnp.where((ci < W) & (ci <= ri), 0.0, NEG)
    ones = jnp.ones((2 * W, LANES), BF16)

    assert DILATED_PATTERNS == ((W, 1), (4 * W, 4), (16 * W, 16)) and S % (16 * W * 2) == 0
    assert (S // W) % DIL_UNROLL == 0
    L4 = S // 4

    def block_stats(q, kb, vb, first):
        qs = jnp.concatenate([jnp.where(head0, q, 0.0), jnp.where(head0, 0.0, q)], axis=0).astype(BF16)
        vb2 = jnp.concatenate([vb.astype(BF16), ones], axis=1)
        s = lax.dot_general(qs, kb.astype(BF16), NT_DIMS, preferred_element_type=F32)
        s = s + jnp.where(first, bias_first, bias_rest)
        m2 = jnp.max(s, axis=1, keepdims=True)
        out = jnp.dot(jnp.exp2((s - m2).astype(BF16)), vb2, preferred_element_type=F32)
        a_u = jnp.where(head0, out[0:W, 0:LANES], out[W:2 * W, 0:LANES])
        l_u = jnp.where(head0, out[0:W, LANES:2 * LANES], out[W:2 * W, LANES:2 * LANES])
        m_u = jnp.where(head0, m2[0:W], m2[W:2 * W])
        return m_u, l_u, a_u

    def merge(idx, stats):
        m_u, l_u, a_u = stats
        m_o = m4[idx, :]
        m_n = jnp.maximum(m_o, m_u)
        e_o = jnp.exp2(m_o - m_n)
        e_u = jnp.exp2(m_u - m_n)
        m4[idx, :] = m_n
        l4[idx, :] = l4[idx, :] * e_o + l_u * e_u
        a4[idx, :] = a4[idx, :] * e_o + a_u * e_u

    def for_blocks(unit):
        def group(t, _):
            for uu in range(DIL_UNROLL):
                unit(t * DIL_UNROLL + uu)
            return 0
        lax.fori_loop(0, S // W // DIL_UNROLL, group, 0)

    def unit1(u):
        qidx = pl.ds(pl.multiple_of(u * W, W), W)
        kidx = pl.ds(pl.multiple_of(jnp.maximum(u - 1, 0) * W, W), 2 * W)
        m_s[qidx, :], l_s[qidx, :], a_s[qidx, :] = block_stats(q_ref[0, qidx, :], k_ref[0, kidx, :],
                                                               v_ref[0, kidx, :], u == 0)
    for_blocks(unit1)

    def reorder(c, _):
        for r in range(4):
            src = pl.ds(c * (4 * W) + r, W, stride=4)
            dst = pl.ds(pl.multiple_of(r * L4 + c * W, W), W)
            q4[dst, :] = q_ref[0, src, :]
            k4[dst, :] = k_ref[0, src, :]
            v4[dst, :] = v_ref[0, src, :]
            m4[dst, :] = m_s[src, :]
            l4[dst, :] = l_s[src, :]
            a4[dst, :] = a_s[src, :]
        return 0
    lax.fori_loop(0, L4 // W, reorder, 0)

    nb4 = L4 // W

    def unit4(u):
        n = u % nb4
        base = (u // nb4) * L4
        qidx = pl.ds(pl.multiple_of(base + n * W, W), W)
        kidx = pl.ds(pl.multiple_of(base + jnp.maximum(n - 1, 0) * W, W), 2 * W)
        merge(qidx, block_stats(q4[qidx, :], k4[kidx, :], v4[kidx, :], n == 0))
    for_blocks(unit4)

    nb16 = S // (16 * W)

    def unit16(u):
        n = u % nb16
        r16 = u // nb16
        base = (r16 % 4) * L4 + r16 // 4
        qidx = pl.ds(base + n * (4 * W), W, stride=4)
        kidx = pl.ds(base + jnp.maximum(n - 1, 0) * (4 * W), 2 * W, stride=4)
        merge(qidx, block_stats(q4[qidx, :], k4[kidx, :], v4[kidx, :], n == 0))
    for_blocks(unit16)

    def finish(c, _):
        for r in range(4):
            src = pl.ds(pl.multiple_of(r * L4 + c * W, W), W)
            a_s[pl.ds(c * (4 * W) + r, W, stride=4), :] = a4[src, :] * (1.0 / l4[src, :])
        return 0
    lax.fori_loop(0, L4 // W, finish, 0)
    o_ref[0] = a_s[...].astype(BF16)


def _dilated_attention(pd):
    B, S, _ = pd.shape
    nhp = DIL_W // LANES
    return pl.pallas_call(
        _dil_kernel,
        grid=(B, nhp),
        in_specs=[pl.BlockSpec((1, S, LANES), lambda b, h: (b, 0, h)),
                  pl.BlockSpec((1, S, LANES), lambda b, h: (b, 0, nhp + h)),
                  pl.BlockSpec((1, S, LANES), lambda b, h: (b, 0, 2 * nhp + h))],
        out_specs=pl.BlockSpec((1, S, LANES), lambda b, h: (b, 0, h)),
        out_shape=jax.ShapeDtypeStruct((B, S, DIL_W), BF16),
        scratch_shapes=[pltpu.VMEM((S, LANES), F32)] * 9,
        compiler_params=_params("arbitrary", "arbitrary"),
        name="dil",
    )(pd, pd, pd)


def _out_kernel(of_ref, od_ref, x_ref, g1_ref, sc2_ref, sh2_ref, wo_ref, lng_ref, lnb_ref, wrh_ref, wrl_ref, br_ref,
                x1_ref, xs_ref, meta_ref, cnt_ref, h2_s, pos_v, pos_s, sem):
    tm = x_ref.shape[1]
    y = (jnp.dot(of_ref[0], wo_ref[0:FOX_W, :], preferred_element_type=F32)
         + jnp.dot(od_ref[0], wo_ref[FOX_W:FOX_W + DIL_W, :], preferred_element_type=F32))
    x1 = _layer_norm(DEEPNORM_ALPHA * x_ref[0] + g1_ref[0] * y, lng_ref[...], lnb_ref[...])
    x1_ref[0] = x1
    h2 = x1 * (1.0 + sc2_ref[0]) + sh2_ref[0]
    for s in range(SUBLANES):
        h2_s[pl.ds(s, tm, stride=SUBLANES), :] = h2[:, s * LANES:(s + 1) * LANES]

    h_hi = h2.astype(BF16)
    h_lo = (h2 - h_hi.astype(F32)).astype(BF16)
    lg = (jnp.dot(h_hi, wrh_ref[...], preferred_element_type=F32)
          + jnp.dot(h_lo, wrh_ref[...], preferred_element_type=F32)
          + jnp.dot(h_hi, wrl_ref[...], preferred_element_type=F32)) + br_ref[...]
    lane = lax.broadcasted_iota(jnp.int32, (tm, LANES), 1)
    lanef = lane.astype(F32)
    far = float(LANES)
    is_g = lane < N_GROUPS
    gl = jnp.where(is_g, lg, NEG)
    gmax = jnp.max(gl, axis=1, keepdims=True)
    gidx = jnp.min(jnp.where(gl == gmax, lanef, far), axis=1, keepdims=True)
    p_g = 1.0 / jnp.sum(jnp.where(is_g, jnp.exp(gl - gmax), 0.0), axis=1, keepdims=True)
    e_lo = N_GROUPS + gidx * EXPERTS_PER_GROUP
    in_group = (lanef >= e_lo) & (lanef < e_lo + EXPERTS_PER_GROUP)
    el = jnp.where(in_group, lg, NEG)
    v1 = jnp.max(el, axis=1, keepdims=True)
    i1 = jnp.min(jnp.where(in_group & (el == v1), lanef, far), axis=1, keepdims=True)
    rest = in_group & (lanef != i1)
    el2 = jnp.where(rest, lg, NEG)
    v2 = jnp.max(el2, axis=1, keepdims=True)
    i2 = jnp.min(jnp.where(rest & (el2 == v2), lanef, far), axis=1, keepdims=True)
    e21 = jnp.exp(v2 - v1)
    w1 = p_g / (1.0 + e21)
    w2 = p_g * e21 / (1.0 + e21)

    pick1 = lanef == i1
    pick2 = lanef == i2
    onehot = jnp.where(pick1 | pick2, 1.0, 0.0)
    rr = lax.broadcasted_iota(jnp.int32, (tm, tm), 0)
    cc = lax.broadcasted_iota(jnp.int32, (tm, tm), 1)
    strict_lower = jnp.where(cc < rr, 1.0, 0.0).astype(BF16)
    before = jnp.dot(strict_lower, onehot.astype(BF16), preferred_element_type=F32)
    cnt = jnp.broadcast_to(jnp.sum(onehot, axis=0, keepdims=True), (SUBLANES, LANES))
    lane8 = lax.broadcasted_iota(jnp.int32, (SUBLANES, LANES), 1)
    incl = cnt
    for sh in (1, 2, 4, 8, 16, 32, 64):
        incl = incl + jnp.where(lane8 >= sh, pltpu.roll(incl, sh, 1), 0.0)
    start = before + (incl - cnt)[0:1]
    d1 = jnp.sum(jnp.where(pick1, start, 0.0), axis=1, keepdims=True)
    d2 = jnp.sum(jnp.where(pick2, start, 0.0), axis=1, keepdims=True)
    cnt_ref[...] = cnt

    packed = jnp.zeros((tm, LANES), F32)
    for k, val in enumerate((d1, d2, w1, w2)):
        packed = jnp.where(lane == k, val, packed)
    meta = packed.T[0:SUBLANES, :]
    meta_ref[0] = meta
    pos_v[...] = meta.astype(jnp.int32)
    to_smem = pltpu.make_async_copy(pos_v, pos_s, sem)
    to_smem.start()
    to_smem.wait()

    def place(t8, _):
        for u in range(SUBLANES):
            t = t8 * SUBLANES + u
            row = h2_s[pl.ds(pl.multiple_of(t * SUBLANES, SUBLANES), SUBLANES), :]
            for k in range(TOP_K):
                xs_ref[pl.ds(pl.multiple_of(pos_s[k, t] * SUBLANES, SUBLANES), SUBLANES), :] = row
        return 0

    lax.fori_loop(0, tm // SUBLANES, place, 0)


def _out_and_route(of, od, x, gate1, scale2, shift2, w_out, ln_g, ln_b, w_rg, b_rg, w_re, b_re, tm):
    B, S, D = x.shape
    n_r = N_GROUPS + N_EXPERTS
    w_r = jnp.pad(jnp.concatenate([w_rg, w_re], axis=1), ((0, 0), (0, LANES - n_r)))
    b_r = jnp.pad(jnp.concatenate([b_rg, b_re]), (0, LANES - n_r)).reshape(1, LANES)
    w_r_hi = w_r.astype(BF16)
    w_r_lo = (w_r - w_r_hi.astype(F32)).astype(BF16)
    nt = S // tm
    tile = lambda w: pl.BlockSpec((1, tm, w), lambda b, i: (b, i, 0))
    per_batch = pl.BlockSpec((1, 1, D), lambda b, i: (b, 0, 0))
    const = lambda shape: pl.BlockSpec(shape, lambda b, i: (0,) * len(shape))
    return pl.pallas_call(
        _out_kernel,
        grid=(B, S // tm),
        in_specs=[tile(FOX_W), tile(DIL_W), tile(D), per_batch, per_batch, per_batch,
                  const((D, D)), const((1, D)), const((1, D)), const((D, LANES)), const((D, LANES)),
                  const((1, LANES))],
        out_specs=[tile(D),
                   pl.BlockSpec((TOP_K * tm * SUBLANES, LANES), lambda b, i: (b * nt + i, 0)),
                   pl.BlockSpec((1, SUBLANES, tm), lambda b, i: (b * nt + i, 0, 0)),
                   pl.BlockSpec((SUBLANES, LANES), lambda b, i: (b * nt + i, 0))],
        out_shape=[jax.ShapeDtypeStruct((B, S, D), F32),
                   jax.ShapeDtypeStruct((TOP_K * B * S * SUBLANES, LANES), F32),
                   jax.ShapeDtypeStruct((B * nt, SUBLANES, tm), F32),
                   jax.ShapeDtypeStruct((B * nt * SUBLANES, LANES), F32)],
        scratch_shapes=[pltpu.VMEM((tm * SUBLANES, LANES), F32), pltpu.VMEM((SUBLANES, tm), jnp.int32),
                        pltpu.SMEM((SUBLANES, tm), jnp.int32), pltpu.SemaphoreType.DMA(())],
        compiler_params=_params("arbitrary", "arbitrary"),
        name="out",
    )(of, od, x, gate1.reshape(B, 1, D), scale2.reshape(B, 1, D), shift2.reshape(B, 1, D),
      w_out.astype(BF16), ln_g.reshape(1, D), ln_b.reshape(1, D), w_r_hi, w_r_lo, b_r)


RUN_BITS = MOE_BLOCK.bit_length()
MAX_RUNS = 80
ST_EXPERT, ST_TILE, ST_OFF, ST_DONE = range(4)


def _copy_rows(src_ref, src_row, dst_ref, dst_row, n, sem, wait):
    pos = 0
    for bit in reversed(range(RUN_BITS)):
        size = 1 << bit

        @pl.when((n & size) != 0)
        def _(pos=pos, size=size):
            cp = pltpu.make_async_copy(
                src_ref.at[pl.ds(pl.multiple_of((src_row + pos) * SUBLANES, SUBLANES), size * SUBLANES), :],
                dst_ref.at[pl.ds(pl.multiple_of((dst_row + pos) * SUBLANES, SUBLANES), size * SUBLANES), :], sem)
            if wait:
                cp.wait()
            else:
                cp.start()

        pos = pos + (n & size)


def _moe_kernel(blk_e_ref, nused_ref, cnt_ref, off_ref, tot_ref, x_ref, wg_ref, wu_ref, wd_ref, y_ref,
                xbuf0, xbuf1, ybuf0, ybuf1, mid_s, st, runs, sruns, gsem, ssem):
    del blk_e_ref
    i = pl.program_id(0)
    nused = nused_ref[0]
    D = xbuf0.shape[0] // MOE_BLOCK * LANES
    nk = D // MXU_DIM
    n_tiles = cnt_ref.shape[0] // N_EXPERTS
    rows_per_tile = x_ref.shape[0] // SUBLANES // n_tiles

    def gather_next(xdst, s):
        def next_expert(c):
            return c[0] + 1, 0, 0, 0

        def exhausted(c):
            return (c[0] < N_EXPERTS) & (c[3] >= tot_ref[jnp.minimum(c[0], N_EXPERTS - 1)])

        e, tile, off, done = lax.while_loop(
            exhausted, next_expert, (st[ST_EXPERT], st[ST_TILE], st[ST_OFF], st[ST_DONE]))
        ec = jnp.minimum(e, N_EXPERTS - 1)
        total = jnp.where(e < N_EXPERTS, tot_ref[ec], 0)

        def more(c):
            return (c[0] < MOE_BLOCK) & (c[4] < total)

        def take_run(c):
            filled, k, tile, off, done = c
            run = cnt_ref[tile * N_EXPERTS + ec]
            take = jnp.minimum(run - off, MOE_BLOCK - filled)
            src = tile * rows_per_tile + off_ref[tile * N_EXPERTS + ec] + off
            _copy_rows(x_ref, src, xdst, filled, take, gsem.at[s], wait=False)
            base = (s * MAX_RUNS + k) * 3
            runs[base] = src
            runs[base + 1] = filled
            runs[base + 2] = take
            run_done = off + take >= run
            return (filled + take, k + jnp.where(take > 0, 1, 0), jnp.where(run_done, tile + 1, tile),
                    jnp.where(run_done, 0, off + take), done + take)

        filled, k, tile, off, done = lax.while_loop(more, take_run, (0, 0, tile, off, done))
        _copy_rows(x_ref, 0, xdst, filled, jnp.where(total > 0, MOE_BLOCK - filled, 0), gsem.at[s], wait=False)
        runs[(2 * MAX_RUNS + s) * 3] = k
        st[ST_EXPERT] = e
        st[ST_TILE] = tile
        st[ST_OFF] = off
        st[ST_DONE] = done

    def wait_gather(xdst, s):
        pltpu.make_async_copy(x_ref.at[pl.ds(0, MOE_BLOCK * SUBLANES), :], xdst, gsem.at[s]).wait()

    def scatter_block(ysrc, s):
        n = runs[(2 * MAX_RUNS + s) * 3]
        sruns[(2 * MAX_RUNS + s) * 3] = n

        def one(k, _):
            base = (s * MAX_RUNS + k) * 3
            _copy_rows(ysrc, runs[base + 1], y_ref, runs[base], runs[base + 2], ssem.at[s], wait=False)
            for f in range(3):
                sruns[base + f] = runs[base + f]
            return 0

        lax.fori_loop(0, n, one, 0)

    def wait_scatter(ysrc, s):
        def one(k, _):
            base = (s * MAX_RUNS + k) * 3
            _copy_rows(ysrc, sruns[base + 1], y_ref, sruns[base], sruns[base + 2], ssem.at[s], wait=True)
            return 0

        lax.fori_loop(0, sruns[(2 * MAX_RUNS + s) * 3], one, 0)

    def compute(xcur, ycur):
        xk = [jnp.concatenate([xcur[pl.ds(2 * j, MOE_BLOCK, stride=SUBLANES), :],
                               xcur[pl.ds(2 * j + 1, MOE_BLOCK, stride=SUBLANES), :]], axis=1).astype(BF16)
              for j in range(nk)]
        for c in range(D_EXPERT // MXU_DIM):
            cols = slice(c * MXU_DIM, (c + 1) * MXU_DIM)
            g = sum(jnp.dot(xk[j], wg_ref[0, MXU_DIM * j:MXU_DIM * (j + 1), cols].astype(BF16),
                            preferred_element_type=F32) for j in range(nk))
            u = sum(jnp.dot(xk[j], wu_ref[0, MXU_DIM * j:MXU_DIM * (j + 1), cols].astype(BF16),
                            preferred_element_type=F32) for j in range(nk))
            mid_s[:, cols] = (g * _sigmoid(g) * u).astype(BF16)
        mid = mid_s[...]
        for c in range(D // MXU_DIM):
            y = jnp.dot(mid, wd_ref[0, :, c * MXU_DIM:(c + 1) * MXU_DIM].astype(BF16), preferred_element_type=F32)
            ycur[pl.ds(2 * c, MOE_BLOCK, stride=SUBLANES), :] = y[:, 0:LANES]
            ycur[pl.ds(2 * c + 1, MOE_BLOCK, stride=SUBLANES), :] = y[:, LANES:2 * LANES]

    def step(s, xcur, xnxt, ycur, yprv):
        o = 1 - s
        wait_gather(xcur, s)

        @pl.when(i + 1 < nused)
        def _():
            gather_next(xnxt, o)

        @pl.when(i >= 2)
        def _():
            wait_scatter(ycur, s)

        compute(xcur, ycur)
        scatter_block(ycur, s)

        @pl.when(i + 1 == nused)
        def _():
            @pl.when(i >= 1)
            def _():
                wait_scatter(yprv, o)

            wait_scatter(ycur, s)

    @pl.when(i == 0)
    def _():
        for f in range(4):
            st[f] = 0
        gather_next(xbuf0, 0)

    @pl.when((i < nused) & (i % 2 == 0))
    def _():
        step(0, xbuf0, xbuf1, ybuf0, ybuf1)

    @pl.when((i < nused) & (i % 2 == 1))
    def _():
        step(1, xbuf1, xbuf0, ybuf1, ybuf0)


def _expert_mlp(x_sorted, cnt, off, tot, blk_e, nused, w_gate, w_up, w_down):
    D = w_gate.shape[1]
    nblk = blk_e.shape[0]
    assert cnt.shape[0] // N_EXPERTS + 2 <= MAX_RUNS
    wmap = lambda i, be, nu, c, o, t: (be[i], 0, 0)
    return pl.pallas_call(
        _moe_kernel,
        grid_spec=pltpu.PrefetchScalarGridSpec(
            num_scalar_prefetch=5,
            grid=(nblk,),
            in_specs=[pl.BlockSpec(memory_space=pl.ANY),
                      pl.BlockSpec((1, D, D_EXPERT), wmap),
                      pl.BlockSpec((1, D, D_EXPERT), wmap),
                      pl.BlockSpec((1, D_EXPERT, D), wmap)],
            out_specs=pl.BlockSpec(memory_space=pl.ANY),
            scratch_shapes=[pltpu.VMEM((MOE_BLOCK * SUBLANES, LANES), F32)] * 4
            + [pltpu.VMEM((MOE_BLOCK, D_EXPERT), BF16),
               pltpu.SMEM((4,), jnp.int32),
               pltpu.SMEM(((2 * MAX_RUNS + 2) * 3,), jnp.int32), pltpu.SMEM(((2 * MAX_RUNS + 2) * 3,), jnp.int32),
               pltpu.SemaphoreType.DMA((2,)), pltpu.SemaphoreType.DMA((2,))]),
        out_shape=jax.ShapeDtypeStruct(x_sorted.shape, F32),
        compiler_params=_params("arbitrary"),
        name="moe",
    )(blk_e, nused, cnt, off, tot, x_sorted, w_gate, w_up, w_down)


def _final_kernel(pos_ref, wgt_ref, y_ref, x1_ref, g2_ref, lng_ref, lnb_ref, o_ref, mix_s):
    tm = x1_ref.shape[1]

    def combine(t8, _):
        for u in range(SUBLANES):
            t = t8 * SUBLANES + u
            rows = [y_ref[pl.ds(pl.multiple_of(pos_ref[k, t] * SUBLANES, SUBLANES), SUBLANES), :]
                    for k in range(TOP_K)]
            mix_s[pl.ds(pl.multiple_of(t * SUBLANES, SUBLANES), SUBLANES), :] = (
                wgt_ref[TOP_K, t] * rows[0] + wgt_ref[TOP_K + 1, t] * rows[1])
        return 0

    lax.fori_loop(0, tm // SUBLANES, combine, 0)
    y = jnp.concatenate([mix_s[pl.ds(s, tm, stride=SUBLANES), :] for s in range(SUBLANES)], axis=1)
    o_ref[0] = _layer_norm(DEEPNORM_ALPHA * x1_ref[0] + g2_ref[0] * y, lng_ref[...], lnb_ref[...])


def _combine_and_norm(y_sorted, meta, x1, gate2, ln_g, ln_b, tm):
    B, S, D = x1.shape
    nt = S // tm
    pos = meta.astype(jnp.int32).reshape(B * nt * SUBLANES, tm)
    wgt = meta.reshape(B * nt * SUBLANES, tm)
    smem_tile = pl.BlockSpec((SUBLANES, tm), lambda b, i: (b * nt + i, 0), memory_space=pltpu.SMEM)
    return pl.pallas_call(
        _final_kernel,
        grid=(B, nt),
        in_specs=[smem_tile, smem_tile,
                  pl.BlockSpec((TOP_K * tm * SUBLANES, LANES), lambda b, i: (b * nt + i, 0)),
                  pl.BlockSpec((1, tm, D), lambda b, i: (b, i, 0)),
                  pl.BlockSpec((1, 1, D), lambda b, i: (b, 0, 0)),
                  pl.BlockSpec((1, D), lambda b, i: (0, 0)),
                  pl.BlockSpec((1, D), lambda b, i: (0, 0))],
        out_specs=pl.BlockSpec((1, tm, D), lambda b, i: (b, i, 0)),
        out_shape=jax.ShapeDtypeStruct((B, S, D), F32),
        scratch_shapes=[pltpu.VMEM((tm * SUBLANES, LANES), F32)],
        compiler_params=_params("arbitrary", "arbitrary"),
        name="final",
    )(pos, wgt, y_sorted, x1, gate2.reshape(B, 1, D), ln_g.reshape(1, D), ln_b.reshape(1, D))


def kernel(x, c, positions, w_ada, b_ada, w_in, b_forget, w_out, ln1_g, ln1_b, w_router_group, b_router_group,
           w_router_expert, b_router_expert, w_up, w_gate, w_down, ln2_g, ln2_b):
    B, S, D = x.shape
    N = B * S
    assert D == FOX_W + DIL_W and S % 2048 == 0
    mod = _modulation(c, w_ada, b_ada)
    shift1, scale1, gate1, shift2, scale2, gate2 = jnp.split(mod, 6, axis=-1)

    pf, vt, pd, log_f = _projection(x, positions, scale1, shift1, w_in, b_forget, tm=512)
    cum, cum_t = _cumulative_gate(log_f)
    nhp = FOX_W // LANES
    ck = cum[..., :FOX_HEADS].reshape(B, S, nhp, HEADS_PER_VREG).transpose(0, 2, 1, 3)
    cq = cum_t.reshape(B, nhp, HEADS_PER_VREG, S)
    of = _fox_attention(pf, vt, ck, cq)
    od = _dilated_attention(pd)

    tm = 512
    x1, x_sorted, meta, tile_cnt = _out_and_route(of, od, x, gate1, scale2, shift2, w_out, ln1_g, ln1_b,
                                                  w_router_group, b_router_group, w_router_expert, b_router_expert,
                                                  tm=tm)

    cnt = tile_cnt[::SUBLANES, N_GROUPS:N_GROUPS + N_EXPERTS].astype(jnp.int32)
    off = jnp.cumsum(cnt, axis=1) - cnt
    tot = jnp.sum(cnt, axis=0)
    pend = jnp.cumsum((tot + MOE_BLOCK - 1) // MOE_BLOCK * MOE_BLOCK)
    nblk = (N * TOP_K) // MOE_BLOCK + N_EXPERTS
    nused = pend[-1:] // MOE_BLOCK
    blk_start = jnp.arange(nblk, dtype=jnp.int32) * MOE_BLOCK
    blk_e = jnp.sum((pend[None, :] <= jnp.minimum(blk_start, pend[-1] - 1)[:, None]).astype(jnp.int32), axis=1)

    y_sorted = _expert_mlp(x_sorted, cnt.reshape(-1), off.reshape(-1), tot, blk_e, nused, w_gate, w_up, w_down)
    return _combine_and_norm(y_sorted, meta, x1, gate2, ln2_g, ln2_b, tm=tm)
```

```python
import functools

import jax
import jax.numpy as jnp
import numpy as np
from jax import lax
from jax.experimental import pallas as pl
from jax.experimental.pallas import tpu as pltpu

HEAD_DIM = 64
FOX_HEADS = 8
DIL_HEADS = 8
FOX_W = FOX_HEADS * HEAD_DIM
DIL_W = DIL_HEADS * HEAD_DIM
DILATED_PATTERNS = ((128, 1), (512, 4), (2048, 16))
ROPE_THETA = 500000.0
ROT_DIM = HEAD_DIM // 4
N_GROUPS = 4
EXPERTS_PER_GROUP = 8
N_EXPERTS = N_GROUPS * EXPERTS_PER_GROUP
TOP_K = 2
D_EXPERT = 512
MOE_BLOCK = 256
LN_EPS = 1e-5
NEG = -1e30
DEPTH = 1
DEEPNORM_ALPHA = (2 * DEPTH) ** 0.25
QK_SCALE = HEAD_DIM ** -0.5
LOG2E = 1.4426950408889634

LANES = 128
SUBLANES = 8
BF16_SUBLANES = 16
MXU_DIM = 256
HEADS_PER_VREG = LANES // HEAD_DIM
V_ROWS = HEAD_DIM + BF16_SUBLANES
FOX_TILE = 512
DIL_UNROLL = 4
VMEM_LIMIT = 56 * 1024 * 1024

F32 = jnp.float32
BF16 = jnp.bfloat16
HIGHEST = lax.Precision.HIGHEST
NT_DIMS = (((1,), (1,)), ((), ()))


def _params(*sem):
    return pltpu.CompilerParams(dimension_semantics=sem, vmem_limit_bytes=VMEM_LIMIT)


def _sigmoid(v):
    return 1.0 / (1.0 + jnp.exp(-v))


def _layer_norm(v, g, b):
    mu = jnp.mean(v, axis=-1, keepdims=True)
    d = v - mu
    var = jnp.mean(d * d, axis=-1, keepdims=True)
    return d * lax.rsqrt(var + LN_EPS) * g + b


def _mod_kernel(c_ref, w_ref, b_ref, o_ref):
    c = c_ref[...]
    o_ref[...] = jnp.dot(c * _sigmoid(c), w_ref[...], precision=HIGHEST,
                         preferred_element_type=F32) + b_ref[...]


def _modulation(c, w_ada, b_ada):
    B, D = c.shape
    cols = w_ada.shape[1]
    tn = 1024
    return pl.pallas_call(
        _mod_kernel,
        grid=(cols // tn,),
        in_specs=[pl.BlockSpec((B, D), lambda j: (0, 0)),
                  pl.BlockSpec((D, tn), lambda j: (0, j)),
                  pl.BlockSpec((1, tn), lambda j: (0, j))],
        out_specs=pl.BlockSpec((B, tn), lambda j: (0, j)),
        out_shape=jax.ShapeDtypeStruct((B, cols), F32),
        compiler_params=_params("arbitrary"),
        name="mod",
    )(c, w_ada, b_ada.reshape(1, cols))


def _proj_kernel(x_ref, pos_ref, sc_ref, sh_ref, wf_ref, wvt_ref, one_ref, wd_ref, wff_ref, bf_ref, invf_ref,
                 sgn_ref, pf_ref, vt_ref, pd_ref, lf_ref):
    tm = x_ref.shape[1]
    h = (x_ref[0] * (1.0 + sc_ref[0]) + sh_ref[0]).astype(BF16)

    for ci in range(3):
        acc = jnp.dot(h, wf_ref[:, ci * FOX_W:(ci + 1) * FOX_W], preferred_element_type=F32)
        if ci == 0:
            acc = acc * (QK_SCALE * LOG2E)
        pf_ref[0, :, ci * FOX_W:(ci + 1) * FOX_W] = acc.astype(BF16)
    vt = lax.dot_general(wvt_ref[...], h, NT_DIMS, preferred_element_type=F32) + one_ref[...]
    vt_ref[0] = vt.astype(BF16)

    z = jnp.dot(h, wff_ref[...], preferred_element_type=F32) + bf_ref[...]
    lf_ref[0] = (jnp.minimum(z, 0.0) - jnp.log1p(jnp.exp(-jnp.abs(z)))) * LOG2E

    ang = pos_ref[0].astype(F32) * invf_ref[...]
    cs = jnp.cos(ang)
    sn = jnp.sin(ang) * sgn_ref[...]
    lane = lax.broadcasted_iota(jnp.int32, (tm, LANES), 1)
    first_half = (lane % HEAD_DIM) < (ROT_DIM // 2)
    for ci in range(3):
        acc = jnp.dot(h, wd_ref[:, ci * DIL_W:(ci + 1) * DIL_W], preferred_element_type=F32)
        if ci == 2:
            pd_ref[0, :, ci * DIL_W:(ci + 1) * DIL_W] = acc
            continue
        for j in range(DIL_W // LANES):
            t = acc[:, j * LANES:(j + 1) * LANES]
            partner = jnp.where(first_half, pltpu.roll(t, LANES - ROT_DIM // 2, 1),
                                pltpu.roll(t, ROT_DIM // 2, 1))
            r = t * cs + partner * sn
            if ci == 0:
                r = r * (QK_SCALE * LOG2E)
            pd_ref[0, :, ci * DIL_W + j * LANES:ci * DIL_W + (j + 1) * LANES] = r


def _projection(x, positions, scale1, shift1, w_in, b_forget, tm):
    B, S, D = x.shape
    o = np.cumsum((0, FOX_W, FOX_W, FOX_W, FOX_W, FOX_HEADS, DIL_W, DIL_W, DIL_W))
    w_fox = jnp.concatenate([w_in[:, o[0]:o[2]], w_in[:, o[3]:o[4]]], axis=1).astype(BF16)
    w_vt = jnp.pad(w_in[:, o[2]:o[3]].T.reshape(FOX_HEADS, HEAD_DIM, D),
                   ((0, 0), (0, V_ROWS - HEAD_DIM), (0, 0))).reshape(FOX_HEADS * V_ROWS, D).astype(BF16)
    ones_row = jnp.asarray((np.arange(FOX_HEADS * V_ROWS) % V_ROWS == HEAD_DIM).astype(np.float32)
                           ).reshape(FOX_HEADS * V_ROWS, 1)
    w_ff = jnp.pad(w_in[:, o[4]:o[5]], ((0, 0), (0, LANES - FOX_HEADS))).astype(BF16)
    w_dil = w_in[:, o[5]:o[8]].astype(BF16)
    b_f = jnp.pad(b_forget, (0, LANES - FOX_HEADS)).reshape(1, LANES)
    e = np.arange(LANES) % HEAD_DIM
    inv_freq = ROPE_THETA ** (-jnp.arange(0, ROT_DIM, 2, dtype=F32) / ROT_DIM)
    invf = jnp.where(e < ROT_DIM, jnp.tile(inv_freq, LANES // (ROT_DIM // 2)), 0.0).reshape(1, LANES)
    sgn = jnp.asarray(np.where(e < ROT_DIM // 2, -1.0, np.where(e < ROT_DIM, 1.0, 0.0)), F32).reshape(1, LANES)
    const = lambda shape: pl.BlockSpec(shape, lambda b, i: (0,) * len(shape))
    return pl.pallas_call(
        _proj_kernel,
        grid=(B, S // tm),
        in_specs=[pl.BlockSpec((1, tm, D), lambda b, i: (b, i, 0)),
                  pl.BlockSpec((1, tm, 1), lambda b, i: (b, i, 0)),
                  pl.BlockSpec((1, 1, D), lambda b, i: (b, 0, 0)),
                  pl.BlockSpec((1, 1, D), lambda b, i: (b, 0, 0)),
                  const((D, 3 * FOX_W)), const((FOX_HEADS * V_ROWS, D)), const((FOX_HEADS * V_ROWS, 1)),
                  const((D, 3 * DIL_W)), const((D, LANES)),
                  const((1, LANES)), const((1, LANES)), const((1, LANES))],
        out_specs=[pl.BlockSpec((1, tm, 3 * FOX_W), lambda b, i: (b, i, 0)),
                   pl.BlockSpec((1, FOX_HEADS * V_ROWS, tm), lambda b, i: (b, 0, i)),
                   pl.BlockSpec((1, tm, 3 * DIL_W), lambda b, i: (b, i, 0)),
                   pl.BlockSpec((1, tm, LANES), lambda b, i: (b, i, 0))],
        out_shape=[jax.ShapeDtypeStruct((B, S, 3 * FOX_W), BF16),
                   jax.ShapeDtypeStruct((B, FOX_HEADS * V_ROWS, S), BF16),
                   jax.ShapeDtypeStruct((B, S, 3 * DIL_W), F32),
                   jax.ShapeDtypeStruct((B, S, LANES), F32)],
        compiler_params=_params("arbitrary", "arbitrary"),
        name="proj",
    )(x, positions.reshape(B, S, 1), scale1.reshape(B, 1, D), shift1.reshape(B, 1, D),
      w_fox, w_vt, ones_row, w_dil, w_ff, b_f, invf, sgn)


def _cum_kernel(lf_ref, c_ref, ct_ref):
    S = lf_ref.shape[1]
    r = lax.broadcasted_iota(jnp.int32, (LANES, LANES), 0)
    c = lax.broadcasted_iota(jnp.int32, (LANES, LANES), 1)
    tri = (c <= r).astype(F32)

    carry = jnp.zeros((1, LANES), F32)
    for j in range(S // LANES):
        rows = slice(j * LANES, (j + 1) * LANES)
        cum = jnp.dot(tri, lf_ref[0, rows, :], precision=HIGHEST, preferred_element_type=F32) + carry
        c_ref[0, rows, :] = cum
        ct_ref[0, :, rows] = cum.T[0:FOX_HEADS, :]
        carry = cum[LANES - 1:LANES, :]


def _cumulative_gate(log_f):
    B, S, _ = log_f.shape
    return pl.pallas_call(
        _cum_kernel,
        grid=(B,),
        in_specs=[pl.BlockSpec((1, S, LANES), lambda b: (b, 0, 0))],
        out_specs=[pl.BlockSpec((1, S, LANES), lambda b: (b, 0, 0)),
                   pl.BlockSpec((1, FOX_HEADS, S), lambda b: (b, 0, 0))],
        out_shape=[jax.ShapeDtypeStruct((B, S, LANES), F32),
                   jax.ShapeDtypeStruct((B, FOX_HEADS, S), F32)],
        compiler_params=_params("arbitrary"),
        name="cum",
    )(log_f)


def _fox_kernel(q_ref, k_ref, vt_ref, g_ref, ck_ref, cq_ref, o_ref, s_scr):
    S = q_ref.shape[1]
    T = FOX_TILE
    lane = lax.broadcasted_iota(jnp.int32, (T, LANES), 1)

    def q_block(i, _):
        qoff = pl.multiple_of(i * T, T)
        q = q_ref[0, pl.ds(qoff, T), :]
        qh = [jnp.where((lane // HEAD_DIM) == hh, q, jnp.zeros_like(q)) for hh in range(HEADS_PER_VREG)]
        cq = [cq_ref[0, 0, hh:hh + 1, pl.ds(qoff, T)] for hh in range(HEADS_PER_VREG)]

        def scores_into(buf, b):
            kb = k_ref[0, pl.ds(pl.multiple_of(b * T, T), T), :]
            for hh in range(HEADS_PER_VREG):
                s_scr[buf, hh] = lax.dot_general(kb, qh[hh], NT_DIMS, preferred_element_type=F32)

        def attend(buf, b, carry, masked):
            koff = pl.multiple_of(b * T, T)
            new = []
            for hh in range(HEADS_PER_VREG):
                m, acc = carry[hh]

                def scores(c0, rows):
                    s = s_scr[buf, hh, c0:c0 + rows, :] - ck_ref[0, 0, pl.ds(koff + c0, rows), hh:hh + 1]
                    if masked:
                        s = jnp.where(lax.broadcasted_iota(jnp.int32, (rows, T), 0) + c0
                                      <= lax.broadcasted_iota(jnp.int32, (rows, T), 1), s, NEG)
                    return s

                m_new = jnp.maximum(m, cq[hh] + jnp.max(scores(0, T), axis=0, keepdims=True))
                row = cq[hh] - m_new
                acc = jnp.exp2(m - m_new) * acc
                for c0 in range(0, T, MXU_DIM):
                    p = jnp.exp2((scores(c0, MXU_DIM) + row).astype(BF16))
                    vt = vt_ref[0, hh * V_ROWS:(hh + 1) * V_ROWS, pl.ds(koff + c0, MXU_DIM)]
                    acc = acc + jnp.dot(vt, p, preferred_element_type=F32)
                new.append((m_new, acc))
            return tuple(new)

        def pair(t, carry):
            scores_into(1, 2 * t + 1)
            carry = attend(0, 2 * t, carry, False)
            scores_into(0, 2 * t + 2)
            return attend(1, 2 * t + 1, carry, False)

        def odd_tail(carry):
            scores_into(1, i)
            return attend(1, i, attend(0, i - 1, carry, False), True)

        def even_tail(carry):
            return attend(0, i, carry, True)

        init = tuple((jnp.full((1, T), NEG, F32), jnp.zeros((V_ROWS, T), F32)) for _ in range(HEADS_PER_VREG))
        scores_into(0, 0)
        carry = lax.fori_loop(0, i // 2, pair, init)
        carry = lax.cond(i % 2 == 1, odd_tail, even_tail, carry)
        o_t = jnp.concatenate([acc[0:HEAD_DIM] * (1.0 / acc[HEAD_DIM:HEAD_DIM + 1]) for (_, acc) in carry], axis=0)
        gate = _sigmoid(g_ref[0, pl.ds(qoff, T), :].astype(F32))
        o_ref[0, pl.ds(qoff, T), :] = (o_t.T * gate).astype(BF16)
        return 0

    lax.fori_loop(0, S // T, q_block, 0)


def _fox_attention(pf, vt, ck, cq):
    B, S, _ = pf.shape
    nhp = FOX_W // LANES
    assert S % FOX_TILE == 0
    return pl.pallas_call(
        _fox_kernel,
        grid=(B, nhp),
        in_specs=[pl.BlockSpec((1, S, LANES), lambda b, h: (b, 0, h)),
                  pl.BlockSpec((1, S, LANES), lambda b, h: (b, 0, nhp + h)),
                  pl.BlockSpec((1, HEADS_PER_VREG * V_ROWS, S), lambda b, h: (b, h, 0)),
                  pl.BlockSpec((1, S, LANES), lambda b, h: (b, 0, 2 * nhp + h)),
                  pl.BlockSpec((1, 1, S, HEADS_PER_VREG), lambda b, h: (b, h, 0, 0)),
                  pl.BlockSpec((1, 1, HEADS_PER_VREG, S), lambda b, h: (b, h, 0, 0))],
        out_specs=pl.BlockSpec((1, S, LANES), lambda b, h: (b, 0, h)),
        out_shape=jax.ShapeDtypeStruct((B, S, FOX_W), BF16),
        scratch_shapes=[pltpu.VMEM((2, HEADS_PER_VREG, FOX_TILE, FOX_TILE), F32)],
        compiler_params=_params("arbitrary", "arbitrary"),
        name="fox",
    )(pf, pf, vt, pf, ck, cq)


def _dil_kernel(q_ref, k_ref, v_ref, o_ref, m_s, l_s, a_s, q4, k4, v4, m4, l4, a4):
    S = q_ref.shape[1]
    W = LANES
    lane = lax.broadcasted_iota(jnp.int32, (W, LANES), 1)
    head0 = lane < HEAD_DIM
    ri = lax.broadcasted_iota(jnp.int32, (2 * W, 2 * W), 0) % W
    ci = lax.broadcasted_iota(jnp.int32, (2 * W, 2 * W), 1)
    bias_rest = jnp.where((ci >= ri) & (ci <= ri + W), 0.0, NEG)
    bias_first = jnp.where((ci < W) & (ci <= ri), 0.0, NEG)
    ones = jnp.ones((2 * W, LANES), BF16)

    assert DILATED_PATTERNS == ((W, 1), (4 * W, 4), (16 * W, 16)) and S % (16 * W * 2) == 0
    assert (S // W) % DIL_UNROLL == 0
    L4 = S // 4

    def block_stats(q, kb, vb, first):
        qs = jnp.concatenate([jnp.where(head0, q, 0.0), jnp.where(head0, 0.0, q)], axis=0).astype(BF16)
        vb2 = jnp.concatenate([vb.astype(BF16), ones], axis=1)
        s = lax.dot_general(qs, kb.astype(BF16), NT_DIMS, preferred_element_type=F32)
        s = s + jnp.where(first, bias_first, bias_rest)
        m2 = jnp.max(s, axis=1, keepdims=True)
        out = jnp.dot(jnp.exp2((s - m2).astype(BF16)), vb2, preferred_element_type=F32)
        a_u = jnp.where(head0, out[0:W, 0:LANES], out[W:2 * W, 0:LANES])
        l_u = jnp.where(head0, out[0:W, LANES:2 * LANES], out[W:2 * W, LANES:2 * LANES])
        m_u = jnp.where(head0, m2[0:W], m2[W:2 * W])
        return m_u, l_u, a_u

    def merge(idx, stats):
        m_u, l_u, a_u = stats
        m_o = m4[idx, :]
        m_n = jnp.maximum(m_o, m_u)
        e_o = jnp.exp2(m_o - m_n)
        e_u = jnp.exp2(m_u - m_n)
        m4[idx, :] = m_n
        l4[idx, :] = l4[idx, :] * e_o + l_u * e_u
        a4[idx, :] = a4[idx, :] * e_o + a_u * e_u

    def for_blocks(unit):
        def group(t, _):
            for uu in range(DIL_UNROLL):
                unit(t * DIL_UNROLL + uu)
            return 0
        lax.fori_loop(0, S // W // DIL_UNROLL, group, 0)

    def unit1(u):
        qidx = pl.ds(pl.multiple_of(u * W, W), W)
        kidx = pl.ds(pl.multiple_of(jnp.maximum(u - 1, 0) * W, W), 2 * W)
        m_s[qidx, :], l_s[qidx, :], a_s[qidx, :] = block_stats(q_ref[0, qidx, :], k_ref[0, kidx, :],
                                                               v_ref[0, kidx, :], u == 0)
    for_blocks(unit1)

    def reorder(c, _):
        for r in range(4):
            src = pl.ds(c * (4 * W) + r, W, stride=4)
            dst = pl.ds(pl.multiple_of(r * L4 + c * W, W), W)
            q4[dst, :] = q_ref[0, src, :]
            k4[dst, :] = k_ref[0, src, :]
            v4[dst, :] = v_ref[0, src, :]
            m4[dst, :] = m_s[src, :]
            l4[dst, :] = l_s[src, :]
            a4[dst, :] = a_s[src, :]
        return 0
    lax.fori_loop(0, L4 // W, reorder, 0)

    nb4 = L4 // W

    def unit4(u):
        n = u % nb4
        base = (u // nb4) * L4
        qidx = pl.ds(pl.multiple_of(base + n * W, W), W)
        kidx = pl.ds(pl.multiple_of(base + jnp.maximum(n - 1, 0) * W, W), 2 * W)
        merge(qidx, block_stats(q4[qidx, :], k4[kidx, :], v4[kidx, :], n == 0))
    for_blocks(unit4)

    nb16 = S // (16 * W)

    def unit16(u):
        n = u % nb16
        r16 = u // nb16
        base = (r16 % 4) * L4 + r16 // 4
        qidx = pl.ds(base + n * (4 * W), W, stride=4)
        kidx = pl.ds(base + jnp.maximum(n - 1, 0) * (4 * W), 2 * W, stride=4)
        merge(qidx, block_stats(q4[qidx, :], k4[kidx, :], v4[kidx, :], n == 0))
    for_blocks(unit16)

    def finish(c, _):
        for r in range(4):
            src = pl.ds(pl.multiple_of(r * L4 + c * W, W), W)
            a_s[pl.ds(c * (4 * W) + r, W, stride=4), :] = a4[src, :] * (1.0 / l4[src, :])
        return 0
    lax.fori_loop(0, L4 // W, finish, 0)
    o_ref[0] = a_s[...].astype(BF16)


def _dilated_attention(pd):
    B, S, _ = pd.shape
    nhp = DIL_W // LANES
    return pl.pallas_call(
        _dil_kernel,
        grid=(B, nhp),
        in_specs=[pl.BlockSpec((1, S, LANES), lambda b, h: (b, 0, h)),
                  pl.BlockSpec((1, S, LANES), lambda b, h: (b, 0, nhp + h)),
                  pl.BlockSpec((1, S, LANES), lambda b, h: (b, 0, 2 * nhp + h))],
        out_specs=pl.BlockSpec((1, S, LANES), lambda b, h: (b, 0, h)),
        out_shape=jax.ShapeDtypeStruct((B, S, DIL_W), BF16),
        scratch_shapes=[pltpu.VMEM((S, LANES), F32)] * 9,
        compiler_params=_params("arbitrary", "arbitrary"),
        name="dil",
    )(pd, pd, pd)


def _out_kernel(of_ref, od_ref, x_ref, g1_ref, sc2_ref, sh2_ref, wo_ref, lng_ref, lnb_ref, wrh_ref, wrl_ref, br_ref,
                x1_ref, xs_ref, meta_ref, rt_ref, cnt_ref, h2_s, pos_v, pos_s, sem):
    tm = x_ref.shape[1]
    y = (jnp.dot(of_ref[0], wo_ref[0:FOX_W, :], preferred_element_type=F32)
         + jnp.dot(od_ref[0], wo_ref[FOX_W:FOX_W + DIL_W, :], preferred_element_type=F32))
    x1 = _layer_norm(DEEPNORM_ALPHA * x_ref[0] + g1_ref[0] * y, lng_ref[...], lnb_ref[...])
    x1_ref[0] = x1
    h2 = x1 * (1.0 + sc2_ref[0]) + sh2_ref[0]
    for s in range(SUBLANES):
        h2_s[pl.ds(s, tm, stride=SUBLANES), :] = h2[:, s * LANES:(s + 1) * LANES]

    h_hi = h2.astype(BF16)
    h_lo = (h2 - h_hi.astype(F32)).astype(BF16)
    lg = (jnp.dot(h_hi, wrh_ref[...], preferred_element_type=F32)
          + jnp.dot(h_lo, wrh_ref[...], preferred_element_type=F32)
          + jnp.dot(h_hi, wrl_ref[...], preferred_element_type=F32)) + br_ref[...]
    lane = lax.broadcasted_iota(jnp.int32, (tm, LANES), 1)
    lanef = lane.astype(F32)
    far = float(LANES)
    is_g = lane < N_GROUPS
    gl = jnp.where(is_g, lg, NEG)
    gmax = jnp.max(gl, axis=1, keepdims=True)
    gidx = jnp.min(jnp.where(gl == gmax, lanef, far), axis=1, keepdims=True)
    p_g = 1.0 / jnp.sum(jnp.where(is_g, jnp.exp(gl - gmax), 0.0), axis=1, keepdims=True)
    e_lo = N_GROUPS + gidx * EXPERTS_PER_GROUP
    in_group = (lanef >= e_lo) & (lanef < e_lo + EXPERTS_PER_GROUP)
    el = jnp.where(in_group, lg, NEG)
    v1 = jnp.max(el, axis=1, keepdims=True)
    i1 = jnp.min(jnp.where(in_group & (el == v1), lanef, far), axis=1, keepdims=True)
    rest = in_group & (lanef != i1)
    el2 = jnp.where(rest, lg, NEG)
    v2 = jnp.max(el2, axis=1, keepdims=True)
    i2 = jnp.min(jnp.where(rest & (el2 == v2), lanef, far), axis=1, keepdims=True)
    e21 = jnp.exp(v2 - v1)
    w1 = p_g / (1.0 + e21)
    w2 = p_g * e21 / (1.0 + e21)

    pick1 = lanef == i1
    pick2 = lanef == i2
    onehot = jnp.where(pick1 | pick2, 1.0, 0.0)
    rr = lax.broadcasted_iota(jnp.int32, (tm, tm), 0)
    cc = lax.broadcasted_iota(jnp.int32, (tm, tm), 1)
    strict_lower = jnp.where(cc < rr, 1.0, 0.0).astype(BF16)
    before = jnp.dot(strict_lower, onehot.astype(BF16), preferred_element_type=F32)
    cnt = jnp.broadcast_to(jnp.sum(onehot, axis=0, keepdims=True), (SUBLANES, LANES))
    lane8 = lax.broadcasted_iota(jnp.int32, (SUBLANES, LANES), 1)
    incl = cnt
    for sh in (1, 2, 4, 8, 16, 32, 64):
        incl = incl + jnp.where(lane8 >= sh, pltpu.roll(incl, sh, 1), 0.0)
    start = before + (incl - cnt)[0:1]
    d1 = jnp.sum(jnp.where(pick1, start, 0.0), axis=1, keepdims=True)
    d2 = jnp.sum(jnp.where(pick2, start, 0.0), axis=1, keepdims=True)
    cnt_ref[...] = cnt

    packed = jnp.zeros((tm, LANES), F32)
    for k, val in enumerate((d1, d2, w1, w2)):
        packed = jnp.where(lane == k, val, packed)
    rt_ref[0] = packed
    meta = packed.T[0:SUBLANES, :]
    meta_ref[0] = meta
    pos_v[...] = meta.astype(jnp.int32)
    to_smem = pltpu.make_async_copy(pos_v, pos_s, sem)
    to_smem.start()
    to_smem.wait()

    def place(t8, _):
        for u in range(SUBLANES):
            t = t8 * SUBLANES + u
            row = h2_s[pl.ds(pl.multiple_of(t * SUBLANES, SUBLANES), SUBLANES), :]
            for k in range(TOP_K):
                xs_ref[pl.ds(pl.multiple_of(pos_s[k, t] * SUBLANES, SUBLANES), SUBLANES), :] = row
        return 0

    lax.fori_loop(0, tm // SUBLANES, place, 0)


def _out_and_route(of, od, x, gate1, scale2, shift2, w_out, ln_g, ln_b, w_rg, b_rg, w_re, b_re, tm):
    B, S, D = x.shape
    n_r = N_GROUPS + N_EXPERTS
    w_r = jnp.pad(jnp.concatenate([w_rg, w_re], axis=1), ((0, 0), (0, LANES - n_r)))
    b_r = jnp.pad(jnp.concatenate([b_rg, b_re]), (0, LANES - n_r)).reshape(1, LANES)
    w_r_hi = w_r.astype(BF16)
    w_r_lo = (w_r - w_r_hi.astype(F32)).astype(BF16)
    nt = S // tm
    tile = lambda w: pl.BlockSpec((1, tm, w), lambda b, i: (b, i, 0))
    per_batch = pl.BlockSpec((1, 1, D), lambda b, i: (b, 0, 0))
    const = lambda shape: pl.BlockSpec(shape, lambda b, i: (0,) * len(shape))
    return pl.pallas_call(
        _out_kernel,
        grid=(B, S // tm),
        in_specs=[tile(FOX_W), tile(DIL_W), tile(D), per_batch, per_batch, per_batch,
                  const((D, D)), const((1, D)), const((1, D)), const((D, LANES)), const((D, LANES)),
                  const((1, LANES))],
        out_specs=[tile(D),
                   pl.BlockSpec((TOP_K * tm * SUBLANES, LANES), lambda b, i: (b * nt + i, 0)),
                   pl.BlockSpec((1, SUBLANES, tm), lambda b, i: (b * nt + i, 0, 0)),
                   tile(LANES),
                   pl.BlockSpec((SUBLANES, LANES), lambda b, i: (b * nt + i, 0))],
        out_shape=[jax.ShapeDtypeStruct((B, S, D), F32),
                   jax.ShapeDtypeStruct((TOP_K * B * S * SUBLANES, LANES), F32),
                   jax.ShapeDtypeStruct((B * nt, SUBLANES, tm), F32),
                   jax.ShapeDtypeStruct((B, S, LANES), F32),
                   jax.ShapeDtypeStruct((B * nt * SUBLANES, LANES), F32)],
        scratch_shapes=[pltpu.VMEM((tm * SUBLANES, LANES), F32), pltpu.VMEM((SUBLANES, tm), jnp.int32),
                        pltpu.SMEM((SUBLANES, tm), jnp.int32), pltpu.SemaphoreType.DMA(())],
        compiler_params=_params("arbitrary", "arbitrary"),
        name="out",
    )(of, od, x, gate1.reshape(B, 1, D), scale2.reshape(B, 1, D), shift2.reshape(B, 1, D),
      w_out.astype(BF16), ln_g.reshape(1, D), ln_b.reshape(1, D), w_r_hi, w_r_lo, b_r)


RUN_BITS = MOE_BLOCK.bit_length()
MAX_RUNS = 80
ST_EXPERT, ST_TILE, ST_OFF, ST_DONE = range(4)


def _copy_rows(src_ref, src_row, dst_ref, dst_row, n, sem, wait):
    pos = 0
    for bit in reversed(range(RUN_BITS)):
        size = 1 << bit

        @pl.when((n & size) != 0)
        def _(pos=pos, size=size):
            cp = pltpu.make_async_copy(
                src_ref.at[pl.ds(pl.multiple_of((src_row + pos) * SUBLANES, SUBLANES), size * SUBLANES), :],
                dst_ref.at[pl.ds(pl.multiple_of((dst_row + pos) * SUBLANES, SUBLANES), size * SUBLANES), :], sem)
            if wait:
                cp.wait()
            else:
                cp.start()

        pos = pos + (n & size)


def _moe_kernel(blk_e_ref, nused_ref, cnt_ref, off_ref, tot_ref, x_ref, wg_ref, wu_ref, wd_ref, y_ref,
                xbuf0, xbuf1, ybuf0, ybuf1, mid_s, st, runs, sruns, gsem, ssem):
    del blk_e_ref
    i = pl.program_id(0)
    nused = nused_ref[0]
    D = xbuf0.shape[0] // MOE_BLOCK * LANES
    nk = D // MXU_DIM
    n_tiles = cnt_ref.shape[0] // N_EXPERTS
    rows_per_tile = x_ref.shape[0] // SUBLANES // n_tiles

    def gather_next(xdst, s):
        def next_expert(c):
            return c[0] + 1, 0, 0, 0

        def exhausted(c):
            return (c[0] < N_EXPERTS) & (c[3] >= tot_ref[jnp.minimum(c[0], N_EXPERTS - 1)])

        e, tile, off, done = lax.while_loop(
            exhausted, next_expert, (st[ST_EXPERT], st[ST_TILE], st[ST_OFF], st[ST_DONE]))
        ec = jnp.minimum(e, N_EXPERTS - 1)
        total = jnp.where(e < N_EXPERTS, tot_ref[ec], 0)

        def more(c):
            return (c[0] < MOE_BLOCK) & (c[4] < total)

        def take_run(c):
            filled, k, tile, off, done = c
            run = cnt_ref[tile * N_EXPERTS + ec]
            take = jnp.minimum(run - off, MOE_BLOCK - filled)
            src = tile * rows_per_tile + off_ref[tile * N_EXPERTS + ec] + off
            _copy_rows(x_ref, src, xdst, filled, take, gsem.at[s], wait=False)
            base = (s * MAX_RUNS + k) * 3
            runs[base] = src
            runs[base + 1] = filled
            runs[base + 2] = take
            run_done = off + take >= run
            return (filled + take, k + jnp.where(take > 0, 1, 0), jnp.where(run_done, tile + 1, tile),
                    jnp.where(run_done, 0, off + take), done + take)

        filled, k, tile, off, done = lax.while_loop(more, take_run, (0, 0, tile, off, done))
        _copy_rows(x_ref, 0, xdst, filled, jnp.where(total > 0, MOE_BLOCK - filled, 0), gsem.at[s], wait=False)
        runs[(2 * MAX_RUNS + s) * 3] = k
        st[ST_EXPERT] = e
        st[ST_TILE] = tile
        st[ST_OFF] = off
        st[ST_DONE] = done

    def wait_gather(xdst, s):
        pltpu.make_async_copy(x_ref.at[pl.ds(0, MOE_BLOCK * SUBLANES), :], xdst, gsem.at[s]).wait()

    def scatter_block(ysrc, s):
        def one(k, rows):
            base = (s * MAX_RUNS + k) * 3
            _copy_rows(ysrc, runs[base + 1], y_ref, runs[base], runs[base + 2], ssem.at[s], wait=False)
            return rows + runs[base + 2]

        sruns[s] = lax.fori_loop(0, runs[(2 * MAX_RUNS + s) * 3], one, 0)

    def wait_scatter(ysrc, s):
        n = pl.multiple_of(sruns[s] * SUBLANES, SUBLANES)
        pltpu.make_async_copy(ysrc.at[pl.ds(0, n), :], y_ref.at[pl.ds(0, n), :], ssem.at[s]).wait()

    def compute(xcur, ycur):
        xk = [jnp.concatenate([xcur[pl.ds(2 * j, MOE_BLOCK, stride=SUBLANES), :],
                               xcur[pl.ds(2 * j + 1, MOE_BLOCK, stride=SUBLANES), :]], axis=1).astype(BF16)
              for j in range(nk)]
        for c in range(D_EXPERT // MXU_DIM):
            cols = slice(c * MXU_DIM, (c + 1) * MXU_DIM)
            g = sum(jnp.dot(xk[j], wg_ref[0, MXU_DIM * j:MXU_DIM * (j + 1), cols].astype(BF16),
                            preferred_element_type=F32) for j in range(nk))
            u = sum(jnp.dot(xk[j], wu_ref[0, MXU_DIM * j:MXU_DIM * (j + 1), cols].astype(BF16),
                            preferred_element_type=F32) for j in range(nk))
            mid_s[:, cols] = (g * _sigmoid(g) * u).astype(BF16)
        mid = mid_s[...]
        for c in range(D // MXU_DIM):
            y = jnp.dot(mid, wd_ref[0, :, c * MXU_DIM:(c + 1) * MXU_DIM].astype(BF16), preferred_element_type=F32)
            ycur[pl.ds(2 * c, MOE_BLOCK, stride=SUBLANES), :] = y[:, 0:LANES]
            ycur[pl.ds(2 * c + 1, MOE_BLOCK, stride=SUBLANES), :] = y[:, LANES:2 * LANES]

    def step(s, xcur, xnxt, ycur, yprv):
        o = 1 - s
        wait_gather(xcur, s)

        @pl.when(i + 1 < nused)
        def _():
            gather_next(xnxt, o)

        @pl.when(i >= 2)
        def _():
            wait_scatter(ycur, s)

        compute(xcur, ycur)
        scatter_block(ycur, s)

        @pl.when(i + 1 == nused)
        def _():
            @pl.when(i >= 1)
            def _():
                wait_scatter(yprv, o)

            wait_scatter(ycur, s)

    @pl.when(i == 0)
    def _():
        for f in range(4):
            st[f] = 0
        gather_next(xbuf0, 0)

    @pl.when((i < nused) & (i % 2 == 0))
    def _():
        step(0, xbuf0, xbuf1, ybuf0, ybuf1)

    @pl.when((i < nused) & (i % 2 == 1))
    def _():
        step(1, xbuf1, xbuf0, ybuf1, ybuf0)


def _expert_mlp(x_sorted, cnt, off, tot, blk_e, nused, w_gate, w_up, w_down):
    D = w_gate.shape[1]
    nblk = blk_e.shape[0]
    assert cnt.shape[0] // N_EXPERTS + 2 <= MAX_RUNS
    wmap = lambda i, be, nu, c, o, t: (be[i], 0, 0)
    return pl.pallas_call(
        _moe_kernel,
        grid_spec=pltpu.PrefetchScalarGridSpec(
            num_scalar_prefetch=5,
            grid=(nblk,),
            in_specs=[pl.BlockSpec(memory_space=pl.ANY),
                      pl.BlockSpec((1, D, D_EXPERT), wmap),
                      pl.BlockSpec((1, D, D_EXPERT), wmap),
                      pl.BlockSpec((1, D_EXPERT, D), wmap)],
            out_specs=pl.BlockSpec(memory_space=pl.ANY),
            scratch_shapes=[pltpu.VMEM((MOE_BLOCK * SUBLANES, LANES), F32)] * 4
            + [pltpu.VMEM((MOE_BLOCK, D_EXPERT), BF16),
               pltpu.SMEM((4,), jnp.int32),
               pltpu.SMEM(((2 * MAX_RUNS + 2) * 3,), jnp.int32), pltpu.SMEM((2,), jnp.int32),
               pltpu.SemaphoreType.DMA((2,)), pltpu.SemaphoreType.DMA((2,))]),
        out_shape=jax.ShapeDtypeStruct(x_sorted.shape, F32),
        compiler_params=_params("arbitrary"),
        name="moe",
    )(blk_e, nused, cnt, off, tot, x_sorted, w_gate, w_up, w_down)


def _final_kernel(pos_ref, rt_ref, y_ref, x1_ref, g2_ref, lng_ref, lnb_ref, o_ref, pick_s):
    tm = x1_ref.shape[1]

    def unsort(t8, _):
        for u in range(SUBLANES):
            t = t8 * SUBLANES + u
            for k in range(TOP_K):
                pick_s[k, pl.ds(pl.multiple_of(t * SUBLANES, SUBLANES), SUBLANES), :] = (
                    y_ref[pl.ds(pl.multiple_of(pos_ref[k, t] * SUBLANES, SUBLANES), SUBLANES), :])
        return 0

    lax.fori_loop(0, tm // SUBLANES, unsort, 0)
    wts = rt_ref[0]
    y = jnp.concatenate([wts[:, TOP_K:TOP_K + 1] * pick_s[0, pl.ds(s, tm, stride=SUBLANES), :]
                         + wts[:, TOP_K + 1:TOP_K + 2] * pick_s[1, pl.ds(s, tm, stride=SUBLANES), :]
                         for s in range(SUBLANES)], axis=1)
    o_ref[0] = _layer_norm(DEEPNORM_ALPHA * x1_ref[0] + g2_ref[0] * y, lng_ref[...], lnb_ref[...])


def _combine_and_norm(y_sorted, meta, route, x1, gate2, ln_g, ln_b, tm):
    B, S, D = x1.shape
    nt = S // tm
    pos = meta.astype(jnp.int32).reshape(B * nt * SUBLANES, tm)
    return pl.pallas_call(
        _final_kernel,
        grid=(B, nt),
        in_specs=[pl.BlockSpec((SUBLANES, tm), lambda b, i: (b * nt + i, 0), memory_space=pltpu.SMEM),
                  pl.BlockSpec((1, tm, LANES), lambda b, i: (b, i, 0)),
                  pl.BlockSpec((TOP_K * tm * SUBLANES, LANES), lambda b, i: (b * nt + i, 0)),
                  pl.BlockSpec((1, tm, D), lambda b, i: (b, i, 0)),
                  pl.BlockSpec((1, 1, D), lambda b, i: (b, 0, 0)),
                  pl.BlockSpec((1, D), lambda b, i: (0, 0)),
                  pl.BlockSpec((1, D), lambda b, i: (0, 0))],
        out_specs=pl.BlockSpec((1, tm, D), lambda b, i: (b, i, 0)),
        out_shape=jax.ShapeDtypeStruct((B, S, D), F32),
        scratch_shapes=[pltpu.VMEM((TOP_K, tm * SUBLANES, LANES), F32)],
        compiler_params=_params("arbitrary", "arbitrary"),
        name="final",
    )(pos, route, y_sorted, x1, gate2.reshape(B, 1, D), ln_g.reshape(1, D), ln_b.reshape(1, D))


def kernel(x, c, positions, w_ada, b_ada, w_in, b_forget, w_out, ln1_g, ln1_b, w_router_group, b_router_group,
           w_router_expert, b_router_expert, w_up, w_gate, w_down, ln2_g, ln2_b):
    B, S, D = x.shape
    N = B * S
    assert D == FOX_W + DIL_W and S % 2048 == 0
    mod = _modulation(c, w_ada, b_ada)
    shift1, scale1, gate1, shift2, scale2, gate2 = jnp.split(mod, 6, axis=-1)

    pf, vt, pd, log_f = _projection(x, positions, scale1, shift1, w_in, b_forget, tm=512)
    cum, cum_t = _cumulative_gate(log_f)
    nhp = FOX_W // LANES
    ck = cum[..., :FOX_HEADS].reshape(B, S, nhp, HEADS_PER_VREG).transpose(0, 2, 1, 3)
    cq = cum_t.reshape(B, nhp, HEADS_PER_VREG, S)
    of = _fox_attention(pf, vt, ck, cq)
    od = _dilated_attention(pd)

    tm = 512
    x1, x_sorted, meta, route, tile_cnt = _out_and_route(
        of, od, x, gate1, scale2, shift2, w_out, ln1_g, ln1_b,
        w_router_group, b_router_group, w_router_expert, b_router_expert, tm=tm)

    cnt = tile_cnt[::SUBLANES, N_GROUPS:N_GROUPS + N_EXPERTS].astype(jnp.int32)
    off = jnp.cumsum(cnt, axis=1) - cnt
    tot = jnp.sum(cnt, axis=0)
    pend = jnp.cumsum((tot + MOE_BLOCK - 1) // MOE_BLOCK * MOE_BLOCK)
    nblk = (N * TOP_K) // MOE_BLOCK + N_EXPERTS
    nused = pend[-1:] // MOE_BLOCK
    blk_start = jnp.arange(nblk, dtype=jnp.int32) * MOE_BLOCK
    blk_e = jnp.sum((pend[None, :] <= jnp.minimum(blk_start, pend[-1] - 1)[:, None]).astype(jnp.int32), axis=1)

    y_sorted = _expert_mlp(x_sorted, cnt.reshape(-1), off.reshape(-1), tot, blk_e, nused, w_gate, w_up, w_down)
    return _combine_and_norm(y_sorted, meta, route, x1, gate2, ln2_g, ln2_b, tm=tm)
```

```python
import functools

import jax
import jax.numpy as jnp
import numpy as np
from jax import lax
from jax.experimental import pallas as pl
from jax.experimental.pallas import tpu as pltpu

HEAD_DIM = 64
FOX_HEADS = 8
DIL_HEADS = 8
FOX_W = FOX_HEADS * HEAD_DIM
DIL_W = DIL_HEADS * HEAD_DIM
DILATED_PATTERNS = ((128, 1), (512, 4), (2048, 16))
ROPE_THETA = 500000.0
ROT_DIM = HEAD_DIM // 4
N_GROUPS = 4
EXPERTS_PER_GROUP = 8
N_EXPERTS = N_GROUPS * EXPERTS_PER_GROUP
TOP_K = 2
D_EXPERT = 512
MOE_BLOCK = 256
LN_EPS = 1e-5
NEG = -1e30
DEPTH = 1
DEEPNORM_ALPHA = (2 * DEPTH) ** 0.25
QK_SCALE = HEAD_DIM ** -0.5
LOG2E = 1.4426950408889634

LANES = 128
SUBLANES = 8
BF16_SUBLANES = 16
MXU_DIM = 256
HEADS_PER_VREG = LANES // HEAD_DIM
V_ROWS = HEAD_DIM + BF16_SUBLANES
FOX_TILE = 512
DIL_UNROLL = 32
VMEM_LIMIT = 56 * 1024 * 1024

F32 = jnp.float32
BF16 = jnp.bfloat16
HIGHEST = lax.Precision.HIGHEST
NT_DIMS = (((1,), (1,)), ((), ()))


def _params(*sem):
    return pltpu.CompilerParams(dimension_semantics=sem, vmem_limit_bytes=VMEM_LIMIT)


def _sigmoid(v):
    return 1.0 / (1.0 + jnp.exp(-v))


def _layer_norm(v, g, b):
    mu = jnp.mean(v, axis=-1, keepdims=True)
    d = v - mu
    var = jnp.mean(d * d, axis=-1, keepdims=True)
    return d * lax.rsqrt(var + LN_EPS) * g + b


def _mod_kernel(c_ref, w_ref, b_ref, o_ref):
    c = c_ref[...]
    o_ref[...] = jnp.dot(c * _sigmoid(c), w_ref[...], precision=HIGHEST,
                         preferred_element_type=F32) + b_ref[...]


def _modulation(c, w_ada, b_ada):
    B, D = c.shape
    cols = w_ada.shape[1]
    tn = 1024
    return pl.pallas_call(
        _mod_kernel,
        grid=(cols // tn,),
        in_specs=[pl.BlockSpec((B, D), lambda j: (0, 0)),
                  pl.BlockSpec((D, tn), lambda j: (0, j)),
                  pl.BlockSpec((1, tn), lambda j: (0, j))],
        out_specs=pl.BlockSpec((B, tn), lambda j: (0, j)),
        out_shape=jax.ShapeDtypeStruct((B, cols), F32),
        compiler_params=_params("arbitrary"),
        name="mod",
    )(c, w_ada, b_ada.reshape(1, cols))


def _proj_kernel(x_ref, pos_ref, sc_ref, sh_ref, wf_ref, wvt_ref, one_ref, wd_ref, wff_ref, bf_ref, invf_ref,
                 sgn_ref, pf_ref, vt_ref, pd_ref, lf_ref):
    tm = x_ref.shape[1]
    h = (x_ref[0] * (1.0 + sc_ref[0]) + sh_ref[0]).astype(BF16)

    for ci in range(3):
        acc = jnp.dot(h, wf_ref[:, ci * FOX_W:(ci + 1) * FOX_W], preferred_element_type=F32)
        if ci == 0:
            acc = acc * (QK_SCALE * LOG2E)
        pf_ref[0, :, ci * FOX_W:(ci + 1) * FOX_W] = acc.astype(BF16)
    vt = lax.dot_general(wvt_ref[...], h, NT_DIMS, preferred_element_type=F32) + one_ref[...]
    vt_ref[0] = vt.astype(BF16)

    z = jnp.dot(h, wff_ref[...], preferred_element_type=F32) + bf_ref[...]
    lf_ref[0] = (jnp.minimum(z, 0.0) - jnp.log1p(jnp.exp(-jnp.abs(z)))) * LOG2E

    ang = pos_ref[0].astype(F32) * invf_ref[...]
    cs = jnp.cos(ang)
    sn = jnp.sin(ang) * sgn_ref[...]
    lane = lax.broadcasted_iota(jnp.int32, (tm, LANES), 1)
    first_half = (lane % HEAD_DIM) < (ROT_DIM // 2)
    for ci in range(3):
        acc = jnp.dot(h, wd_ref[:, ci * DIL_W:(ci + 1) * DIL_W], preferred_element_type=F32)
        if ci == 2:
            pd_ref[0, :, ci * DIL_W:(ci + 1) * DIL_W] = acc
            continue
        for j in range(DIL_W // LANES):
            t = acc[:, j * LANES:(j + 1) * LANES]
            partner = jnp.where(first_half, pltpu.roll(t, LANES - ROT_DIM // 2, 1),
                                pltpu.roll(t, ROT_DIM // 2, 1))
            r = t * cs + partner * sn
            if ci == 0:
                r = r * (QK_SCALE * LOG2E)
            pd_ref[0, :, ci * DIL_W + j * LANES:ci * DIL_W + (j + 1) * LANES] = r


def _projection(x, positions, scale1, shift1, w_in, b_forget, tm):
    B, S, D = x.shape
    o = np.cumsum((0, FOX_W, FOX_W, FOX_W, FOX_W, FOX_HEADS, DIL_W, DIL_W, DIL_W))
    w_fox = jnp.concatenate([w_in[:, o[0]:o[2]], w_in[:, o[3]:o[4]]], axis=1).astype(BF16)
    w_vt = jnp.pad(w_in[:, o[2]:o[3]].T.reshape(FOX_HEADS, HEAD_DIM, D),
                   ((0, 0), (0, V_ROWS - HEAD_DIM), (0, 0))).reshape(FOX_HEADS * V_ROWS, D).astype(BF16)
    ones_row = jnp.asarray((np.arange(FOX_HEADS * V_ROWS) % V_ROWS == HEAD_DIM).astype(np.float32)
                           ).reshape(FOX_HEADS * V_ROWS, 1)
    w_ff = jnp.pad(w_in[:, o[4]:o[5]], ((0, 0), (0, LANES - FOX_HEADS))).astype(BF16)
    w_dil = w_in[:, o[5]:o[8]].astype(BF16)
    b_f = jnp.pad(b_forget, (0, LANES - FOX_HEADS)).reshape(1, LANES)
    e = np.arange(LANES) % HEAD_DIM
    inv_freq = ROPE_THETA ** (-jnp.arange(0, ROT_DIM, 2, dtype=F32) / ROT_DIM)
    invf = jnp.where(e < ROT_DIM, jnp.tile(inv_freq, LANES // (ROT_DIM // 2)), 0.0).reshape(1, LANES)
    sgn = jnp.asarray(np.where(e < ROT_DIM // 2, -1.0, np.where(e < ROT_DIM, 1.0, 0.0)), F32).reshape(1, LANES)
    const = lambda shape: pl.BlockSpec(shape, lambda b, i: (0,) * len(shape))
    return pl.pallas_call(
        _proj_kernel,
        grid=(B, S // tm),
        in_specs=[pl.BlockSpec((1, tm, D), lambda b, i: (b, i, 0)),
                  pl.BlockSpec((1, tm, 1), lambda b, i: (b, i, 0)),
                  pl.BlockSpec((1, 1, D), lambda b, i: (b, 0, 0)),
                  pl.BlockSpec((1, 1, D), lambda b, i: (b, 0, 0)),
                  const((D, 3 * FOX_W)), const((FOX_HEADS * V_ROWS, D)), const((FOX_HEADS * V_ROWS, 1)),
                  const((D, 3 * DIL_W)), const((D, LANES)),
                  const((1, LANES)), const((1, LANES)), const((1, LANES))],
        out_specs=[pl.BlockSpec((1, tm, 3 * FOX_W), lambda b, i: (b, i, 0)),
                   pl.BlockSpec((1, FOX_HEADS * V_ROWS, tm), lambda b, i: (b, 0, i)),
                   pl.BlockSpec((1, tm, 3 * DIL_W), lambda b, i: (b, i, 0)),
                   pl.BlockSpec((1, tm, LANES), lambda b, i: (b, i, 0))],
        out_shape=[jax.ShapeDtypeStruct((B, S, 3 * FOX_W), BF16),
                   jax.ShapeDtypeStruct((B, FOX_HEADS * V_ROWS, S), BF16),
                   jax.ShapeDtypeStruct((B, S, 3 * DIL_W), F32),
                   jax.ShapeDtypeStruct((B, S, LANES), F32)],
        compiler_params=_params("arbitrary", "arbitrary"),
        name="proj",
    )(x, positions.reshape(B, S, 1), scale1.reshape(B, 1, D), shift1.reshape(B, 1, D),
      w_fox, w_vt, ones_row, w_dil, w_ff, b_f, invf, sgn)


def _cum_kernel(lf_ref, c_ref, ct_ref):
    S = lf_ref.shape[1]
    r = lax.broadcasted_iota(jnp.int32, (LANES, LANES), 0)
    c = lax.broadcasted_iota(jnp.int32, (LANES, LANES), 1)
    tri = (c <= r).astype(F32)

    carry = jnp.zeros((1, LANES), F32)
    for j in range(S // LANES):
        rows = slice(j * LANES, (j + 1) * LANES)
        cum = jnp.dot(tri, lf_ref[0, rows, :], precision=HIGHEST, preferred_element_type=F32) + carry
        c_ref[0, rows, :] = cum
        ct_ref[0, :, rows] = cum.T[0:FOX_HEADS, :]
        carry = cum[LANES - 1:LANES, :]


def _cumulative_gate(log_f):
    B, S, _ = log_f.shape
    return pl.pallas_call(
        _cum_kernel,
        grid=(B,),
        in_specs=[pl.BlockSpec((1, S, LANES), lambda b: (b, 0, 0))],
        out_specs=[pl.BlockSpec((1, S, LANES), lambda b: (b, 0, 0)),
                   pl.BlockSpec((1, FOX_HEADS, S), lambda b: (b, 0, 0))],
        out_shape=[jax.ShapeDtypeStruct((B, S, LANES), F32),
                   jax.ShapeDtypeStruct((B, FOX_HEADS, S), F32)],
        compiler_params=_params("arbitrary"),
        name="cum",
    )(log_f)


def _fox_kernel(q_ref, k_ref, vt_ref, g_ref, ck_ref, cq_ref, o_ref, s_scr):
    S = q_ref.shape[1]
    T = FOX_TILE
    lane = lax.broadcasted_iota(jnp.int32, (T, LANES), 1)

    def q_block(i, _):
        qoff = pl.multiple_of(i * T, T)
        q = q_ref[0, pl.ds(qoff, T), :]
        qh = [jnp.where((lane // HEAD_DIM) == hh, q, jnp.zeros_like(q)) for hh in range(HEADS_PER_VREG)]
        cq = [cq_ref[0, 0, hh:hh + 1, pl.ds(qoff, T)] for hh in range(HEADS_PER_VREG)]

        def scores_into(buf, b):
            kb = k_ref[0, pl.ds(pl.multiple_of(b * T, T), T), :]
            for hh in range(HEADS_PER_VREG):
                s_scr[buf, hh] = lax.dot_general(kb, qh[hh], NT_DIMS, preferred_element_type=F32)

        def attend(buf, b, carry, masked):
            koff = pl.multiple_of(b * T, T)
            new = []
            for hh in range(HEADS_PER_VREG):
                m, acc = carry[hh]

                def scores(c0, rows):
                    s = s_scr[buf, hh, c0:c0 + rows, :] - ck_ref[0, 0, pl.ds(koff + c0, rows), hh:hh + 1]
                    if masked:
                        s = jnp.where(lax.broadcasted_iota(jnp.int32, (rows, T), 0) + c0
                                      <= lax.broadcasted_iota(jnp.int32, (rows, T), 1), s, NEG)
                    return s

                m_new = jnp.maximum(m, cq[hh] + jnp.max(scores(0, T), axis=0, keepdims=True))
                row = cq[hh] - m_new
                acc = jnp.exp2(m - m_new) * acc
                for c0 in range(0, T, MXU_DIM):
                    p = jnp.exp2((scores(c0, MXU_DIM) + row).astype(BF16))
                    vt = vt_ref[0, hh * V_ROWS:(hh + 1) * V_ROWS, pl.ds(koff + c0, MXU_DIM)]
                    acc = acc + jnp.dot(vt, p, preferred_element_type=F32)
                new.append((m_new, acc))
            return tuple(new)

        def pair(t, carry):
            scores_into(1, 2 * t + 1)
            carry = attend(0, 2 * t, carry, False)
            scores_into(0, 2 * t + 2)
            return attend(1, 2 * t + 1, carry, False)

        def odd_tail(carry):
            scores_into(1, i)
            return attend(1, i, attend(0, i - 1, carry, False), True)

        def even_tail(carry):
            return attend(0, i, carry, True)

        init = tuple((jnp.full((1, T), NEG, F32), jnp.zeros((V_ROWS, T), F32)) for _ in range(HEADS_PER_VREG))
        scores_into(0, 0)
        carry = lax.fori_loop(0, i // 2, pair, init)
        carry = lax.cond(i % 2 == 1, odd_tail, even_tail, carry)
        o_t = jnp.concatenate([acc[0:HEAD_DIM] * (1.0 / acc[HEAD_DIM:HEAD_DIM + 1]) for (_, acc) in carry], axis=0)
        gate = _sigmoid(g_ref[0, pl.ds(qoff, T), :].astype(F32))
        o_ref[0, pl.ds(qoff, T), :] = (o_t.T * gate).astype(BF16)
        return 0

    lax.fori_loop(0, S // T, q_block, 0)


def _fox_attention(pf, vt, ck, cq):
    B, S, _ = pf.shape
    nhp = FOX_W // LANES
    assert S % FOX_TILE == 0
    return pl.pallas_call(
        _fox_kernel,
        grid=(B, nhp),
        in_specs=[pl.BlockSpec((1, S, LANES), lambda b, h: (b, 0, h)),
                  pl.BlockSpec((1, S, LANES), lambda b, h: (b, 0, nhp + h)),
                  pl.BlockSpec((1, HEADS_PER_VREG * V_ROWS, S), lambda b, h: (b, h, 0)),
                  pl.BlockSpec((1, S, LANES), lambda b, h: (b, 0, 2 * nhp + h)),
                  pl.BlockSpec((1, 1, S, HEADS_PER_VREG), lambda b, h: (b, h, 0, 0)),
                  pl.BlockSpec((1, 1, HEADS_PER_VREG, S), lambda b, h: (b, h, 0, 0))],
        out_specs=pl.BlockSpec((1, S, LANES), lambda b, h: (b, 0, h)),
        out_shape=jax.ShapeDtypeStruct((B, S, FOX_W), BF16),
        scratch_shapes=[pltpu.VMEM((2, HEADS_PER_VREG, FOX_TILE, FOX_TILE), F32)],
        compiler_params=_params("arbitrary", "arbitrary"),
        name="fox",
    )(pf, pf, vt, pf, ck, cq)


def _dil_kernel(q_ref, k_ref, v_ref, o_ref, m_s, l_s, a_s, q4, k4, v4, m4, l4, a4):
    S = q_ref.shape[1]
    W = LANES
    lane = lax.broadcasted_iota(jnp.int32, (W, LANES), 1)
    head0 = lane < HEAD_DIM
    ri = lax.broadcasted_iota(jnp.int32, (2 * W, 2 * W), 0) % W
    ci = lax.broadcasted_iota(jnp.int32, (2 * W, 2 * W), 1)
    bias_rest = jnp.where((ci >= ri) & (ci <= ri + W), 0.0, NEG)
    bias_first = jnp.where((ci < W) & (ci <= ri), 0.0, NEG)
    ones = jnp.ones((2 * W, LANES), BF16)

    assert DILATED_PATTERNS == ((W, 1), (4 * W, 4), (16 * W, 16)) and S % (16 * W * 2) == 0
    assert (S // W) % DIL_UNROLL == 0
    L4 = S // 4

    def block_stats(q, kb, vb, first):
        qs = jnp.concatenate([jnp.where(head0, q, 0.0), jnp.where(head0, 0.0, q)], axis=0).astype(BF16)
        vb2 = jnp.concatenate([vb.astype(BF16), ones], axis=1)
        s = lax.dot_general(qs, kb.astype(BF16), NT_DIMS, preferred_element_type=F32)
        s = s + jnp.where(first, bias_first, bias_rest)
        m2 = jnp.max(s, axis=1, keepdims=True)
        out = jnp.dot(jnp.exp2((s - m2).astype(BF16)), vb2, preferred_element_type=F32)
        a_u = jnp.where(head0, out[0:W, 0:LANES], out[W:2 * W, 0:LANES])
        l_u = jnp.where(head0, out[0:W, LANES:2 * LANES], out[W:2 * W, LANES:2 * LANES])
        m_u = jnp.where(head0, m2[0:W], m2[W:2 * W])
        return m_u, l_u, a_u

    def merge(idx, stats):
        m_u, l_u, a_u = stats
        m_o = m4[idx, :]
        m_n = jnp.maximum(m_o, m_u)
        e_o = jnp.exp2(m_o - m_n)
        e_u = jnp.exp2(m_u - m_n)
        m4[idx, :] = m_n
        l4[idx, :] = l4[idx, :] * e_o + l_u * e_u
        a4[idx, :] = a4[idx, :] * e_o + a_u * e_u

    def for_blocks(unit):
        def group(t, _):
            for uu in range(DIL_UNROLL):
                unit(t * DIL_UNROLL + uu)
            return 0
        lax.fori_loop(0, S // W // DIL_UNROLL, group, 0)

    def unit1(u):
        qidx = pl.ds(pl.multiple_of(u * W, W), W)
        kidx = pl.ds(pl.multiple_of(jnp.maximum(u - 1, 0) * W, W), 2 * W)
        m_s[qidx, :], l_s[qidx, :], a_s[qidx, :] = block_stats(q_ref[0, qidx, :], k_ref[0, kidx, :],
                                                               v_ref[0, kidx, :], u == 0)
    for_blocks(unit1)

    def reorder(c, _):
        for r in range(4):
            src = pl.ds(c * (4 * W) + r, W, stride=4)
            dst = pl.ds(pl.multiple_of(r * L4 + c * W, W), W)
            q4[dst, :] = q_ref[0, src, :]
            k4[dst, :] = k_ref[0, src, :]
            v4[dst, :] = v_ref[0, src, :]
            m4[dst, :] = m_s[src, :]
            l4[dst, :] = l_s[src, :]
            a4[dst, :] = a_s[src, :]
        return 0
    lax.fori_loop(0, L4 // W, reorder, 0)

    nb4 = L4 // W

    def unit4(u):
        n = u % nb4
        base = (u // nb4) * L4
        qidx = pl.ds(pl.multiple_of(base + n * W, W), W)
        kidx = pl.ds(pl.multiple_of(base + jnp.maximum(n - 1, 0) * W, W), 2 * W)
        merge(qidx, block_stats(q4[qidx, :], k4[kidx, :], v4[kidx, :], n == 0))
    for_blocks(unit4)

    nb16 = S // (16 * W)

    def unit16(u):
        n = u % nb16
        r16 = u // nb16
        base = (r16 % 4) * L4 + r16 // 4
        qidx = pl.ds(base + n * (4 * W), W, stride=4)
        kidx = pl.ds(base + jnp.maximum(n - 1, 0) * (4 * W), 2 * W, stride=4)
        merge(qidx, block_stats(q4[qidx, :], k4[kidx, :], v4[kidx, :], n == 0))
    for_blocks(unit16)

    def finish(c, _):
        for r in range(4):
            src = pl.ds(pl.multiple_of(r * L4 + c * W, W), W)
            a_s[pl.ds(c * (4 * W) + r, W, stride=4), :] = a4[src, :] * (1.0 / l4[src, :])
        return 0
    lax.fori_loop(0, L4 // W, finish, 0)
    o_ref[0] = a_s[...].astype(BF16)


def _dilated_attention(pd):
    B, S, _ = pd.shape
    nhp = DIL_W // LANES
    return pl.pallas_call(
        _dil_kernel,
        grid=(B, nhp),
        in_specs=[pl.BlockSpec((1, S, LANES), lambda b, h: (b, 0, h)),
                  pl.BlockSpec((1, S, LANES), lambda b, h: (b, 0, nhp + h)),
                  pl.BlockSpec((1, S, LANES), lambda b, h: (b, 0, 2 * nhp + h))],
        out_specs=pl.BlockSpec((1, S, LANES), lambda b, h: (b, 0, h)),
        out_shape=jax.ShapeDtypeStruct((B, S, DIL_W), BF16),
        scratch_shapes=[pltpu.VMEM((S, LANES), F32)] * 9,
        compiler_params=_params("arbitrary", "arbitrary"),
        name="dil",
    )(pd, pd, pd)


def _out_kernel(of_ref, od_ref, x_ref, g1_ref, sc2_ref, sh2_ref, wo_ref, lng_ref, lnb_ref, wrh_ref, wrl_ref, br_ref,
                x1_ref, xs_ref, meta_ref, rt_ref, cnt_ref, h2_s, pos_v, pos_s, sem):
    tm = x_ref.shape[1]
    y = (jnp.dot(of_ref[0], wo_ref[0:FOX_W, :], preferred_element_type=F32)
         + jnp.dot(od_ref[0], wo_ref[FOX_W:FOX_W + DIL_W, :], preferred_element_type=F32))
    x1 = _layer_norm(DEEPNORM_ALPHA * x_ref[0] + g1_ref[0] * y, lng_ref[...], lnb_ref[...])
    x1_ref[0] = x1
    h2 = x1 * (1.0 + sc2_ref[0]) + sh2_ref[0]
    for s in range(SUBLANES):
        h2_s[pl.ds(s, tm, stride=SUBLANES), :] = h2[:, s * LANES:(s + 1) * LANES]

    h_hi = h2.astype(BF16)
    h_lo = (h2 - h_hi.astype(F32)).astype(BF16)
    lg = (jnp.dot(h_hi, wrh_ref[...], preferred_element_type=F32)
          + jnp.dot(h_lo, wrh_ref[...], preferred_element_type=F32)
          + jnp.dot(h_hi, wrl_ref[...], preferred_element_type=F32)) + br_ref[...]
    lane = lax.broadcasted_iota(jnp.int32, (tm, LANES), 1)
    lanef = lane.astype(F32)
    far = float(LANES)
    is_g = lane < N_GROUPS
    gl = jnp.where(is_g, lg, NEG)
    gmax = jnp.max(gl, axis=1, keepdims=True)
    gidx = jnp.min(jnp.where(gl == gmax, lanef, far), axis=1, keepdims=True)
    p_g = 1.0 / jnp.sum(jnp.where(is_g, jnp.exp(gl - gmax), 0.0), axis=1, keepdims=True)
    e_lo = N_GROUPS + gidx * EXPERTS_PER_GROUP
    in_group = (lanef >= e_lo) & (lanef < e_lo + EXPERTS_PER_GROUP)
    el = jnp.where(in_group, lg, NEG)
    v1 = jnp.max(el, axis=1, keepdims=True)
    i1 = jnp.min(jnp.where(in_group & (el == v1), lanef, far), axis=1, keepdims=True)
    rest = in_group & (lanef != i1)
    el2 = jnp.where(rest, lg, NEG)
    v2 = jnp.max(el2, axis=1, keepdims=True)
    i2 = jnp.min(jnp.where(rest & (el2 == v2), lanef, far), axis=1, keepdims=True)
    e21 = jnp.exp(v2 - v1)
    w1 = p_g / (1.0 + e21)
    w2 = p_g * e21 / (1.0 + e21)

    pick1 = lanef == i1
    pick2 = lanef == i2
    onehot = jnp.where(pick1 | pick2, 1.0, 0.0)
    rr = lax.broadcasted_iota(jnp.int32, (tm, tm), 0)
    cc = lax.broadcasted_iota(jnp.int32, (tm, tm), 1)
    strict_lower = jnp.where(cc < rr, 1.0, 0.0).astype(BF16)
    before = jnp.dot(strict_lower, onehot.astype(BF16), preferred_element_type=F32)
    cnt = jnp.broadcast_to(jnp.sum(onehot, axis=0, keepdims=True), (SUBLANES, LANES))
    lane8 = lax.broadcasted_iota(jnp.int32, (SUBLANES, LANES), 1)
    incl = cnt
    for sh in (1, 2, 4, 8, 16, 32, 64):
        incl = incl + jnp.where(lane8 >= sh, pltpu.roll(incl, sh, 1), 0.0)
    start = before + (incl - cnt)[0:1]
    d1 = jnp.sum(jnp.where(pick1, start, 0.0), axis=1, keepdims=True)
    d2 = jnp.sum(jnp.where(pick2, start, 0.0), axis=1, keepdims=True)
    cnt_ref[...] = cnt

    packed = jnp.zeros((tm, LANES), F32)
    for k, val in enumerate((d1, d2, w1, w2)):
        packed = jnp.where(lane == k, val, packed)
    rt_ref[0] = packed
    meta = packed.T[0:SUBLANES, :]
    meta_ref[0] = meta
    pos_v[...] = (meta * SUBLANES).astype(jnp.int32)
    to_smem = pltpu.make_async_copy(pos_v, pos_s, sem)
    to_smem.start()
    to_smem.wait()

    def place(t8, _):
        for u in range(SUBLANES):
            t = t8 * SUBLANES + u
            row = h2_s[pl.ds(pl.multiple_of(t * SUBLANES, SUBLANES), SUBLANES), :]
            for k in range(TOP_K):
                xs_ref[pl.ds(pl.multiple_of(pos_s[k, t], SUBLANES), SUBLANES), :] = row
        return 0

    lax.fori_loop(0, tm // SUBLANES, place, 0)


def _out_and_route(of, od, x, gate1, scale2, shift2, w_out, ln_g, ln_b, w_rg, b_rg, w_re, b_re, tm):
    B, S, D = x.shape
    n_r = N_GROUPS + N_EXPERTS
    w_r = jnp.pad(jnp.concatenate([w_rg, w_re], axis=1), ((0, 0), (0, LANES - n_r)))
    b_r = jnp.pad(jnp.concatenate([b_rg, b_re]), (0, LANES - n_r)).reshape(1, LANES)
    w_r_hi = w_r.astype(BF16)
    w_r_lo = (w_r - w_r_hi.astype(F32)).astype(BF16)
    nt = S // tm
    tile = lambda w: pl.BlockSpec((1, tm, w), lambda b, i: (b, i, 0))
    per_batch = pl.BlockSpec((1, 1, D), lambda b, i: (b, 0, 0))
    const = lambda shape: pl.BlockSpec(shape, lambda b, i: (0,) * len(shape))
    return pl.pallas_call(
        _out_kernel,
        grid=(B, S // tm),
        in_specs=[tile(FOX_W), tile(DIL_W), tile(D), per_batch, per_batch, per_batch,
                  const((D, D)), const((1, D)), const((1, D)), const((D, LANES)), const((D, LANES)),
                  const((1, LANES))],
        out_specs=[tile(D),
                   pl.BlockSpec((TOP_K * tm * SUBLANES, LANES), lambda b, i: (b * nt + i, 0)),
                   pl.BlockSpec((1, SUBLANES, tm), lambda b, i: (b * nt + i, 0, 0)),
                   tile(LANES),
                   pl.BlockSpec((SUBLANES, LANES), lambda b, i: (b * nt + i, 0))],
        out_shape=[jax.ShapeDtypeStruct((B, S, D), F32),
                   jax.ShapeDtypeStruct((TOP_K * B * S * SUBLANES, LANES), F32),
                   jax.ShapeDtypeStruct((B * nt, SUBLANES, tm), F32),
                   jax.ShapeDtypeStruct((B, S, LANES), F32),
                   jax.ShapeDtypeStruct((B * nt * SUBLANES, LANES), F32)],
        scratch_shapes=[pltpu.VMEM((tm * SUBLANES, LANES), F32), pltpu.VMEM((SUBLANES, tm), jnp.int32),
                        pltpu.SMEM((SUBLANES, tm), jnp.int32), pltpu.SemaphoreType.DMA(())],
        compiler_params=_params("arbitrary", "arbitrary"),
        name="out",
    )(of, od, x, gate1.reshape(B, 1, D), scale2.reshape(B, 1, D), shift2.reshape(B, 1, D),
      w_out.astype(BF16), ln_g.reshape(1, D), ln_b.reshape(1, D), w_r_hi, w_r_lo, b_r)


RUN_BITS = MOE_BLOCK.bit_length()
MAX_RUNS = 80
ST_EXPERT, ST_TILE, ST_OFF, ST_DONE = range(4)


def _copy_rows(src_ref, src_row, dst_ref, dst_row, n, sem, wait):
    pos = 0
    for bit in reversed(range(RUN_BITS)):
        size = 1 << bit

        @pl.when((n & size) != 0)
        def _(pos=pos, size=size):
            cp = pltpu.make_async_copy(
                src_ref.at[pl.ds(pl.multiple_of((src_row + pos) * SUBLANES, SUBLANES), size * SUBLANES), :],
                dst_ref.at[pl.ds(pl.multiple_of((dst_row + pos) * SUBLANES, SUBLANES), size * SUBLANES), :], sem)
            if wait:
                cp.wait()
            else:
                cp.start()

        pos = pos + (n & size)


def _moe_kernel(blk_e_ref, nused_ref, cnt_ref, off_ref, tot_ref, x_ref, wg_ref, wu_ref, wd_ref, y_ref,
                xbuf0, xbuf1, ybuf0, ybuf1, mid_s, st, runs, sruns, gsem, ssem):
    del blk_e_ref
    i = pl.program_id(0)
    nused = nused_ref[0]
    D = xbuf0.shape[0] // MOE_BLOCK * LANES
    nk = D // MXU_DIM
    n_tiles = cnt_ref.shape[0] // N_EXPERTS
    rows_per_tile = x_ref.shape[0] // SUBLANES // n_tiles

    def gather_next(xdst, s):
        def next_expert(c):
            return c[0] + 1, 0, 0, 0

        def exhausted(c):
            return (c[0] < N_EXPERTS) & (c[3] >= tot_ref[jnp.minimum(c[0], N_EXPERTS - 1)])

        e, tile, off, done = lax.while_loop(
            exhausted, next_expert, (st[ST_EXPERT], st[ST_TILE], st[ST_OFF], st[ST_DONE]))
        ec = jnp.minimum(e, N_EXPERTS - 1)
        total = jnp.where(e < N_EXPERTS, tot_ref[ec], 0)

        def more(c):
            return (c[0] < MOE_BLOCK) & (c[4] < total)

        def take_run(c):
            filled, k, tile, off, done = c
            run = cnt_ref[tile * N_EXPERTS + ec]
            take = jnp.minimum(run - off, MOE_BLOCK - filled)
            src = tile * rows_per_tile + off_ref[tile * N_EXPERTS + ec] + off
            _copy_rows(x_ref, src, xdst, filled, take, gsem.at[s], wait=False)
            base = (s * MAX_RUNS + k) * 3
            runs[base] = src
            runs[base + 1] = filled
            runs[base + 2] = take
            run_done = off + take >= run
            return (filled + take, k + jnp.where(take > 0, 1, 0), jnp.where(run_done, tile + 1, tile),
                    jnp.where(run_done, 0, off + take), done + take)

        filled, k, tile, off, done = lax.while_loop(more, take_run, (0, 0, tile, off, done))
        _copy_rows(x_ref, 0, xdst, filled, jnp.where(total > 0, MOE_BLOCK - filled, 0), gsem.at[s], wait=False)
        runs[(2 * MAX_RUNS + s) * 3] = k
        st[ST_EXPERT] = e
        st[ST_TILE] = tile
        st[ST_OFF] = off
        st[ST_DONE] = done

    def wait_gather(xdst, s):
        pltpu.make_async_copy(x_ref.at[pl.ds(0, MOE_BLOCK * SUBLANES), :], xdst, gsem.at[s]).wait()

    def scatter_block(ysrc, s):
        def one(k, rows):
            base = (s * MAX_RUNS + k) * 3
            _copy_rows(ysrc, runs[base + 1], y_ref, runs[base], runs[base + 2], ssem.at[s], wait=False)
            return rows + runs[base + 2]

        sruns[s] = lax.fori_loop(0, runs[(2 * MAX_RUNS + s) * 3], one, 0)

    def wait_scatter(ysrc, s):
        n = pl.multiple_of(sruns[s] * SUBLANES, SUBLANES)
        pltpu.make_async_copy(ysrc.at[pl.ds(0, n), :], y_ref.at[pl.ds(0, n), :], ssem.at[s]).wait()

    def compute(xcur, ycur):
        xk = [jnp.concatenate([xcur[pl.ds(2 * j, MOE_BLOCK, stride=SUBLANES), :],
                               xcur[pl.ds(2 * j + 1, MOE_BLOCK, stride=SUBLANES), :]], axis=1).astype(BF16)
              for j in range(nk)]
        for c in range(D_EXPERT // MXU_DIM):
            cols = slice(c * MXU_DIM, (c + 1) * MXU_DIM)
            g = sum(jnp.dot(xk[j], wg_ref[0, MXU_DIM * j:MXU_DIM * (j + 1), cols].astype(BF16),
                            preferred_element_type=F32) for j in range(nk))
            u = sum(jnp.dot(xk[j], wu_ref[0, MXU_DIM * j:MXU_DIM * (j + 1), cols].astype(BF16),
                            preferred_element_type=F32) for j in range(nk))
            mid_s[:, cols] = (g * _sigmoid(g) * u).astype(BF16)
        mid = mid_s[...]
        for c in range(D // MXU_DIM):
            y = jnp.dot(mid, wd_ref[0, :, c * MXU_DIM:(c + 1) * MXU_DIM].astype(BF16), preferred_element_type=F32)
            ycur[pl.ds(2 * c, MOE_BLOCK, stride=SUBLANES), :] = y[:, 0:LANES]
            ycur[pl.ds(2 * c + 1, MOE_BLOCK, stride=SUBLANES), :] = y[:, LANES:2 * LANES]

    def step(s, xcur, xnxt, ycur, yprv):
        o = 1 - s
        wait_gather(xcur, s)

        @pl.when(i + 1 < nused)
        def _():
            gather_next(xnxt, o)

        @pl.when(i >= 2)
        def _():
            wait_scatter(ycur, s)

        compute(xcur, ycur)
        scatter_block(ycur, s)

        @pl.when(i + 1 == nused)
        def _():
            @pl.when(i >= 1)
            def _():
                wait_scatter(yprv, o)

            wait_scatter(ycur, s)

    @pl.when(i == 0)
    def _():
        for f in range(4):
            st[f] = 0
        gather_next(xbuf0, 0)

    @pl.when((i < nused) & (i % 2 == 0))
    def _():
        step(0, xbuf0, xbuf1, ybuf0, ybuf1)

    @pl.when((i < nused) & (i % 2 == 1))
    def _():
        step(1, xbuf1, xbuf0, ybuf1, ybuf0)


def _expert_mlp(x_sorted, cnt, off, tot, blk_e, nused, w_gate, w_up, w_down):
    D = w_gate.shape[1]
    nblk = blk_e.shape[0]
    assert cnt.shape[0] // N_EXPERTS + 2 <= MAX_RUNS
    wmap = lambda i, be, nu, c, o, t: (be[i], 0, 0)
    return pl.pallas_call(
        _moe_kernel,
        grid_spec=pltpu.PrefetchScalarGridSpec(
            num_scalar_prefetch=5,
            grid=(nblk,),
            in_specs=[pl.BlockSpec(memory_space=pl.ANY),
                      pl.BlockSpec((1, D, D_EXPERT), wmap),
                      pl.BlockSpec((1, D, D_EXPERT), wmap),
                      pl.BlockSpec((1, D_EXPERT, D), wmap)],
            out_specs=pl.BlockSpec(memory_space=pl.ANY),
            scratch_shapes=[pltpu.VMEM((MOE_BLOCK * SUBLANES, LANES), F32)] * 4
            + [pltpu.VMEM((MOE_BLOCK, D_EXPERT), BF16),
               pltpu.SMEM((4,), jnp.int32),
               pltpu.SMEM(((2 * MAX_RUNS + 2) * 3,), jnp.int32), pltpu.SMEM((2,), jnp.int32),
               pltpu.SemaphoreType.DMA((2,)), pltpu.SemaphoreType.DMA((2,))]),
        out_shape=jax.ShapeDtypeStruct(x_sorted.shape, F32),
        compiler_params=_params("arbitrary"),
        name="moe",
    )(blk_e, nused, cnt, off, tot, x_sorted, w_gate, w_up, w_down)


def _final_kernel(pos_ref, rt_ref, y_ref, x1_ref, g2_ref, lng_ref, lnb_ref, o_ref, pick_s):
    tm = x1_ref.shape[1]

    def unsort(t8, _):
        for u in range(SUBLANES):
            t = t8 * SUBLANES + u
            for k in range(TOP_K):
                pick_s[k, pl.ds(pl.multiple_of(t * SUBLANES, SUBLANES), SUBLANES), :] = (
                    y_ref[pl.ds(pl.multiple_of(pos_ref[k, t], SUBLANES), SUBLANES), :])
        return 0

    lax.fori_loop(0, tm // SUBLANES, unsort, 0)
    wts = rt_ref[0]
    y = jnp.concatenate([wts[:, TOP_K:TOP_K + 1] * pick_s[0, pl.ds(s, tm, stride=SUBLANES), :]
                         + wts[:, TOP_K + 1:TOP_K + 2] * pick_s[1, pl.ds(s, tm, stride=SUBLANES), :]
                         for s in range(SUBLANES)], axis=1)
    o_ref[0] = _layer_norm(DEEPNORM_ALPHA * x1_ref[0] + g2_ref[0] * y, lng_ref[...], lnb_ref[...])


def _combine_and_norm(y_sorted, meta, route, x1, gate2, ln_g, ln_b, tm):
    B, S, D = x1.shape
    nt = S // tm
    pos = (meta * SUBLANES).astype(jnp.int32).reshape(B * nt * SUBLANES, tm)
    return pl.pallas_call(
        _final_kernel,
        grid=(B, nt),
        in_specs=[pl.BlockSpec((SUBLANES, tm), lambda b, i: (b * nt + i, 0), memory_space=pltpu.SMEM),
                  pl.BlockSpec((1, tm, LANES), lambda b, i: (b, i, 0)),
                  pl.BlockSpec((TOP_K * tm * SUBLANES, LANES), lambda b, i: (b * nt + i, 0)),
                  pl.BlockSpec((1, tm, D), lambda b, i: (b, i, 0)),
                  pl.BlockSpec((1, 1, D), lambda b, i: (b, 0, 0)),
                  pl.BlockSpec((1, D), lambda b, i: (0, 0)),
                  pl.BlockSpec((1, D), lambda b, i: (0, 0))],
        out_specs=pl.BlockSpec((1, tm, D), lambda b, i: (b, i, 0)),
        out_shape=jax.ShapeDtypeStruct((B, S, D), F32),
        scratch_shapes=[pltpu.VMEM((TOP_K, tm * SUBLANES, LANES), F32)],
        compiler_params=_params("arbitrary", "arbitrary"),
        name="final",
    )(pos, route, y_sorted, x1, gate2.reshape(B, 1, D), ln_g.reshape(1, D), ln_b.reshape(1, D))


def kernel(x, c, positions, w_ada, b_ada, w_in, b_forget, w_out, ln1_g, ln1_b, w_router_group, b_router_group,
           w_router_expert, b_router_expert, w_up, w_gate, w_down, ln2_g, ln2_b):
    B, S, D = x.shape
    N = B * S
    assert D == FOX_W + DIL_W and S % 2048 == 0
    mod = _modulation(c, w_ada, b_ada)
    shift1, scale1, gate1, shift2, scale2, gate2 = jnp.split(mod, 6, axis=-1)

    pf, vt, pd, log_f = _projection(x, positions, scale1, shift1, w_in, b_forget, tm=512)
    cum, cum_t = _cumulative_gate(log_f)
    nhp = FOX_W // LANES
    ck = cum[..., :FOX_HEADS].reshape(B, S, nhp, HEADS_PER_VREG).transpose(0, 2, 1, 3)
    cq = cum_t.reshape(B, nhp, HEADS_PER_VREG, S)
    of = _fox_attention(pf, vt, ck, cq)
    od = _dilated_attention(pd)

    tm = 512
    x1, x_sorted, meta, route, tile_cnt = _out_and_route(
        of, od, x, gate1, scale2, shift2, w_out, ln1_g, ln1_b,
        w_router_group, b_router_group, w_router_expert, b_router_expert, tm=tm)

    cnt = tile_cnt[::SUBLANES, N_GROUPS:N_GROUPS + N_EXPERTS].astype(jnp.int32)
    off = jnp.cumsum(cnt, axis=1) - cnt
    tot = jnp.sum(cnt, axis=0)
    pend = jnp.cumsum((tot + MOE_BLOCK - 1) // MOE_BLOCK * MOE_BLOCK)
    nblk = (N * TOP_K) // MOE_BLOCK + N_EXPERTS
    nused = pend[-1:] // MOE_BLOCK
    blk_start = jnp.arange(nblk, dtype=jnp.int32) * MOE_BLOCK
    blk_e = jnp.sum((pend[None, :] <= jnp.minimum(blk_start, pend[-1] - 1)[:, None]).astype(jnp.int32), axis=1)

    y_sorted = _expert_mlp(x_sorted, cnt.reshape(-1), off.reshape(-1), tot, blk_e, nused, w_gate, w_up, w_down)
    return _combine_and_norm(y_sorted, meta, route, x1, gate2, ln2_g, ln2_b, tm=tm)
```

```python
import jax
import jax.numpy as jnp
import numpy as np
from jax import lax
from jax.experimental import pallas as pl
from jax.experimental.pallas import tpu as pltpu

HEAD_DIM = 64
FOX_HEADS = 8
DIL_HEADS = 8
FOX_W = FOX_HEADS * HEAD_DIM
DIL_W = DIL_HEADS * HEAD_DIM
DILATED_PATTERNS = ((128, 1), (512, 4), (2048, 16))
ROPE_THETA = 500000.0
ROT_DIM = HEAD_DIM // 4
N_GROUPS = 4
EXPERTS_PER_GROUP = 8
N_EXPERTS = N_GROUPS * EXPERTS_PER_GROUP
TOP_K = 2
D_EXPERT = 512
MOE_BLOCK = 256
LN_EPS = 1e-5
NEG = -1e30
DEPTH = 1
DEEPNORM_ALPHA = (2 * DEPTH) ** 0.25
QK_SCALE = HEAD_DIM ** -0.5
LOG2E = 1.4426950408889634

LANES = 128
SUBLANES = 8
BF16_SUBLANES = 16
MXU_DIM = 256
HEADS_PER_VREG = LANES // HEAD_DIM
V_ROWS = HEAD_DIM + BF16_SUBLANES
FOX_TILE = 512
DIL_UNROLL = 32
VMEM_LIMIT = 56 * 1024 * 1024

F32 = jnp.float32
BF16 = jnp.bfloat16
HIGHEST = lax.Precision.HIGHEST
NT_DIMS = (((1,), (1,)), ((), ()))


def _params(*sem):
    return pltpu.CompilerParams(dimension_semantics=sem, vmem_limit_bytes=VMEM_LIMIT)


def _sigmoid(v):
    return 1.0 / (1.0 + jnp.exp(-v))


def _layer_norm(v, g, b):
    mu = jnp.mean(v, axis=-1, keepdims=True)
    d = v - mu
    var = jnp.mean(d * d, axis=-1, keepdims=True)
    return d * lax.rsqrt(var + LN_EPS) * g + b


def _mod_kernel(c_ref, w_ref, b_ref, o_ref):
    c = c_ref[...]
    o_ref[...] = jnp.dot(c * _sigmoid(c), w_ref[...], precision=HIGHEST,
                         preferred_element_type=F32) + b_ref[...]


def _modulation(c, w_ada, b_ada):
    B, D = c.shape
    cols = w_ada.shape[1]
    tn = 1024
    return pl.pallas_call(
        _mod_kernel,
        grid=(cols // tn,),
        in_specs=[pl.BlockSpec((B, D), lambda j: (0, 0)),
                  pl.BlockSpec((D, tn), lambda j: (0, j)),
                  pl.BlockSpec((1, tn), lambda j: (0, j))],
        out_specs=pl.BlockSpec((B, tn), lambda j: (0, j)),
        out_shape=jax.ShapeDtypeStruct((B, cols), F32),
        compiler_params=_params("arbitrary"),
        name="mod",
    )(c, w_ada, b_ada.reshape(1, cols))


def _proj_kernel(x_ref, pos_ref, sc_ref, sh_ref, wf_ref, wvt_ref, one_ref, wd_ref, wff_ref, bf_ref, invf_ref,
                 sgn_ref, pf_ref, vt_ref, pd_ref, lf_ref):
    tm = x_ref.shape[1]
    h = (x_ref[0] * (1.0 + sc_ref[0]) + sh_ref[0]).astype(BF16)

    for ci in range(3):
        acc = jnp.dot(h, wf_ref[:, ci * FOX_W:(ci + 1) * FOX_W], preferred_element_type=F32)
        if ci == 0:
            acc = acc * (QK_SCALE * LOG2E)
        pf_ref[0, :, ci * FOX_W:(ci + 1) * FOX_W] = acc.astype(BF16)
    vt = lax.dot_general(wvt_ref[...], h, NT_DIMS, preferred_element_type=F32) + one_ref[...]
    vt_ref[0] = vt.astype(BF16)

    z = jnp.dot(h, wff_ref[...], preferred_element_type=F32) + bf_ref[...]
    lf_ref[0] = (jnp.minimum(z, 0.0) - jnp.log1p(jnp.exp(-jnp.abs(z)))) * LOG2E

    ang = pos_ref[0].astype(F32) * invf_ref[...]
    cs = jnp.cos(ang)
    sn = jnp.sin(ang) * sgn_ref[...]
    lane = lax.broadcasted_iota(jnp.int32, (tm, LANES), 1)
    first_half = (lane % HEAD_DIM) < (ROT_DIM // 2)
    for ci in range(3):
        acc = jnp.dot(h, wd_ref[:, ci * DIL_W:(ci + 1) * DIL_W], preferred_element_type=F32)
        if ci == 2:
            pd_ref[0, :, ci * DIL_W:(ci + 1) * DIL_W] = acc
            continue
        for j in range(DIL_W // LANES):
            t = acc[:, j * LANES:(j + 1) * LANES]
            partner = jnp.where(first_half, pltpu.roll(t, LANES - ROT_DIM // 2, 1),
                                pltpu.roll(t, ROT_DIM // 2, 1))
            r = t * cs + partner * sn
            if ci == 0:
                r = r * (QK_SCALE * LOG2E)
            pd_ref[0, :, ci * DIL_W + j * LANES:ci * DIL_W + (j + 1) * LANES] = r


def _projection(x, positions, scale1, shift1, w_in, b_forget, tm):
    B, S, D = x.shape
    o = np.cumsum((0, FOX_W, FOX_W, FOX_W, FOX_W, FOX_HEADS, DIL_W, DIL_W, DIL_W))
    w_fox = jnp.concatenate([w_in[:, o[0]:o[2]], w_in[:, o[3]:o[4]]], axis=1).astype(BF16)
    w_vt = jnp.pad(w_in[:, o[2]:o[3]].T.reshape(FOX_HEADS, HEAD_DIM, D),
                   ((0, 0), (0, V_ROWS - HEAD_DIM), (0, 0))).reshape(FOX_HEADS * V_ROWS, D).astype(BF16)
    ones_row = jnp.asarray((np.arange(FOX_HEADS * V_ROWS) % V_ROWS == HEAD_DIM).astype(np.float32)
                           ).reshape(FOX_HEADS * V_ROWS, 1)
    w_ff = jnp.pad(w_in[:, o[4]:o[5]], ((0, 0), (0, LANES - FOX_HEADS))).astype(BF16)
    w_dil = w_in[:, o[5]:o[8]].astype(BF16)
    b_f = jnp.pad(b_forget, (0, LANES - FOX_HEADS)).reshape(1, LANES)
    e = np.arange(LANES) % HEAD_DIM
    inv_freq = ROPE_THETA ** (-jnp.arange(0, ROT_DIM, 2, dtype=F32) / ROT_DIM)
    invf = jnp.where(e < ROT_DIM, jnp.tile(inv_freq, LANES // (ROT_DIM // 2)), 0.0).reshape(1, LANES)
    sgn = jnp.asarray(np.where(e < ROT_DIM // 2, -1.0, np.where(e < ROT_DIM, 1.0, 0.0)), F32).reshape(1, LANES)
    const = lambda shape: pl.BlockSpec(shape, lambda b, i: (0,) * len(shape))
    return pl.pallas_call(
        _proj_kernel,
        grid=(B, S // tm),
        in_specs=[pl.BlockSpec((1, tm, D), lambda b, i: (b, i, 0)),
                  pl.BlockSpec((1, tm, 1), lambda b, i: (b, i, 0)),
                  pl.BlockSpec((1, 1, D), lambda b, i: (b, 0, 0)),
                  pl.BlockSpec((1, 1, D), lambda b, i: (b, 0, 0)),
                  const((D, 3 * FOX_W)), const((FOX_HEADS * V_ROWS, D)), const((FOX_HEADS * V_ROWS, 1)),
                  const((D, 3 * DIL_W)), const((D, LANES)),
                  const((1, LANES)), const((1, LANES)), const((1, LANES))],
        out_specs=[pl.BlockSpec((1, tm, 3 * FOX_W), lambda b, i: (b, i, 0)),
                   pl.BlockSpec((1, FOX_HEADS * V_ROWS, tm), lambda b, i: (b, 0, i)),
                   pl.BlockSpec((1, tm, 3 * DIL_W), lambda b, i: (b, i, 0)),
                   pl.BlockSpec((1, tm, LANES), lambda b, i: (b, i, 0))],
        out_shape=[jax.ShapeDtypeStruct((B, S, 3 * FOX_W), BF16),
                   jax.ShapeDtypeStruct((B, FOX_HEADS * V_ROWS, S), BF16),
                   jax.ShapeDtypeStruct((B, S, 3 * DIL_W), F32),
                   jax.ShapeDtypeStruct((B, S, LANES), F32)],
        compiler_params=_params("arbitrary", "arbitrary"),
        name="proj",
    )(x, positions.reshape(B, S, 1), scale1.reshape(B, 1, D), shift1.reshape(B, 1, D),
      w_fox, w_vt, ones_row, w_dil, w_ff, b_f, invf, sgn)


def _cum_kernel(lf_ref, c_ref, ct_ref):
    S = lf_ref.shape[1]
    r = lax.broadcasted_iota(jnp.int32, (LANES, LANES), 0)
    c = lax.broadcasted_iota(jnp.int32, (LANES, LANES), 1)
    tri = (c <= r).astype(F32)

    carry = jnp.zeros((1, LANES), F32)
    for j in range(S // LANES):
        rows = slice(j * LANES, (j + 1) * LANES)
        cum = jnp.dot(tri, lf_ref[0, rows, :], precision=HIGHEST, preferred_element_type=F32) + carry
        c_ref[0, rows, :] = cum
        ct_ref[0, :, rows] = cum.T[0:FOX_HEADS, :]
        carry = cum[LANES - 1:LANES, :]


def _cumulative_gate(log_f):
    B, S, _ = log_f.shape
    return pl.pallas_call(
        _cum_kernel,
        grid=(B,),
        in_specs=[pl.BlockSpec((1, S, LANES), lambda b: (b, 0, 0))],
        out_specs=[pl.BlockSpec((1, S, LANES), lambda b: (b, 0, 0)),
                   pl.BlockSpec((1, FOX_HEADS, S), lambda b: (b, 0, 0))],
        out_shape=[jax.ShapeDtypeStruct((B, S, LANES), F32),
                   jax.ShapeDtypeStruct((B, FOX_HEADS, S), F32)],
        compiler_params=_params("arbitrary"),
        name="cum",
    )(log_f)


def _fox_kernel(q_ref, k_ref, vt_ref, g_ref, ck_ref, cq_ref, o_ref, s_scr):
    S = q_ref.shape[1]
    T = FOX_TILE
    lane = lax.broadcasted_iota(jnp.int32, (T, LANES), 1)

    def q_block(i, _):
        qoff = pl.multiple_of(i * T, T)
        q = q_ref[0, pl.ds(qoff, T), :]
        qh = [jnp.where((lane // HEAD_DIM) == hh, q, jnp.zeros_like(q)) for hh in range(HEADS_PER_VREG)]
        cq = [cq_ref[0, 0, hh:hh + 1, pl.ds(qoff, T)] for hh in range(HEADS_PER_VREG)]

        def scores_into(buf, b):
            kb = k_ref[0, pl.ds(pl.multiple_of(b * T, T), T), :]
            for hh in range(HEADS_PER_VREG):
                s_scr[buf, hh] = lax.dot_general(kb, qh[hh], NT_DIMS, preferred_element_type=F32)

        def attend(buf, b, carry, masked):
            koff = pl.multiple_of(b * T, T)
            new = []
            for hh in range(HEADS_PER_VREG):
                m, acc = carry[hh]

                def scores(c0, rows):
                    s = s_scr[buf, hh, c0:c0 + rows, :] - ck_ref[0, 0, pl.ds(koff + c0, rows), hh:hh + 1]
                    if masked:
                        s = jnp.where(lax.broadcasted_iota(jnp.int32, (rows, T), 0) + c0
                                      <= lax.broadcasted_iota(jnp.int32, (rows, T), 1), s, NEG)
                    return s

                m_new = jnp.maximum(m, cq[hh] + jnp.max(scores(0, T), axis=0, keepdims=True))
                row = cq[hh] - m_new
                acc = jnp.exp2(m - m_new) * acc
                for c0 in range(0, T, MXU_DIM):
                    p = jnp.exp2((scores(c0, MXU_DIM) + row).astype(BF16))
                    vt = vt_ref[0, hh * V_ROWS:(hh + 1) * V_ROWS, pl.ds(koff + c0, MXU_DIM)]
                    acc = acc + jnp.dot(vt, p, preferred_element_type=F32)
                new.append((m_new, acc))
            return tuple(new)

        def pair(t, carry):
            scores_into(1, 2 * t + 1)
            carry = attend(0, 2 * t, carry, False)
            scores_into(0, 2 * t + 2)
            return attend(1, 2 * t + 1, carry, False)

        def odd_tail(carry):
            scores_into(1, i)
            return attend(1, i, attend(0, i - 1, carry, False), True)

        def even_tail(carry):
            return attend(0, i, carry, True)

        init = tuple((jnp.full((1, T), NEG, F32), jnp.zeros((V_ROWS, T), F32)) for _ in range(HEADS_PER_VREG))
        scores_into(0, 0)
        carry = lax.fori_loop(0, i // 2, pair, init)
        carry = lax.cond(i % 2 == 1, odd_tail, even_tail, carry)
        o_t = jnp.concatenate([acc[0:HEAD_DIM] * (1.0 / acc[HEAD_DIM:HEAD_DIM + 1]) for (_, acc) in carry], axis=0)
        gate = _sigmoid(g_ref[0, pl.ds(qoff, T), :].astype(F32))
        o_ref[0, pl.ds(qoff, T), :] = (o_t.T * gate).astype(BF16)
        return 0

    lax.fori_loop(0, S // T, q_block, 0)


def _fox_attention(pf, vt, ck, cq):
    B, S, _ = pf.shape
    nhp = FOX_W // LANES
    assert S % FOX_TILE == 0
    return pl.pallas_call(
        _fox_kernel,
        grid=(B, nhp),
        in_specs=[pl.BlockSpec((1, S, LANES), lambda b, h: (b, 0, h)),
                  pl.BlockSpec((1, S, LANES), lambda b, h: (b, 0, nhp + h)),
                  pl.BlockSpec((1, HEADS_PER_VREG * V_ROWS, S), lambda b, h: (b, h, 0)),
                  pl.BlockSpec((1, S, LANES), lambda b, h: (b, 0, 2 * nhp + h)),
                  pl.BlockSpec((1, 1, S, HEADS_PER_VREG), lambda b, h: (b, h, 0, 0)),
                  pl.BlockSpec((1, 1, HEADS_PER_VREG, S), lambda b, h: (b, h, 0, 0))],
        out_specs=pl.BlockSpec((1, S, LANES), lambda b, h: (b, 0, h)),
        out_shape=jax.ShapeDtypeStruct((B, S, FOX_W), BF16),
        scratch_shapes=[pltpu.VMEM((2, HEADS_PER_VREG, FOX_TILE, FOX_TILE), F32)],
        compiler_params=_params("arbitrary", "arbitrary"),
        name="fox",
    )(pf, pf, vt, pf, ck, cq)


def _dil_kernel(q_ref, k_ref, v_ref, o_ref, m_s, l_s, a_s, q4, k4, v4, m4, l4, a4):
    S = q_ref.shape[1]
    W = LANES
    lane = lax.broadcasted_iota(jnp.int32, (W, LANES), 1)
    head0 = lane < HEAD_DIM
    ri = lax.broadcasted_iota(jnp.int32, (2 * W, 2 * W), 0) % W
    ci = lax.broadcasted_iota(jnp.int32, (2 * W, 2 * W), 1)
    bias_rest = jnp.where((ci >= ri) & (ci <= ri + W), 0.0, NEG)
    bias_first = jnp.where((ci < W) & (ci <= ri), 0.0, NEG)
    ones = jnp.ones((2 * W, LANES), BF16)

    assert DILATED_PATTERNS == ((W, 1), (4 * W, 4), (16 * W, 16)) and S % (16 * W * 2) == 0
    assert (S // W) % DIL_UNROLL == 0
    L4 = S // 4

    def block_stats(q, kb, vb, first):
        qs = jnp.concatenate([jnp.where(head0, q, 0.0), jnp.where(head0, 0.0, q)], axis=0).astype(BF16)
        vb2 = jnp.concatenate([vb.astype(BF16), ones], axis=1)
        s = lax.dot_general(qs, kb.astype(BF16), NT_DIMS, preferred_element_type=F32)
        s = s + jnp.where(first, bias_first, bias_rest)
        m2 = jnp.max(s, axis=1, keepdims=True)
        out = jnp.dot(jnp.exp2((s - m2).astype(BF16)), vb2, preferred_element_type=F32)
        a_u = jnp.where(head0, out[0:W, 0:LANES], out[W:2 * W, 0:LANES])
        l_u = jnp.where(head0, out[0:W, LANES:2 * LANES], out[W:2 * W, LANES:2 * LANES])
        m_u = jnp.where(head0, m2[0:W], m2[W:2 * W])
        return m_u, l_u, a_u

    def merge(idx, stats):
        m_u, l_u, a_u = stats
        m_o = m4[idx, :]
        m_n = jnp.maximum(m_o, m_u)
        e_o = jnp.exp2(m_o - m_n)
        e_u = jnp.exp2(m_u - m_n)
        m4[idx, :] = m_n
        l4[idx, :] = l4[idx, :] * e_o + l_u * e_u
        a4[idx, :] = a4[idx, :] * e_o + a_u * e_u

    def for_blocks(unit):
        def group(t, _):
            for uu in range(DIL_UNROLL):
                unit(t * DIL_UNROLL + uu)
            return 0
        lax.fori_loop(0, S // W // DIL_UNROLL, group, 0)

    def unit1(u):
        qidx = pl.ds(pl.multiple_of(u * W, W), W)
        kidx = pl.ds(pl.multiple_of(jnp.maximum(u - 1, 0) * W, W), 2 * W)
        m_s[qidx, :], l_s[qidx, :], a_s[qidx, :] = block_stats(q_ref[0, qidx, :], k_ref[0, kidx, :],
                                                               v_ref[0, kidx, :], u == 0)
    for_blocks(unit1)

    def reorder(c, _):
        for r in range(4):
            src = pl.ds(c * (4 * W) + r, W, stride=4)
            dst = pl.ds(pl.multiple_of(r * L4 + c * W, W), W)
            q4[dst, :] = q_ref[0, src, :]
            k4[dst, :] = k_ref[0, src, :]
            v4[dst, :] = v_ref[0, src, :]
            m4[dst, :] = m_s[src, :]
            l4[dst, :] = l_s[src, :]
            a4[dst, :] = a_s[src, :]
        return 0
    lax.fori_loop(0, L4 // W, reorder, 0)

    nb4 = L4 // W

    def unit4(u):
        n = u % nb4
        base = (u // nb4) * L4
        qidx = pl.ds(pl.multiple_of(base + n * W, W), W)
        kidx = pl.ds(pl.multiple_of(base + jnp.maximum(n - 1, 0) * W, W), 2 * W)
        merge(qidx, block_stats(q4[qidx, :], k4[kidx, :], v4[kidx, :], n == 0))
    for_blocks(unit4)

    nb16 = S // (16 * W)

    def unit16(u):
        n = u % nb16
        r16 = u // nb16
        base = (r16 % 4) * L4 + r16 // 4
        qidx = pl.ds(base + n * (4 * W), W, stride=4)
        kidx = pl.ds(base + jnp.maximum(n - 1, 0) * (4 * W), 2 * W, stride=4)
        merge(qidx, block_stats(q4[qidx, :], k4[kidx, :], v4[kidx, :], n == 0))
    for_blocks(unit16)

    def finish(c, _):
        for r in range(4):
            src = pl.ds(pl.multiple_of(r * L4 + c * W, W), W)
            a_s[pl.ds(c * (4 * W) + r, W, stride=4), :] = a4[src, :] * (1.0 / l4[src, :])
        return 0
    lax.fori_loop(0, L4 // W, finish, 0)
    o_ref[0] = a_s[...].astype(BF16)


def _dilated_attention(pd):
    B, S, _ = pd.shape
    nhp = DIL_W // LANES
    return pl.pallas_call(
        _dil_kernel,
        grid=(B, nhp),
        in_specs=[pl.BlockSpec((1, S, LANES), lambda b, h: (b, 0, h)),
                  pl.BlockSpec((1, S, LANES), lambda b, h: (b, 0, nhp + h)),
                  pl.BlockSpec((1, S, LANES), lambda b, h: (b, 0, 2 * nhp + h))],
        out_specs=pl.BlockSpec((1, S, LANES), lambda b, h: (b, 0, h)),
        out_shape=jax.ShapeDtypeStruct((B, S, DIL_W), BF16),
        scratch_shapes=[pltpu.VMEM((S, LANES), F32)] * 9,
        compiler_params=_params("arbitrary", "arbitrary"),
        name="dil",
    )(pd, pd, pd)


def _out_kernel(of_ref, od_ref, x_ref, g1_ref, sc2_ref, sh2_ref, wo_ref, lng_ref, lnb_ref, wrh_ref, wrl_ref, br_ref,
                x1_ref, xs_ref, meta_ref, rt_ref, cnt_ref, h2_s, pos_v, pos_s, sem):
    tm = x_ref.shape[1]
    y = (jnp.dot(of_ref[0], wo_ref[0:FOX_W, :], preferred_element_type=F32)
         + jnp.dot(od_ref[0], wo_ref[FOX_W:FOX_W + DIL_W, :], preferred_element_type=F32))
    x1 = _layer_norm(DEEPNORM_ALPHA * x_ref[0] + g1_ref[0] * y, lng_ref[...], lnb_ref[...])
    x1_ref[0] = x1
    h2 = x1 * (1.0 + sc2_ref[0]) + sh2_ref[0]

    h_hi = h2.astype(BF16)
    h_lo = (h2 - h_hi.astype(F32)).astype(BF16)
    lg = (jnp.dot(h_hi, wrh_ref[...], preferred_element_type=F32)
          + jnp.dot(h_lo, wrh_ref[...], preferred_element_type=F32)
          + jnp.dot(h_hi, wrl_ref[...], preferred_element_type=F32)) + br_ref[...]
    lane = lax.broadcasted_iota(jnp.int32, (tm, LANES), 1)
    lanef = lane.astype(F32)
    far = float(LANES)
    is_g = lane < N_GROUPS
    gl = jnp.where(is_g, lg, NEG)
    gmax = jnp.max(gl, axis=1, keepdims=True)
    gidx = jnp.min(jnp.where(gl == gmax, lanef, far), axis=1, keepdims=True)
    p_g = 1.0 / jnp.sum(jnp.where(is_g, jnp.exp(gl - gmax), 0.0), axis=1, keepdims=True)
    e_lo = N_GROUPS + gidx * EXPERTS_PER_GROUP
    in_group = (lanef >= e_lo) & (lanef < e_lo + EXPERTS_PER_GROUP)
    el = jnp.where(in_group, lg, NEG)
    v1 = jnp.max(el, axis=1, keepdims=True)
    i1 = jnp.min(jnp.where(in_group & (el == v1), lanef, far), axis=1, keepdims=True)
    rest = in_group & (lanef != i1)
    el2 = jnp.where(rest, lg, NEG)
    v2 = jnp.max(el2, axis=1, keepdims=True)
    i2 = jnp.min(jnp.where(rest & (el2 == v2), lanef, far), axis=1, keepdims=True)
    e21 = jnp.exp(v2 - v1)
    w1 = p_g / (1.0 + e21)
    w2 = p_g * e21 / (1.0 + e21)

    pick1 = lanef == i1
    pick2 = lanef == i2
    onehot = jnp.where(pick1 | pick2, 1.0, 0.0)
    rr = lax.broadcasted_iota(jnp.int32, (tm, tm), 0)
    cc = lax.broadcasted_iota(jnp.int32, (tm, tm), 1)
    strict_lower = jnp.where(cc < rr, 1.0, 0.0).astype(BF16)
    before = jnp.dot(strict_lower, onehot.astype(BF16), preferred_element_type=F32)
    cnt = jnp.broadcast_to(jnp.sum(onehot, axis=0, keepdims=True), (SUBLANES, LANES))
    lane8 = lax.broadcasted_iota(jnp.int32, (SUBLANES, LANES), 1)
    incl = cnt
    for sh in (1, 2, 4, 8, 16, 32, 64):
        incl = incl + jnp.where(lane8 >= sh, pltpu.roll(incl, sh, 1), 0.0)
    start = before + (incl - cnt)[0:1]
    d1 = jnp.sum(jnp.where(pick1, start, 0.0), axis=1, keepdims=True)
    d2 = jnp.sum(jnp.where(pick2, start, 0.0), axis=1, keepdims=True)
    cnt_ref[...] = cnt

    packed = jnp.zeros((tm, LANES), F32)
    for k, val in enumerate((d1, d2, w1, w2)):
        packed = jnp.where(lane == k, val, packed)
    rt_ref[0] = packed
    meta = packed.T[0:SUBLANES, :]
    meta_ref[0] = meta
    pos_v[...] = (meta * SUBLANES).astype(jnp.int32)
    to_smem = pltpu.make_async_copy(pos_v, pos_s, sem)
    to_smem.start()
    for s in range(SUBLANES):
        h2_s[pl.ds(s, tm, stride=SUBLANES), :] = h2[:, s * LANES:(s + 1) * LANES]
    to_smem.wait()

    def place(t8, _):
        for u in range(SUBLANES):
            t = t8 * SUBLANES + u
            row = h2_s[pl.ds(pl.multiple_of(t * SUBLANES, SUBLANES), SUBLANES), :]
            for k in range(TOP_K):
                xs_ref[pl.ds(pl.multiple_of(pos_s[k, t], SUBLANES), SUBLANES), :] = row
        return 0

    lax.fori_loop(0, tm // SUBLANES, place, 0)


def _out_and_route(of, od, x, gate1, scale2, shift2, w_out, ln_g, ln_b, w_rg, b_rg, w_re, b_re, tm):
    B, S, D = x.shape
    n_r = N_GROUPS + N_EXPERTS
    w_r = jnp.pad(jnp.concatenate([w_rg, w_re], axis=1), ((0, 0), (0, LANES - n_r)))
    b_r = jnp.pad(jnp.concatenate([b_rg, b_re]), (0, LANES - n_r)).reshape(1, LANES)
    w_r_hi = w_r.astype(BF16)
    w_r_lo = (w_r - w_r_hi.astype(F32)).astype(BF16)
    nt = S // tm
    tile = lambda w: pl.BlockSpec((1, tm, w), lambda b, i: (b, i, 0))
    per_batch = pl.BlockSpec((1, 1, D), lambda b, i: (b, 0, 0))
    const = lambda shape: pl.BlockSpec(shape, lambda b, i: (0,) * len(shape))
    return pl.pallas_call(
        _out_kernel,
        grid=(B, S // tm),
        in_specs=[tile(FOX_W), tile(DIL_W), tile(D), per_batch, per_batch, per_batch,
                  const((D, D)), const((1, D)), const((1, D)), const((D, LANES)), const((D, LANES)),
                  const((1, LANES))],
        out_specs=[tile(D),
                   pl.BlockSpec((TOP_K * tm * SUBLANES, LANES), lambda b, i: (b * nt + i, 0)),
                   pl.BlockSpec((1, SUBLANES, tm), lambda b, i: (b * nt + i, 0, 0)),
                   tile(LANES),
                   pl.BlockSpec((SUBLANES, LANES), lambda b, i: (b * nt + i, 0))],
        out_shape=[jax.ShapeDtypeStruct((B, S, D), F32),
                   jax.ShapeDtypeStruct((TOP_K * B * S * SUBLANES, LANES), F32),
                   jax.ShapeDtypeStruct((B * nt, SUBLANES, tm), F32),
                   jax.ShapeDtypeStruct((B, S, LANES), F32),
                   jax.ShapeDtypeStruct((B * nt * SUBLANES, LANES), F32)],
        scratch_shapes=[pltpu.VMEM((tm * SUBLANES, LANES), F32), pltpu.VMEM((SUBLANES, tm), jnp.int32),
                        pltpu.SMEM((SUBLANES, tm), jnp.int32), pltpu.SemaphoreType.DMA(())],
        compiler_params=_params("arbitrary", "arbitrary"),
        name="out",
    )(of, od, x, gate1.reshape(B, 1, D), scale2.reshape(B, 1, D), shift2.reshape(B, 1, D),
      w_out.astype(BF16), ln_g.reshape(1, D), ln_b.reshape(1, D), w_r_hi, w_r_lo, b_r)


RUN_BITS = MOE_BLOCK.bit_length()
RUN_SMALL_BITS = 5
MAX_RUNS = 80
ST_EXPERT, ST_TILE, ST_OFF, ST_DONE = range(4)


def _copy_rows(src_ref, src_row, dst_ref, dst_row, n, sem, wait):
    def pieces(bits, pos):
        for bit in bits:
            size = 1 << bit

            @pl.when((n & size) != 0)
            def _(pos=pos, size=size):
                cp = pltpu.make_async_copy(
                    src_ref.at[pl.ds(pl.multiple_of((src_row + pos) * SUBLANES, SUBLANES), size * SUBLANES), :],
                    dst_ref.at[pl.ds(pl.multiple_of((dst_row + pos) * SUBLANES, SUBLANES), size * SUBLANES), :],
                    sem)
                if wait:
                    cp.wait()
                else:
                    cp.start()

            pos = pos + (n & size)

    small = 1 << RUN_SMALL_BITS
    pl.when(n >= small)(lambda: pieces(reversed(range(RUN_SMALL_BITS, RUN_BITS)), 0))
    pieces(reversed(range(RUN_SMALL_BITS)), n & ~(small - 1))


def _moe_kernel(blk_e_ref, nused_ref, cnt_ref, off_ref, tot_ref, x_ref, wg_ref, wu_ref, wd_ref, y_ref,
                xbuf0, xbuf1, ybuf0, ybuf1, mid_s, st, runs, sruns, gsem, ssem):
    del blk_e_ref
    i = pl.program_id(0)
    nused = nused_ref[0]
    D = xbuf0.shape[0] // MOE_BLOCK * LANES
    nk = D // MXU_DIM
    n_tiles = cnt_ref.shape[0] // N_EXPERTS
    rows_per_tile = x_ref.shape[0] // SUBLANES // n_tiles

    def gather_next(xdst, s):
        def next_expert(c):
            return c[0] + 1, 0, 0, 0

        def exhausted(c):
            return (c[0] < N_EXPERTS) & (c[3] >= tot_ref[jnp.minimum(c[0], N_EXPERTS - 1)])

        e, tile, off, done = lax.while_loop(
            exhausted, next_expert, (st[ST_EXPERT], st[ST_TILE], st[ST_OFF], st[ST_DONE]))
        ec = jnp.minimum(e, N_EXPERTS - 1)
        total = jnp.where(e < N_EXPERTS, tot_ref[ec], 0)

        def more(c):
            return (c[0] < MOE_BLOCK) & (c[4] < total)

        def take_run(c):
            filled, k, tile, off, done = c
            run = cnt_ref[tile * N_EXPERTS + ec]
            take = jnp.minimum(run - off, MOE_BLOCK - filled)
            src = tile * rows_per_tile + off_ref[tile * N_EXPERTS + ec] + off
            _copy_rows(x_ref, src, xdst, filled, take, gsem.at[s], wait=False)
            base = (s * MAX_RUNS + k) * 3
            runs[base] = src
            runs[base + 1] = filled
            runs[base + 2] = take
            run_done = off + take >= run
            return (filled + take, k + jnp.where(take > 0, 1, 0), jnp.where(run_done, tile + 1, tile),
                    jnp.where(run_done, 0, off + take), done + take)

        filled, k, tile, off, done = lax.while_loop(more, take_run, (0, 0, tile, off, done))
        _copy_rows(x_ref, 0, xdst, filled, jnp.where(total > 0, MOE_BLOCK - filled, 0), gsem.at[s], wait=False)
        runs[(2 * MAX_RUNS + s) * 3] = k
        st[ST_EXPERT] = e
        st[ST_TILE] = tile
        st[ST_OFF] = off
        st[ST_DONE] = done

    def wait_gather(xdst, s):
        pltpu.make_async_copy(x_ref.at[pl.ds(0, MOE_BLOCK * SUBLANES), :], xdst, gsem.at[s]).wait()

    def scatter_block(ysrc, s):
        def one(k, rows):
            base = (s * MAX_RUNS + k) * 3
            _copy_rows(ysrc, runs[base + 1], y_ref, runs[base], runs[base + 2], ssem.at[s], wait=False)
            return rows + runs[base + 2]

        sruns[s] = lax.fori_loop(0, runs[(2 * MAX_RUNS + s) * 3], one, 0)

    def wait_scatter(ysrc, s):
        n = pl.multiple_of(sruns[s] * SUBLANES, SUBLANES)
        pltpu.make_async_copy(ysrc.at[pl.ds(0, n), :], y_ref.at[pl.ds(0, n), :], ssem.at[s]).wait()

    def compute(xcur, ycur):
        xk = [jnp.concatenate([xcur[pl.ds(2 * j, MOE_BLOCK, stride=SUBLANES), :],
                               xcur[pl.ds(2 * j + 1, MOE_BLOCK, stride=SUBLANES), :]], axis=1).astype(BF16)
              for j in range(nk)]
        for c in range(D_EXPERT // MXU_DIM):
            cols = slice(c * MXU_DIM, (c + 1) * MXU_DIM)
            g = sum(jnp.dot(xk[j], wg_ref[0, MXU_DIM * j:MXU_DIM * (j + 1), cols].astype(BF16),
                            preferred_element_type=F32) for j in range(nk))
            u = sum(jnp.dot(xk[j], wu_ref[0, MXU_DIM * j:MXU_DIM * (j + 1), cols].astype(BF16),
                            preferred_element_type=F32) for j in range(nk))
            mid_s[:, cols] = (g * _sigmoid(g) * u).astype(BF16)
        mid = mid_s[...]
        for c in range(D // MXU_DIM):
            y = jnp.dot(mid, wd_ref[0, :, c * MXU_DIM:(c + 1) * MXU_DIM].astype(BF16), preferred_element_type=F32)
            ycur[pl.ds(2 * c, MOE_BLOCK, stride=SUBLANES), :] = y[:, 0:LANES]
            ycur[pl.ds(2 * c + 1, MOE_BLOCK, stride=SUBLANES), :] = y[:, LANES:2 * LANES]

    def step(s, xcur, xnxt, ycur, yprv):
        o = 1 - s
        wait_gather(xcur, s)

        @pl.when(i + 1 < nused)
        def _():
            gather_next(xnxt, o)

        @pl.when(i >= 2)
        def _():
            wait_scatter(ycur, s)

        compute(xcur, ycur)
        scatter_block(ycur, s)

        @pl.when(i + 1 == nused)
        def _():
            @pl.when(i >= 1)
            def _():
                wait_scatter(yprv, o)

            wait_scatter(ycur, s)

    @pl.when(i == 0)
    def _():
        for f in range(4):
            st[f] = 0
        gather_next(xbuf0, 0)

    @pl.when((i < nused) & (i % 2 == 0))
    def _():
        step(0, xbuf0, xbuf1, ybuf0, ybuf1)

    @pl.when((i < nused) & (i % 2 == 1))
    def _():
        step(1, xbuf1, xbuf0, ybuf1, ybuf0)


def _expert_mlp(x_sorted, cnt, off, tot, blk_e, nused, w_gate, w_up, w_down):
    D = w_gate.shape[1]
    nblk = blk_e.shape[0]
    assert cnt.shape[0] // N_EXPERTS + 2 <= MAX_RUNS
    wmap = lambda i, be, nu, c, o, t: (be[i], 0, 0)
    return pl.pallas_call(
        _moe_kernel,
        grid_spec=pltpu.PrefetchScalarGridSpec(
            num_scalar_prefetch=5,
            grid=(nblk,),
            in_specs=[pl.BlockSpec(memory_space=pl.ANY),
                      pl.BlockSpec((1, D, D_EXPERT), wmap),
                      pl.BlockSpec((1, D, D_EXPERT), wmap),
                      pl.BlockSpec((1, D_EXPERT, D), wmap)],
            out_specs=pl.BlockSpec(memory_space=pl.ANY),
            scratch_shapes=[pltpu.VMEM((MOE_BLOCK * SUBLANES, LANES), F32)] * 4
            + [pltpu.VMEM((MOE_BLOCK, D_EXPERT), BF16),
               pltpu.SMEM((4,), jnp.int32),
               pltpu.SMEM(((2 * MAX_RUNS + 2) * 3,), jnp.int32), pltpu.SMEM((2,), jnp.int32),
               pltpu.SemaphoreType.DMA((2,)), pltpu.SemaphoreType.DMA((2,))]),
        out_shape=jax.ShapeDtypeStruct(x_sorted.shape, F32),
        compiler_params=_params("arbitrary"),
        name="moe",
    )(blk_e, nused, cnt, off, tot, x_sorted, w_gate, w_up, w_down)


def _final_kernel(pos_ref, rt_ref, y_ref, x1_ref, g2_ref, lng_ref, lnb_ref, o_ref, pick_s):
    tm = x1_ref.shape[1]

    def unsort(t8, _):
        for u in range(SUBLANES):
            t = t8 * SUBLANES + u
            for k in range(TOP_K):
                pick_s[k, pl.ds(pl.multiple_of(t * SUBLANES, SUBLANES), SUBLANES), :] = (
                    y_ref[pl.ds(pl.multiple_of(pos_ref[k, t], SUBLANES), SUBLANES), :])
        return 0

    lax.fori_loop(0, tm // SUBLANES, unsort, 0)
    wts = rt_ref[0]
    y = jnp.concatenate([wts[:, TOP_K:TOP_K + 1] * pick_s[0, pl.ds(s, tm, stride=SUBLANES), :]
                         + wts[:, TOP_K + 1:TOP_K + 2] * pick_s[1, pl.ds(s, tm, stride=SUBLANES), :]
                         for s in range(SUBLANES)], axis=1)
    o_ref[0] = _layer_norm(DEEPNORM_ALPHA * x1_ref[0] + g2_ref[0] * y, lng_ref[...], lnb_ref[...])


def _combine_and_norm(y_sorted, meta, route, x1, gate2, ln_g, ln_b, tm):
    B, S, D = x1.shape
    nt = S // tm
    pos = (meta * SUBLANES).astype(jnp.int32).reshape(B * nt * SUBLANES, tm)
    return pl.pallas_call(
        _final_kernel,
        grid=(B, nt),
        in_specs=[pl.BlockSpec((SUBLANES, tm), lambda b, i: (b * nt + i, 0), memory_space=pltpu.SMEM),
                  pl.BlockSpec((1, tm, LANES), lambda b, i: (b, i, 0)),
                  pl.BlockSpec((TOP_K * tm * SUBLANES, LANES), lambda b, i: (b * nt + i, 0)),
                  pl.BlockSpec((1, tm, D), lambda b, i: (b, i, 0)),
                  pl.BlockSpec((1, 1, D), lambda b, i: (b, 0, 0)),
                  pl.BlockSpec((1, D), lambda b, i: (0, 0)),
                  pl.BlockSpec((1, D), lambda b, i: (0, 0))],
        out_specs=pl.BlockSpec((1, tm, D), lambda b, i: (b, i, 0)),
        out_shape=jax.ShapeDtypeStruct((B, S, D), F32),
        scratch_shapes=[pltpu.VMEM((TOP_K, tm * SUBLANES, LANES), F32)],
        compiler_params=_params("arbitrary", "arbitrary"),
        name="final",
    )(pos, route, y_sorted, x1, gate2.reshape(B, 1, D), ln_g.reshape(1, D), ln_b.reshape(1, D))


def kernel(x, c, positions, w_ada, b_ada, w_in, b_forget, w_out, ln1_g, ln1_b, w_router_group, b_router_group,
           w_router_expert, b_router_expert, w_up, w_gate, w_down, ln2_g, ln2_b):
    B, S, D = x.shape
    N = B * S
    assert D == FOX_W + DIL_W and S % 2048 == 0
    mod = _modulation(c, w_ada, b_ada)
    shift1, scale1, gate1, shift2, scale2, gate2 = jnp.split(mod, 6, axis=-1)

    pf, vt, pd, log_f = _projection(x, positions, scale1, shift1, w_in, b_forget, tm=512)
    cum, cum_t = _cumulative_gate(log_f)
    nhp = FOX_W // LANES
    ck = cum[..., :FOX_HEADS].reshape(B, S, nhp, HEADS_PER_VREG).transpose(0, 2, 1, 3)
    cq = cum_t.reshape(B, nhp, HEADS_PER_VREG, S)
    of = _fox_attention(pf, vt, ck, cq)
    od = _dilated_attention(pd)

    tm = 512
    x1, x_sorted, meta, route, tile_cnt = _out_and_route(
        of, od, x, gate1, scale2, shift2, w_out, ln1_g, ln1_b,
        w_router_group, b_router_group, w_router_expert, b_router_expert, tm=tm)

    cnt = tile_cnt[::SUBLANES, N_GROUPS:N_GROUPS + N_EXPERTS].astype(jnp.int32)
    off = jnp.cumsum(cnt, axis=1) - cnt
    tot = jnp.sum(cnt, axis=0)
    pend = jnp.cumsum((tot + MOE_BLOCK - 1) // MOE_BLOCK * MOE_BLOCK)
    nblk = (N * TOP_K) // MOE_BLOCK + N_EXPERTS
    nused = pend[-1:] // MOE_BLOCK
    blk_start = jnp.arange(nblk, dtype=jnp.int32) * MOE_BLOCK
    blk_e = jnp.sum((pend[None, :] <= jnp.minimum(blk_start, pend[-1] - 1)[:, None]).astype(jnp.int32), axis=1)

    y_sorted = _expert_mlp(x_sorted, cnt.reshape(-1), off.reshape(-1), tot, blk_e, nused, w_gate, w_up, w_down)
    return _combine_and_norm(y_sorted, meta, route, x1, gate2, ln2_g, ln2_b, tm=tm)
```

```python
import jax
import jax.numpy as jnp
import numpy as np
from jax import lax
from jax.experimental import pallas as pl
from jax.experimental.pallas import tpu as pltpu

HEAD_DIM = 64
FOX_HEADS = 8
DIL_HEADS = 8
FOX_W = FOX_HEADS * HEAD_DIM
DIL_W = DIL_HEADS * HEAD_DIM
DILATED_PATTERNS = ((128, 1), (512, 4), (2048, 16))
ROPE_THETA = 500000.0
ROT_DIM = HEAD_DIM // 4
N_GROUPS = 4
EXPERTS_PER_GROUP = 8
N_EXPERTS = N_GROUPS * EXPERTS_PER_GROUP
TOP_K = 2
D_EXPERT = 512
MOE_BLOCK = 256
LN_EPS = 1e-5
NEG = -1e30
DEPTH = 1
DEEPNORM_ALPHA = (2 * DEPTH) ** 0.25
QK_SCALE = HEAD_DIM ** -0.5
LOG2E = 1.4426950408889634

LANES = 128
SUBLANES = 8
BF16_SUBLANES = 16
MXU_DIM = 256
HEADS_PER_VREG = LANES // HEAD_DIM
V_ROWS = HEAD_DIM + BF16_SUBLANES
FOX_TILE = 512
DIL_UNROLL = 32
VMEM_LIMIT = 56 * 1024 * 1024

F32 = jnp.float32
BF16 = jnp.bfloat16
HIGHEST = lax.Precision.HIGHEST
NT_DIMS = (((1,), (1,)), ((), ()))


def _params(*sem):
    return pltpu.CompilerParams(dimension_semantics=sem, vmem_limit_bytes=VMEM_LIMIT)


def _sigmoid(v):
    return 1.0 / (1.0 + jnp.exp(-v))


def _layer_norm(v, g, b):
    mu = jnp.mean(v, axis=-1, keepdims=True)
    d = v - mu
    var = jnp.mean(d * d, axis=-1, keepdims=True)
    return d * lax.rsqrt(var + LN_EPS) * g + b


def _mod_kernel(c_ref, w_ref, b_ref, o_ref):
    c = c_ref[...]
    o_ref[...] = jnp.dot(c * _sigmoid(c), w_ref[...], precision=HIGHEST,
                         preferred_element_type=F32) + b_ref[...]


def _modulation(c, w_ada, b_ada):
    B, D = c.shape
    cols = w_ada.shape[1]
    tn = 1024
    return pl.pallas_call(
        _mod_kernel,
        grid=(cols // tn,),
        in_specs=[pl.BlockSpec((B, D), lambda j: (0, 0)),
                  pl.BlockSpec((D, tn), lambda j: (0, j)),
                  pl.BlockSpec((1, tn), lambda j: (0, j))],
        out_specs=pl.BlockSpec((B, tn), lambda j: (0, j)),
        out_shape=jax.ShapeDtypeStruct((B, cols), F32),
        compiler_params=_params("arbitrary"),
        name="mod",
    )(c, w_ada, b_ada.reshape(1, cols))


def _proj_kernel(x_ref, pos_ref, sc_ref, sh_ref, wf_ref, wvt_ref, one_ref, wd_ref, wff_ref, bf_ref, invf_ref,
                 sgn_ref, pf_ref, vt_ref, pd_ref, lf_ref):
    tm = x_ref.shape[1]
    h = (x_ref[0] * (1.0 + sc_ref[0]) + sh_ref[0]).astype(BF16)

    for ci in range(3):
        acc = jnp.dot(h, wf_ref[:, ci * FOX_W:(ci + 1) * FOX_W], preferred_element_type=F32)
        if ci == 0:
            acc = acc * (QK_SCALE * LOG2E)
        pf_ref[0, :, ci * FOX_W:(ci + 1) * FOX_W] = acc.astype(BF16)
    vt = lax.dot_general(wvt_ref[...], h, NT_DIMS, preferred_element_type=F32) + one_ref[...]
    vt_ref[0] = vt.astype(BF16)

    z = jnp.dot(h, wff_ref[...], preferred_element_type=F32) + bf_ref[...]
    lf_ref[0] = (jnp.minimum(z, 0.0) - jnp.log1p(jnp.exp(-jnp.abs(z)))) * LOG2E

    ang = pos_ref[0].astype(F32) * invf_ref[...]
    cs = jnp.cos(ang)
    sn = jnp.sin(ang) * sgn_ref[...]
    lane = lax.broadcasted_iota(jnp.int32, (tm, LANES), 1)
    first_half = (lane % HEAD_DIM) < (ROT_DIM // 2)
    for ci in range(3):
        acc = jnp.dot(h, wd_ref[:, ci * DIL_W:(ci + 1) * DIL_W], preferred_element_type=F32)
        if ci == 2:
            pd_ref[0, :, ci * DIL_W:(ci + 1) * DIL_W] = acc
            continue
        for j in range(DIL_W // LANES):
            t = acc[:, j * LANES:(j + 1) * LANES]
            partner = jnp.where(first_half, pltpu.roll(t, LANES - ROT_DIM // 2, 1),
                                pltpu.roll(t, ROT_DIM // 2, 1))
            r = t * cs + partner * sn
            if ci == 0:
                r = r * (QK_SCALE * LOG2E)
            pd_ref[0, :, ci * DIL_W + j * LANES:ci * DIL_W + (j + 1) * LANES] = r


def _projection(x, positions, scale1, shift1, w_in, b_forget, tm):
    B, S, D = x.shape
    o = np.cumsum((0, FOX_W, FOX_W, FOX_W, FOX_W, FOX_HEADS, DIL_W, DIL_W, DIL_W))
    w_fox = jnp.concatenate([w_in[:, o[0]:o[2]], w_in[:, o[3]:o[4]]], axis=1).astype(BF16)
    w_vt = jnp.pad(w_in[:, o[2]:o[3]].T.reshape(FOX_HEADS, HEAD_DIM, D),
                   ((0, 0), (0, V_ROWS - HEAD_DIM), (0, 0))).reshape(FOX_HEADS * V_ROWS, D).astype(BF16)
    ones_row = jnp.asarray((np.arange(FOX_HEADS * V_ROWS) % V_ROWS == HEAD_DIM).astype(np.float32)
                           ).reshape(FOX_HEADS * V_ROWS, 1)
    w_ff = jnp.pad(w_in[:, o[4]:o[5]], ((0, 0), (0, LANES - FOX_HEADS))).astype(BF16)
    w_dil = w_in[:, o[5]:o[8]].astype(BF16)
    b_f = jnp.pad(b_forget, (0, LANES - FOX_HEADS)).reshape(1, LANES)
    e = np.arange(LANES) % HEAD_DIM
    inv_freq = ROPE_THETA ** (-jnp.arange(0, ROT_DIM, 2, dtype=F32) / ROT_DIM)
    invf = jnp.where(e < ROT_DIM, jnp.tile(inv_freq, LANES // (ROT_DIM // 2)), 0.0).reshape(1, LANES)
    sgn = jnp.asarray(np.where(e < ROT_DIM // 2, -1.0, np.where(e < ROT_DIM, 1.0, 0.0)), F32).reshape(1, LANES)
    const = lambda shape: pl.BlockSpec(shape, lambda b, i: (0,) * len(shape))
    return pl.pallas_call(
        _proj_kernel,
        grid=(B, S // tm),
        in_specs=[pl.BlockSpec((1, tm, D), lambda b, i: (b, i, 0)),
                  pl.BlockSpec((1, tm, 1), lambda b, i: (b, i, 0)),
                  pl.BlockSpec((1, 1, D), lambda b, i: (b, 0, 0)),
                  pl.BlockSpec((1, 1, D), lambda b, i: (b, 0, 0)),
                  const((D, 3 * FOX_W)), const((FOX_HEADS * V_ROWS, D)), const((FOX_HEADS * V_ROWS, 1)),
                  const((D, 3 * DIL_W)), const((D, LANES)),
                  const((1, LANES)), const((1, LANES)), const((1, LANES))],
        out_specs=[pl.BlockSpec((1, tm, 3 * FOX_W), lambda b, i: (b, i, 0)),
                   pl.BlockSpec((1, FOX_HEADS * V_ROWS, tm), lambda b, i: (b, 0, i)),
                   pl.BlockSpec((1, tm, 3 * DIL_W), lambda b, i: (b, i, 0)),
                   pl.BlockSpec((1, tm, LANES), lambda b, i: (b, i, 0))],
        out_shape=[jax.ShapeDtypeStruct((B, S, 3 * FOX_W), BF16),
                   jax.ShapeDtypeStruct((B, FOX_HEADS * V_ROWS, S), BF16),
                   jax.ShapeDtypeStruct((B, S, 3 * DIL_W), F32),
                   jax.ShapeDtypeStruct((B, S, LANES), F32)],
        compiler_params=_params("arbitrary", "arbitrary"),
        name="proj",
    )(x, positions.reshape(B, S, 1), scale1.reshape(B, 1, D), shift1.reshape(B, 1, D),
      w_fox, w_vt, ones_row, w_dil, w_ff, b_f, invf, sgn)


def _cum_kernel(lf_ref, ck_ref, cq_ref):
    S = lf_ref.shape[1]
    r = lax.broadcasted_iota(jnp.int32, (LANES, LANES), 0)
    c = lax.broadcasted_iota(jnp.int32, (LANES, LANES), 1)
    tri = (c <= r).astype(F32)

    carry = jnp.zeros((1, LANES), F32)
    for j in range(S // LANES):
        rows = slice(j * LANES, (j + 1) * LANES)
        cum = jnp.dot(tri, lf_ref[0, rows, :], precision=HIGHEST, preferred_element_type=F32) + carry
        cum_t = cum.T
        for hp in range(FOX_HEADS // HEADS_PER_VREG):
            pair = slice(hp * HEADS_PER_VREG, (hp + 1) * HEADS_PER_VREG)
            ck_ref[0, hp, rows, :] = cum[:, pair]
            cq_ref[0, hp, :, rows] = cum_t[pair, :]
        carry = cum[LANES - 1:LANES, :]


def _cumulative_gate(log_f):
    B, S, _ = log_f.shape
    nhp = FOX_HEADS // HEADS_PER_VREG
    return pl.pallas_call(
        _cum_kernel,
        grid=(B,),
        in_specs=[pl.BlockSpec((1, S, LANES), lambda b: (b, 0, 0))],
        out_specs=[pl.BlockSpec((1, nhp, S, HEADS_PER_VREG), lambda b: (b, 0, 0, 0)),
                   pl.BlockSpec((1, nhp, HEADS_PER_VREG, S), lambda b: (b, 0, 0, 0))],
        out_shape=[jax.ShapeDtypeStruct((B, nhp, S, HEADS_PER_VREG), F32),
                   jax.ShapeDtypeStruct((B, nhp, HEADS_PER_VREG, S), F32)],
        compiler_params=_params("arbitrary"),
        name="cum",
    )(log_f)


def _fox_kernel(q_ref, k_ref, vt_ref, g_ref, ck_ref, cq_ref, o_ref, s_scr):
    S = q_ref.shape[1]
    T = FOX_TILE
    lane = lax.broadcasted_iota(jnp.int32, (T, LANES), 1)

    def q_block(i, _):
        qoff = pl.multiple_of(i * T, T)
        q = q_ref[0, pl.ds(qoff, T), :]
        qh = [jnp.where((lane // HEAD_DIM) == hh, q, jnp.zeros_like(q)) for hh in range(HEADS_PER_VREG)]
        cq = [cq_ref[0, 0, hh:hh + 1, pl.ds(qoff, T)] for hh in range(HEADS_PER_VREG)]

        def scores_into(buf, b):
            kb = k_ref[0, pl.ds(pl.multiple_of(b * T, T), T), :]
            for hh in range(HEADS_PER_VREG):
                s_scr[buf, hh] = lax.dot_general(kb, qh[hh], NT_DIMS, preferred_element_type=F32)

        def attend(buf, b, carry, masked):
            koff = pl.multiple_of(b * T, T)
            new = []
            for hh in range(HEADS_PER_VREG):
                m, acc = carry[hh]

                def scores(c0, rows):
                    s = s_scr[buf, hh, c0:c0 + rows, :] - ck_ref[0, 0, pl.ds(koff + c0, rows), hh:hh + 1]
                    if masked:
                        s = jnp.where(lax.broadcasted_iota(jnp.int32, (rows, T), 0) + c0
                                      <= lax.broadcasted_iota(jnp.int32, (rows, T), 1), s, NEG)
                    return s

                m_new = jnp.maximum(m, cq[hh] + jnp.max(scores(0, T), axis=0, keepdims=True))
                row = cq[hh] - m_new
                acc = jnp.exp2(m - m_new) * acc
                for c0 in range(0, T, MXU_DIM):
                    p = jnp.exp2((scores(c0, MXU_DIM) + row).astype(BF16))
                    vt = vt_ref[0, hh * V_ROWS:(hh + 1) * V_ROWS, pl.ds(koff + c0, MXU_DIM)]
                    acc = acc + jnp.dot(vt, p, preferred_element_type=F32)
                new.append((m_new, acc))
            return tuple(new)

        def pair(t, carry):
            scores_into(1, 2 * t + 1)
            carry = attend(0, 2 * t, carry, False)
            scores_into(0, 2 * t + 2)
            return attend(1, 2 * t + 1, carry, False)

        def odd_tail(carry):
            scores_into(1, i)
            return attend(1, i, attend(0, i - 1, carry, False), True)

        def even_tail(carry):
            return attend(0, i, carry, True)

        init = tuple((jnp.full((1, T), NEG, F32), jnp.zeros((V_ROWS, T), F32)) for _ in range(HEADS_PER_VREG))
        scores_into(0, 0)
        carry = lax.fori_loop(0, i // 2, pair, init)
        carry = lax.cond(i % 2 == 1, odd_tail, even_tail, carry)
        o_t = jnp.concatenate([acc[0:HEAD_DIM] * (1.0 / acc[HEAD_DIM:HEAD_DIM + 1]) for (_, acc) in carry], axis=0)
        gate = _sigmoid(g_ref[0, pl.ds(qoff, T), :].astype(F32))
        o_ref[0, pl.ds(qoff, T), :] = (o_t.T * gate).astype(BF16)
        return 0

    lax.fori_loop(0, S // T, q_block, 0)


def _fox_attention(pf, vt, ck, cq):
    B, S, _ = pf.shape
    nhp = FOX_W // LANES
    assert S % FOX_TILE == 0
    return pl.pallas_call(
        _fox_kernel,
        grid=(B, nhp),
        in_specs=[pl.BlockSpec((1, S, LANES), lambda b, h: (b, 0, h)),
                  pl.BlockSpec((1, S, LANES), lambda b, h: (b, 0, nhp + h)),
                  pl.BlockSpec((1, HEADS_PER_VREG * V_ROWS, S), lambda b, h: (b, h, 0)),
                  pl.BlockSpec((1, S, LANES), lambda b, h: (b, 0, 2 * nhp + h)),
                  pl.BlockSpec((1, 1, S, HEADS_PER_VREG), lambda b, h: (b, h, 0, 0)),
                  pl.BlockSpec((1, 1, HEADS_PER_VREG, S), lambda b, h: (b, h, 0, 0))],
        out_specs=pl.BlockSpec((1, S, LANES), lambda b, h: (b, 0, h)),
        out_shape=jax.ShapeDtypeStruct((B, S, FOX_W), BF16),
        scratch_shapes=[pltpu.VMEM((2, HEADS_PER_VREG, FOX_TILE, FOX_TILE), F32)],
        compiler_params=_params("arbitrary", "arbitrary"),
        name="fox",
    )(pf, pf, vt, pf, ck, cq)


def _dil_kernel(q_ref, k_ref, v_ref, o_ref, m_s, l_s, a_s, q4, k4, v4, m4, l4, a4):
    S = q_ref.shape[1]
    W = LANES
    lane = lax.broadcasted_iota(jnp.int32, (W, LANES), 1)
    head0 = lane < HEAD_DIM
    ri = lax.broadcasted_iota(jnp.int32, (2 * W, 2 * W), 0) % W
    ci = lax.broadcasted_iota(jnp.int32, (2 * W, 2 * W), 1)
    bias_rest = jnp.where((ci >= ri) & (ci <= ri + W), 0.0, NEG)
    bias_first = jnp.where((ci < W) & (ci <= ri), 0.0, NEG)
    ones = jnp.ones((2 * W, LANES), BF16)

    assert DILATED_PATTERNS == ((W, 1), (4 * W, 4), (16 * W, 16)) and S % (16 * W * 2) == 0
    assert (S // W) % DIL_UNROLL == 0
    L4 = S // 4

    def block_stats(q, kb, vb, first):
        qs = jnp.concatenate([jnp.where(head0, q, 0.0), jnp.where(head0, 0.0, q)], axis=0).astype(BF16)
        vb2 = jnp.concatenate([vb.astype(BF16), ones], axis=1)
        s = lax.dot_general(qs, kb.astype(BF16), NT_DIMS, preferred_element_type=F32)
        s = s + jnp.where(first, bias_first, bias_rest)
        m2 = jnp.max(s, axis=1, keepdims=True)
        out = jnp.dot(jnp.exp2((s - m2).astype(BF16)), vb2, preferred_element_type=F32)
        a_u = jnp.where(head0, out[0:W, 0:LANES], out[W:2 * W, 0:LANES])
        l_u = jnp.where(head0, out[0:W, LANES:2 * LANES], out[W:2 * W, LANES:2 * LANES])
        m_u = jnp.where(head0, m2[0:W], m2[W:2 * W])
        return m_u, l_u, a_u

    def merge(idx, stats):
        m_u, l_u, a_u = stats
        m_o = m4[idx, :]
        m_n = jnp.maximum(m_o, m_u)
        e_o = jnp.exp2(m_o - m_n)
        e_u = jnp.exp2(m_u - m_n)
        m4[idx, :] = m_n
        l4[idx, :] = l4[idx, :] * e_o + l_u * e_u
        a4[idx, :] = a4[idx, :] * e_o + a_u * e_u

    def for_blocks(unit):
        def group(t, _):
            for uu in range(DIL_UNROLL):
                unit(t * DIL_UNROLL + uu)
            return 0
        lax.fori_loop(0, S // W // DIL_UNROLL, group, 0)

    def unit1(u):
        qidx = pl.ds(pl.multiple_of(u * W, W), W)
        kidx = pl.ds(pl.multiple_of(jnp.maximum(u - 1, 0) * W, W), 2 * W)
        m_s[qidx, :], l_s[qidx, :], a_s[qidx, :] = block_stats(q_ref[0, qidx, :], k_ref[0, kidx, :],
                                                               v_ref[0, kidx, :], u == 0)
    for_blocks(unit1)

    def reorder(c, _):
        for r in range(4):
            src = pl.ds(c * (4 * W) + r, W, stride=4)
            dst = pl.ds(pl.multiple_of(r * L4 + c * W, W), W)
            q4[dst, :] = q_ref[0, src, :]
            k4[dst, :] = k_ref[0, src, :]
            v4[dst, :] = v_ref[0, src, :]
            m4[dst, :] = m_s[src, :]
            l4[dst, :] = l_s[src, :]
            a4[dst, :] = a_s[src, :]
        return 0
    lax.fori_loop(0, L4 // W, reorder, 0)

    nb4 = L4 // W

    def unit4(u):
        n = u % nb4
        base = (u // nb4) * L4
        qidx = pl.ds(pl.multiple_of(base + n * W, W), W)
        kidx = pl.ds(pl.multiple_of(base + jnp.maximum(n - 1, 0) * W, W), 2 * W)
        merge(qidx, block_stats(q4[qidx, :], k4[kidx, :], v4[kidx, :], n == 0))
    for_blocks(unit4)

    nb16 = S // (16 * W)

    def unit16(u):
        n = u % nb16
        r16 = u // nb16
        base = (r16 % 4) * L4 + r16 // 4
        qidx = pl.ds(base + n * (4 * W), W, stride=4)
        kidx = pl.ds(base + jnp.maximum(n - 1, 0) * (4 * W), 2 * W, stride=4)
        merge(qidx, block_stats(q4[qidx, :], k4[kidx, :], v4[kidx, :], n == 0))
    for_blocks(unit16)

    def finish(c, _):
        for r in range(4):
            src = pl.ds(pl.multiple_of(r * L4 + c * W, W), W)
            a_s[pl.ds(c * (4 * W) + r, W, stride=4), :] = a4[src, :] * (1.0 / l4[src, :])
        return 0
    lax.fori_loop(0, L4 // W, finish, 0)
    o_ref[0] = a_s[...].astype(BF16)


def _dilated_attention(pd):
    B, S, _ = pd.shape
    nhp = DIL_W // LANES
    return pl.pallas_call(
        _dil_kernel,
        grid=(B, nhp),
        in_specs=[pl.BlockSpec((1, S, LANES), lambda b, h: (b, 0, h)),
                  pl.BlockSpec((1, S, LANES), lambda b, h: (b, 0, nhp + h)),
                  pl.BlockSpec((1, S, LANES), lambda b, h: (b, 0, 2 * nhp + h))],
        out_specs=pl.BlockSpec((1, S, LANES), lambda b, h: (b, 0, h)),
        out_shape=jax.ShapeDtypeStruct((B, S, DIL_W), BF16),
        scratch_shapes=[pltpu.VMEM((S, LANES), F32)] * 9,
        compiler_params=_params("arbitrary", "arbitrary"),
        name="dil",
    )(pd, pd, pd)


def _out_kernel(of_ref, od_ref, x_ref, g1_ref, sc2_ref, sh2_ref, wo_ref, lng_ref, lnb_ref, wrh_ref, wrl_ref, br_ref,
                x1_ref, xs_ref, meta_ref, rt_ref, cnt_ref, h2_s, pos_v, pos_s, sem):
    tm = x_ref.shape[1]
    y = (jnp.dot(of_ref[0], wo_ref[0:FOX_W, :], preferred_element_type=F32)
         + jnp.dot(od_ref[0], wo_ref[FOX_W:FOX_W + DIL_W, :], preferred_element_type=F32))
    x1 = _layer_norm(DEEPNORM_ALPHA * x_ref[0] + g1_ref[0] * y, lng_ref[...], lnb_ref[...])
    x1_ref[0] = x1
    h2 = x1 * (1.0 + sc2_ref[0]) + sh2_ref[0]

    h_hi = h2.astype(BF16)
    h_lo = (h2 - h_hi.astype(F32)).astype(BF16)
    lg = (jnp.dot(h_hi, wrh_ref[...], preferred_element_type=F32)
          + jnp.dot(h_lo, wrh_ref[...], preferred_element_type=F32)
          + jnp.dot(h_hi, wrl_ref[...], preferred_element_type=F32)) + br_ref[...]
    lane = lax.broadcasted_iota(jnp.int32, (tm, LANES), 1)
    lanef = lane.astype(F32)
    far = float(LANES)
    is_g = lane < N_GROUPS
    gl = jnp.where(is_g, lg, NEG)
    gmax = jnp.max(gl, axis=1, keepdims=True)
    gidx = jnp.min(jnp.where(gl == gmax, lanef, far), axis=1, keepdims=True)
    p_g = 1.0 / jnp.sum(jnp.where(is_g, jnp.exp(gl - gmax), 0.0), axis=1, keepdims=True)
    e_lo = N_GROUPS + gidx * EXPERTS_PER_GROUP
    in_group = (lanef >= e_lo) & (lanef < e_lo + EXPERTS_PER_GROUP)
    el = jnp.where(in_group, lg, NEG)
    v1 = jnp.max(el, axis=1, keepdims=True)
    i1 = jnp.min(jnp.where(in_group & (el == v1), lanef, far), axis=1, keepdims=True)
    rest = in_group & (lanef != i1)
    el2 = jnp.where(rest, lg, NEG)
    v2 = jnp.max(el2, axis=1, keepdims=True)
    i2 = jnp.min(jnp.where(rest & (el2 == v2), lanef, far), axis=1, keepdims=True)
    e21 = jnp.exp(v2 - v1)
    w1 = p_g / (1.0 + e21)
    w2 = p_g * e21 / (1.0 + e21)

    pick1 = lanef == i1
    pick2 = lanef == i2
    onehot = jnp.where(pick1 | pick2, 1.0, 0.0)
    rr = lax.broadcasted_iota(jnp.int32, (tm, tm), 0)
    cc = lax.broadcasted_iota(jnp.int32, (tm, tm), 1)
    strict_lower = jnp.where(cc < rr, 1.0, 0.0).astype(BF16)
    before = jnp.dot(strict_lower, onehot.astype(BF16), preferred_element_type=F32)
    cnt = jnp.broadcast_to(jnp.sum(onehot, axis=0, keepdims=True), (SUBLANES, LANES))
    lane8 = lax.broadcasted_iota(jnp.int32, (SUBLANES, LANES), 1)
    incl = cnt
    for sh in (1, 2, 4, 8, 16, 32, 64):
        incl = incl + jnp.where(lane8 >= sh, pltpu.roll(incl, sh, 1), 0.0)
    start = before + (incl - cnt)[0:1]
    d1 = jnp.sum(jnp.where(pick1, start, 0.0), axis=1, keepdims=True)
    d2 = jnp.sum(jnp.where(pick2, start, 0.0), axis=1, keepdims=True)
    cnt_ref[...] = cnt

    packed = jnp.zeros((tm, LANES), F32)
    for k, val in enumerate((d1, d2, w1, w2)):
        packed = jnp.where(lane == k, val, packed)
    rt_ref[0] = packed
    meta = packed.T[0:SUBLANES, :]
    meta_ref[0] = meta
    pos_v[...] = (meta * SUBLANES).astype(jnp.int32)
    to_smem = pltpu.make_async_copy(pos_v, pos_s, sem)
    to_smem.start()
    for s in range(SUBLANES):
        h2_s[pl.ds(s, tm, stride=SUBLANES), :] = h2[:, s * LANES:(s + 1) * LANES]
    to_smem.wait()

    def place(t8, _):
        for u in range(SUBLANES):
            t = t8 * SUBLANES + u
            row = h2_s[pl.ds(pl.multiple_of(t * SUBLANES, SUBLANES), SUBLANES), :]
            for k in range(TOP_K):
                xs_ref[pl.ds(pl.multiple_of(pos_s[k, t], SUBLANES), SUBLANES), :] = row
        return 0

    lax.fori_loop(0, tm // SUBLANES, place, 0)


def _out_and_route(of, od, x, gate1, scale2, shift2, w_out, ln_g, ln_b, w_rg, b_rg, w_re, b_re, tm):
    B, S, D = x.shape
    n_r = N_GROUPS + N_EXPERTS
    w_r = jnp.pad(jnp.concatenate([w_rg, w_re], axis=1), ((0, 0), (0, LANES - n_r)))
    b_r = jnp.pad(jnp.concatenate([b_rg, b_re]), (0, LANES - n_r)).reshape(1, LANES)
    w_r_hi = w_r.astype(BF16)
    w_r_lo = (w_r - w_r_hi.astype(F32)).astype(BF16)
    nt = S // tm
    tile = lambda w: pl.BlockSpec((1, tm, w), lambda b, i: (b, i, 0))
    per_batch = pl.BlockSpec((1, 1, D), lambda b, i: (b, 0, 0))
    const = lambda shape: pl.BlockSpec(shape, lambda b, i: (0,) * len(shape))
    return pl.pallas_call(
        _out_kernel,
        grid=(B, S // tm),
        in_specs=[tile(FOX_W), tile(DIL_W), tile(D), per_batch, per_batch, per_batch,
                  const((D, D)), const((1, D)), const((1, D)), const((D, LANES)), const((D, LANES)),
                  const((1, LANES))],
        out_specs=[tile(D),
                   pl.BlockSpec((TOP_K * tm * SUBLANES, LANES), lambda b, i: (b * nt + i, 0)),
                   pl.BlockSpec((1, SUBLANES, tm), lambda b, i: (b * nt + i, 0, 0)),
                   tile(LANES),
                   pl.BlockSpec((SUBLANES, LANES), lambda b, i: (b * nt + i, 0))],
        out_shape=[jax.ShapeDtypeStruct((B, S, D), F32),
                   jax.ShapeDtypeStruct((TOP_K * B * S * SUBLANES, LANES), F32),
                   jax.ShapeDtypeStruct((B * nt, SUBLANES, tm), F32),
                   jax.ShapeDtypeStruct((B, S, LANES), F32),
                   jax.ShapeDtypeStruct((B * nt * SUBLANES, LANES), F32)],
        scratch_shapes=[pltpu.VMEM((tm * SUBLANES, LANES), F32), pltpu.VMEM((SUBLANES, tm), jnp.int32),
                        pltpu.SMEM((SUBLANES, tm), jnp.int32), pltpu.SemaphoreType.DMA(())],
        compiler_params=_params("arbitrary", "arbitrary"),
        name="out",
    )(of, od, x, gate1.reshape(B, 1, D), scale2.reshape(B, 1, D), shift2.reshape(B, 1, D),
      w_out.astype(BF16), ln_g.reshape(1, D), ln_b.reshape(1, D), w_r_hi, w_r_lo, b_r)


RUN_BITS = MOE_BLOCK.bit_length()
RUN_SMALL_BITS = 5
MAX_RUNS = 80
ST_EXPERT, ST_TILE, ST_OFF, ST_DONE = range(4)


def _copy_rows(src_ref, src_row, dst_ref, dst_row, n, sem, wait):
    def pieces(bits, pos):
        for bit in bits:
            size = 1 << bit

            @pl.when((n & size) != 0)
            def _(pos=pos, size=size):
                cp = pltpu.make_async_copy(
                    src_ref.at[pl.ds(pl.multiple_of((src_row + pos) * SUBLANES, SUBLANES), size * SUBLANES), :],
                    dst_ref.at[pl.ds(pl.multiple_of((dst_row + pos) * SUBLANES, SUBLANES), size * SUBLANES), :],
                    sem)
                if wait:
                    cp.wait()
                else:
                    cp.start()

            pos = pos + (n & size)

    small = 1 << RUN_SMALL_BITS
    pl.when(n >= small)(lambda: pieces(reversed(range(RUN_SMALL_BITS, RUN_BITS)), 0))
    pieces(reversed(range(RUN_SMALL_BITS)), n & ~(small - 1))


def _moe_kernel(blk_e_ref, nused_ref, cnt_ref, off_ref, tot_ref, x_ref, wg_ref, wu_ref, wd_ref, y_ref,
                xbuf0, xbuf1, ybuf0, ybuf1, mid_s, st, runs, sruns, gsem, ssem):
    del blk_e_ref
    i = pl.program_id(0)
    nused = nused_ref[0]
    D = xbuf0.shape[0] // MOE_BLOCK * LANES
    nk = D // MXU_DIM
    n_tiles = cnt_ref.shape[0] // N_EXPERTS
    rows_per_tile = x_ref.shape[0] // SUBLANES // n_tiles

    def gather_next(xdst, s):
        def next_expert(c):
            return c[0] + 1, 0, 0, 0

        def exhausted(c):
            return (c[0] < N_EXPERTS) & (c[3] >= tot_ref[jnp.minimum(c[0], N_EXPERTS - 1)])

        e, tile, off, done = lax.while_loop(
            exhausted, next_expert, (st[ST_EXPERT], st[ST_TILE], st[ST_OFF], st[ST_DONE]))
        ec = jnp.minimum(e, N_EXPERTS - 1)
        total = jnp.where(e < N_EXPERTS, tot_ref[ec], 0)

        def more(c):
            return (c[0] < MOE_BLOCK) & (c[4] < total)

        def take_run(c):
            filled, k, tile, off, done = c
            run = cnt_ref[tile * N_EXPERTS + ec]
            take = jnp.minimum(run - off, MOE_BLOCK - filled)
            src = tile * rows_per_tile + off_ref[tile * N_EXPERTS + ec] + off
            _copy_rows(x_ref, src, xdst, filled, take, gsem.at[s], wait=False)
            base = (s * MAX_RUNS + k) * 3
            runs[base] = src
            runs[base + 1] = filled
            runs[base + 2] = take
            run_done = off + take >= run
            return (filled + take, k + jnp.where(take > 0, 1, 0), jnp.where(run_done, tile + 1, tile),
                    jnp.where(run_done, 0, off + take), done + take)

        filled, k, tile, off, done = lax.while_loop(more, take_run, (0, 0, tile, off, done))
        _copy_rows(x_ref, 0, xdst, filled, jnp.where(total > 0, MOE_BLOCK - filled, 0), gsem.at[s], wait=False)
        runs[(2 * MAX_RUNS + s) * 3] = k
        st[ST_EXPERT] = e
        st[ST_TILE] = tile
        st[ST_OFF] = off
        st[ST_DONE] = done

    def wait_gather(xdst, s):
        pltpu.make_async_copy(x_ref.at[pl.ds(0, MOE_BLOCK * SUBLANES), :], xdst, gsem.at[s]).wait()

    def scatter_block(ysrc, s):
        def one(k, rows):
            base = (s * MAX_RUNS + k) * 3
            _copy_rows(ysrc, runs[base + 1], y_ref, runs[base], runs[base + 2], ssem.at[s], wait=False)
            return rows + runs[base + 2]

        sruns[s] = lax.fori_loop(0, runs[(2 * MAX_RUNS + s) * 3], one, 0)

    def wait_scatter(ysrc, s):
        n = pl.multiple_of(sruns[s] * SUBLANES, SUBLANES)
        pltpu.make_async_copy(ysrc.at[pl.ds(0, n), :], y_ref.at[pl.ds(0, n), :], ssem.at[s]).wait()

    def compute(xcur, ycur):
        xk = [jnp.concatenate([xcur[pl.ds(2 * j, MOE_BLOCK, stride=SUBLANES), :],
                               xcur[pl.ds(2 * j + 1, MOE_BLOCK, stride=SUBLANES), :]], axis=1).astype(BF16)
              for j in range(nk)]
        for c in range(D_EXPERT // MXU_DIM):
            cols = slice(c * MXU_DIM, (c + 1) * MXU_DIM)
            g = sum(jnp.dot(xk[j], wg_ref[0, MXU_DIM * j:MXU_DIM * (j + 1), cols].astype(BF16),
                            preferred_element_type=F32) for j in range(nk))
            u = sum(jnp.dot(xk[j], wu_ref[0, MXU_DIM * j:MXU_DIM * (j + 1), cols].astype(BF16),
                            preferred_element_type=F32) for j in range(nk))
            mid_s[:, cols] = (g * _sigmoid(g) * u).astype(BF16)
        mid = mid_s[...]
        for c in range(D // MXU_DIM):
            y = jnp.dot(mid, wd_ref[0, :, c * MXU_DIM:(c + 1) * MXU_DIM].astype(BF16), preferred_element_type=F32)
            ycur[pl.ds(2 * c, MOE_BLOCK, stride=SUBLANES), :] = y[:, 0:LANES]
            ycur[pl.ds(2 * c + 1, MOE_BLOCK, stride=SUBLANES), :] = y[:, LANES:2 * LANES]

    def step(s, xcur, xnxt, ycur, yprv):
        o = 1 - s
        wait_gather(xcur, s)

        @pl.when(i + 1 < nused)
        def _():
            gather_next(xnxt, o)

        @pl.when(i >= 2)
        def _():
            wait_scatter(ycur, s)

        compute(xcur, ycur)
        scatter_block(ycur, s)

        @pl.when(i + 1 == nused)
        def _():
            @pl.when(i >= 1)
            def _():
                wait_scatter(yprv, o)

            wait_scatter(ycur, s)

    @pl.when(i == 0)
    def _():
        for f in range(4):
            st[f] = 0
        gather_next(xbuf0, 0)

    @pl.when((i < nused) & (i % 2 == 0))
    def _():
        step(0, xbuf0, xbuf1, ybuf0, ybuf1)

    @pl.when((i < nused) & (i % 2 == 1))
    def _():
        step(1, xbuf1, xbuf0, ybuf1, ybuf0)


def _expert_mlp(x_sorted, cnt, off, tot, blk_e, nused, w_gate, w_up, w_down):
    D = w_gate.shape[1]
    nblk = blk_e.shape[0]
    assert cnt.shape[0] // N_EXPERTS + 2 <= MAX_RUNS
    wmap = lambda i, be, nu, c, o, t: (be[i], 0, 0)
    return pl.pallas_call(
        _moe_kernel,
        grid_spec=pltpu.PrefetchScalarGridSpec(
            num_scalar_prefetch=5,
            grid=(nblk,),
            in_specs=[pl.BlockSpec(memory_space=pl.ANY),
                      pl.BlockSpec((1, D, D_EXPERT), wmap),
                      pl.BlockSpec((1, D, D_EXPERT), wmap),
                      pl.BlockSpec((1, D_EXPERT, D), wmap)],
            out_specs=pl.BlockSpec(memory_space=pl.ANY),
            scratch_shapes=[pltpu.VMEM((MOE_BLOCK * SUBLANES, LANES), F32)] * 4
            + [pltpu.VMEM((MOE_BLOCK, D_EXPERT), BF16),
               pltpu.SMEM((4,), jnp.int32),
               pltpu.SMEM(((2 * MAX_RUNS + 2) * 3,), jnp.int32), pltpu.SMEM((2,), jnp.int32),
               pltpu.SemaphoreType.DMA((2,)), pltpu.SemaphoreType.DMA((2,))]),
        out_shape=jax.ShapeDtypeStruct(x_sorted.shape, F32),
        compiler_params=_params("arbitrary"),
        name="moe",
    )(blk_e, nused, cnt, off, tot, x_sorted, w_gate, w_up, w_down)


def _final_kernel(pos_ref, rt_ref, y_ref, x1_ref, g2_ref, lng_ref, lnb_ref, o_ref, pick_s):
    tm = x1_ref.shape[1]

    def unsort(t8, _):
        for u in range(SUBLANES):
            t = t8 * SUBLANES + u
            for k in range(TOP_K):
                pick_s[k, pl.ds(pl.multiple_of(t * SUBLANES, SUBLANES), SUBLANES), :] = (
                    y_ref[pl.ds(pl.multiple_of(pos_ref[k, t], SUBLANES), SUBLANES), :])
        return 0

    lax.fori_loop(0, tm // SUBLANES, unsort, 0)
    wts = rt_ref[0]
    y = jnp.concatenate([wts[:, TOP_K:TOP_K + 1] * pick_s[0, pl.ds(s, tm, stride=SUBLANES), :]
                         + wts[:, TOP_K + 1:TOP_K + 2] * pick_s[1, pl.ds(s, tm, stride=SUBLANES), :]
                         for s in range(SUBLANES)], axis=1)
    o_ref[0] = _layer_norm(DEEPNORM_ALPHA * x1_ref[0] + g2_ref[0] * y, lng_ref[...], lnb_ref[...])


def _combine_and_norm(y_sorted, meta, route, x1, gate2, ln_g, ln_b, tm):
    B, S, D = x1.shape
    nt = S // tm
    pos = (meta * SUBLANES).astype(jnp.int32).reshape(B * nt * SUBLANES, tm)
    return pl.pallas_call(
        _final_kernel,
        grid=(B, nt),
        in_specs=[pl.BlockSpec((SUBLANES, tm), lambda b, i: (b * nt + i, 0), memory_space=pltpu.SMEM),
                  pl.BlockSpec((1, tm, LANES), lambda b, i: (b, i, 0)),
                  pl.BlockSpec((TOP_K * tm * SUBLANES, LANES), lambda b, i: (b * nt + i, 0)),
                  pl.BlockSpec((1, tm, D), lambda b, i: (b, i, 0)),
                  pl.BlockSpec((1, 1, D), lambda b, i: (b, 0, 0)),
                  pl.BlockSpec((1, D), lambda b, i: (0, 0)),
                  pl.BlockSpec((1, D), lambda b, i: (0, 0))],
        out_specs=pl.BlockSpec((1, tm, D), lambda b, i: (b, i, 0)),
        out_shape=jax.ShapeDtypeStruct((B, S, D), F32),
        scratch_shapes=[pltpu.VMEM((TOP_K, tm * SUBLANES, LANES), F32)],
        compiler_params=_params("arbitrary", "arbitrary"),
        name="final",
    )(pos, route, y_sorted, x1, gate2.reshape(B, 1, D), ln_g.reshape(1, D), ln_b.reshape(1, D))


def kernel(x, c, positions, w_ada, b_ada, w_in, b_forget, w_out, ln1_g, ln1_b, w_router_group, b_router_group,
           w_router_expert, b_router_expert, w_up, w_gate, w_down, ln2_g, ln2_b):
    B, S, D = x.shape
    N = B * S
    assert D == FOX_W + DIL_W and S % 2048 == 0
    mod = _modulation(c, w_ada, b_ada)
    shift1, scale1, gate1, shift2, scale2, gate2 = jnp.split(mod, 6, axis=-1)

    pf, vt, pd, log_f = _projection(x, positions, scale1, shift1, w_in, b_forget, tm=512)
    ck, cq = _cumulative_gate(log_f)
    of = _fox_attention(pf, vt, ck, cq)
    od = _dilated_attention(pd)

    tm = 512
    x1, x_sorted, meta, route, tile_cnt = _out_and_route(
        of, od, x, gate1, scale2, shift2, w_out, ln1_g, ln1_b,
        w_router_group, b_router_group, w_router_expert, b_router_expert, tm=tm)

    cnt = tile_cnt[::SUBLANES, N_GROUPS:N_GROUPS + N_EXPERTS].astype(jnp.int32)
    off = jnp.cumsum(cnt, axis=1) - cnt
    tot = jnp.sum(cnt, axis=0)
    pend = jnp.cumsum((tot + MOE_BLOCK - 1) // MOE_BLOCK * MOE_BLOCK)
    nblk = (N * TOP_K) // MOE_BLOCK + N_EXPERTS
    nused = pend[-1:] // MOE_BLOCK
    blk_start = jnp.arange(nblk, dtype=jnp.int32) * MOE_BLOCK
    blk_e = jnp.sum((pend[None, :] <= jnp.minimum(blk_start, pend[-1] - 1)[:, None]).astype(jnp.int32), axis=1)

    y_sorted = _expert_mlp(x_sorted, cnt.reshape(-1), off.reshape(-1), tot, blk_e, nused, w_gate, w_up, w_down)
    return _combine_and_norm(y_sorted, meta, route, x1, gate2, ln2_g, ln2_b, tm=tm)
```

```python
import jax
import jax.numpy as jnp
import numpy as np
from jax import lax
from jax.experimental import pallas as pl
from jax.experimental.pallas import tpu as pltpu

HEAD_DIM = 64
FOX_HEADS = 8
DIL_HEADS = 8
FOX_W = FOX_HEADS * HEAD_DIM
DIL_W = DIL_HEADS * HEAD_DIM
DILATED_PATTERNS = ((128, 1), (512, 4), (2048, 16))
ROPE_THETA = 500000.0
ROT_DIM = HEAD_DIM // 4
N_GROUPS = 4
EXPERTS_PER_GROUP = 8
N_EXPERTS = N_GROUPS * EXPERTS_PER_GROUP
TOP_K = 2
D_EXPERT = 512
MOE_BLOCK = 512
LN_EPS = 1e-5
NEG = -1e30
DEPTH = 1
DEEPNORM_ALPHA = (2 * DEPTH) ** 0.25
QK_SCALE = HEAD_DIM ** -0.5
LOG2E = 1.4426950408889634

LANES = 128
SUBLANES = 8
BF16_SUBLANES = 16
MXU_DIM = 256
HEADS_PER_VREG = LANES // HEAD_DIM
V_ROWS = HEAD_DIM + BF16_SUBLANES
FOX_TILE = 512
DIL_UNROLL = 32
VMEM_LIMIT = 56 * 1024 * 1024

F32 = jnp.float32
BF16 = jnp.bfloat16
HIGHEST = lax.Precision.HIGHEST
NT_DIMS = (((1,), (1,)), ((), ()))


def _params(*sem):
    return pltpu.CompilerParams(dimension_semantics=sem, vmem_limit_bytes=VMEM_LIMIT)


def _sigmoid(v):
    return 1.0 / (1.0 + jnp.exp(-v))


def _layer_norm(v, g, b):
    mu = jnp.mean(v, axis=-1, keepdims=True)
    d = v - mu
    var = jnp.mean(d * d, axis=-1, keepdims=True)
    return d * lax.rsqrt(var + LN_EPS) * g + b


def _mod_kernel(c_ref, w_ref, b_ref, o_ref):
    c = c_ref[...]
    o_ref[...] = jnp.dot(c * _sigmoid(c), w_ref[...], precision=HIGHEST,
                         preferred_element_type=F32) + b_ref[...]


def _modulation(c, w_ada, b_ada):
    B, D = c.shape
    cols = w_ada.shape[1]
    tn = 1024
    return pl.pallas_call(
        _mod_kernel,
        grid=(cols // tn,),
        in_specs=[pl.BlockSpec((B, D), lambda j: (0, 0)),
                  pl.BlockSpec((D, tn), lambda j: (0, j)),
                  pl.BlockSpec((1, tn), lambda j: (0, j))],
        out_specs=pl.BlockSpec((B, tn), lambda j: (0, j)),
        out_shape=jax.ShapeDtypeStruct((B, cols), F32),
        compiler_params=_params("arbitrary"),
        name="mod",
    )(c, w_ada, b_ada.reshape(1, cols))


def _proj_kernel(x_ref, pos_ref, sc_ref, sh_ref, wf_ref, wvt_ref, one_ref, wd_ref, wff_ref, bf_ref, invf_ref,
                 sgn_ref, pf_ref, vt_ref, pd_ref, lf_ref):
    tm = x_ref.shape[1]
    h = (x_ref[0] * (1.0 + sc_ref[0]) + sh_ref[0]).astype(BF16)

    for ci in range(3):
        acc = jnp.dot(h, wf_ref[:, ci * FOX_W:(ci + 1) * FOX_W], preferred_element_type=F32)
        if ci == 0:
            acc = acc * (QK_SCALE * LOG2E)
        pf_ref[0, :, ci * FOX_W:(ci + 1) * FOX_W] = acc.astype(BF16)
    vt = lax.dot_general(wvt_ref[...], h, NT_DIMS, preferred_element_type=F32) + one_ref[...]
    vt_ref[0] = vt.astype(BF16)

    z = jnp.dot(h, wff_ref[...], preferred_element_type=F32) + bf_ref[...]
    lf_ref[0] = (jnp.minimum(z, 0.0) - jnp.log1p(jnp.exp(-jnp.abs(z)))) * LOG2E

    ang = pos_ref[0].astype(F32) * invf_ref[...]
    cs = jnp.cos(ang)
    sn = jnp.sin(ang) * sgn_ref[...]
    lane = lax.broadcasted_iota(jnp.int32, (tm, LANES), 1)
    first_half = (lane % HEAD_DIM) < (ROT_DIM // 2)
    for ci in range(3):
        acc = jnp.dot(h, wd_ref[:, ci * DIL_W:(ci + 1) * DIL_W], preferred_element_type=F32)
        if ci == 2:
            pd_ref[0, :, ci * DIL_W:(ci + 1) * DIL_W] = acc
            continue
        for j in range(DIL_W // LANES):
            t = acc[:, j * LANES:(j + 1) * LANES]
            partner = jnp.where(first_half, pltpu.roll(t, LANES - ROT_DIM // 2, 1),
                                pltpu.roll(t, ROT_DIM // 2, 1))
            r = t * cs + partner * sn
            if ci == 0:
                r = r * (QK_SCALE * LOG2E)
            pd_ref[0, :, ci * DIL_W + j * LANES:ci * DIL_W + (j + 1) * LANES] = r


def _projection(x, positions, scale1, shift1, w_in, b_forget, tm):
    B, S, D = x.shape
    o = np.cumsum((0, FOX_W, FOX_W, FOX_W, FOX_W, FOX_HEADS, DIL_W, DIL_W, DIL_W))
    w_fox = jnp.concatenate([w_in[:, o[0]:o[2]], w_in[:, o[3]:o[4]]], axis=1).astype(BF16)
    w_vt = jnp.pad(w_in[:, o[2]:o[3]].T.reshape(FOX_HEADS, HEAD_DIM, D),
                   ((0, 0), (0, V_ROWS - HEAD_DIM), (0, 0))).reshape(FOX_HEADS * V_ROWS, D).astype(BF16)
    ones_row = jnp.asarray((np.arange(FOX_HEADS * V_ROWS) % V_ROWS == HEAD_DIM).astype(np.float32)
                           ).reshape(FOX_HEADS * V_ROWS, 1)
    w_ff = jnp.pad(w_in[:, o[4]:o[5]], ((0, 0), (0, LANES - FOX_HEADS))).astype(BF16)
    w_dil = w_in[:, o[5]:o[8]].astype(BF16)
    b_f = jnp.pad(b_forget, (0, LANES - FOX_HEADS)).reshape(1, LANES)
    e = np.arange(LANES) % HEAD_DIM
    inv_freq = ROPE_THETA ** (-jnp.arange(0, ROT_DIM, 2, dtype=F32) / ROT_DIM)
    invf = jnp.where(e < ROT_DIM, jnp.tile(inv_freq, LANES // (ROT_DIM // 2)), 0.0).reshape(1, LANES)
    sgn = jnp.asarray(np.where(e < ROT_DIM // 2, -1.0, np.where(e < ROT_DIM, 1.0, 0.0)), F32).reshape(1, LANES)
    const = lambda shape: pl.BlockSpec(shape, lambda b, i: (0,) * len(shape))
    return pl.pallas_call(
        _proj_kernel,
        grid=(B, S // tm),
        in_specs=[pl.BlockSpec((1, tm, D), lambda b, i: (b, i, 0)),
                  pl.BlockSpec((1, tm, 1), lambda b, i: (b, i, 0)),
                  pl.BlockSpec((1, 1, D), lambda b, i: (b, 0, 0)),
                  pl.BlockSpec((1, 1, D), lambda b, i: (b, 0, 0)),
                  const((D, 3 * FOX_W)), const((FOX_HEADS * V_ROWS, D)), const((FOX_HEADS * V_ROWS, 1)),
                  const((D, 3 * DIL_W)), const((D, LANES)),
                  const((1, LANES)), const((1, LANES)), const((1, LANES))],
        out_specs=[pl.BlockSpec((1, tm, 3 * FOX_W), lambda b, i: (b, i, 0)),
                   pl.BlockSpec((1, FOX_HEADS * V_ROWS, tm), lambda b, i: (b, 0, i)),
                   pl.BlockSpec((1, tm, 3 * DIL_W), lambda b, i: (b, i, 0)),
                   pl.BlockSpec((1, tm, LANES), lambda b, i: (b, i, 0))],
        out_shape=[jax.ShapeDtypeStruct((B, S, 3 * FOX_W), BF16),
                   jax.ShapeDtypeStruct((B, FOX_HEADS * V_ROWS, S), BF16),
                   jax.ShapeDtypeStruct((B, S, 3 * DIL_W), F32),
                   jax.ShapeDtypeStruct((B, S, LANES), F32)],
        compiler_params=_params("arbitrary", "arbitrary"),
        name="proj",
    )(x, positions.reshape(B, S, 1), scale1.reshape(B, 1, D), shift1.reshape(B, 1, D),
      w_fox, w_vt, ones_row, w_dil, w_ff, b_f, invf, sgn)


def _cum_kernel(lf_ref, ck_ref, cq_ref):
    S = lf_ref.shape[1]
    r = lax.broadcasted_iota(jnp.int32, (LANES, LANES), 0)
    c = lax.broadcasted_iota(jnp.int32, (LANES, LANES), 1)
    tri = (c <= r).astype(F32)

    carry = jnp.zeros((1, LANES), F32)
    for j in range(S // LANES):
        rows = slice(j * LANES, (j + 1) * LANES)
        cum = jnp.dot(tri, lf_ref[0, rows, :], precision=HIGHEST, preferred_element_type=F32) + carry
        cum_t = cum.T
        for hp in range(FOX_HEADS // HEADS_PER_VREG):
            pair = slice(hp * HEADS_PER_VREG, (hp + 1) * HEADS_PER_VREG)
            ck_ref[0, hp, rows, :] = cum[:, pair]
            cq_ref[0, hp, :, rows] = cum_t[pair, :]
        carry = cum[LANES - 1:LANES, :]


def _cumulative_gate(log_f):
    B, S, _ = log_f.shape
    nhp = FOX_HEADS // HEADS_PER_VREG
    return pl.pallas_call(
        _cum_kernel,
        grid=(B,),
        in_specs=[pl.BlockSpec((1, S, LANES), lambda b: (b, 0, 0))],
        out_specs=[pl.BlockSpec((1, nhp, S, HEADS_PER_VREG), lambda b: (b, 0, 0, 0)),
                   pl.BlockSpec((1, nhp, HEADS_PER_VREG, S), lambda b: (b, 0, 0, 0))],
        out_shape=[jax.ShapeDtypeStruct((B, nhp, S, HEADS_PER_VREG), F32),
                   jax.ShapeDtypeStruct((B, nhp, HEADS_PER_VREG, S), F32)],
        compiler_params=_params("arbitrary"),
        name="cum",
    )(log_f)


def _fox_kernel(q_ref, k_ref, vt_ref, g_ref, ck_ref, cq_ref, o_ref, s_scr):
    S = q_ref.shape[1]
    T = FOX_TILE
    lane = lax.broadcasted_iota(jnp.int32, (T, LANES), 1)

    def q_block(i, _):
        qoff = pl.multiple_of(i * T, T)
        q = q_ref[0, pl.ds(qoff, T), :]
        qh = [jnp.where((lane // HEAD_DIM) == hh, q, jnp.zeros_like(q)) for hh in range(HEADS_PER_VREG)]
        cq = [cq_ref[0, 0, hh:hh + 1, pl.ds(qoff, T)] for hh in range(HEADS_PER_VREG)]

        def scores_into(buf, b):
            kb = k_ref[0, pl.ds(pl.multiple_of(b * T, T), T), :]
            for hh in range(HEADS_PER_VREG):
                s_scr[buf, hh] = lax.dot_general(kb, qh[hh], NT_DIMS, preferred_element_type=F32)

        def attend(buf, b, carry, masked):
            koff = pl.multiple_of(b * T, T)
            new = []
            for hh in range(HEADS_PER_VREG):
                m, acc = carry[hh]

                def scores(c0, rows):
                    s = s_scr[buf, hh, c0:c0 + rows, :] - ck_ref[0, 0, pl.ds(koff + c0, rows), hh:hh + 1]
                    if masked:
                        s = jnp.where(lax.broadcasted_iota(jnp.int32, (rows, T), 0) + c0
                                      <= lax.broadcasted_iota(jnp.int32, (rows, T), 1), s, NEG)
                    return s

                m_new = jnp.maximum(m, cq[hh] + jnp.max(scores(0, T), axis=0, keepdims=True))
                row = cq[hh] - m_new
                acc = jnp.exp2(m - m_new) * acc
                for c0 in range(0, T, MXU_DIM):
                    p = jnp.exp2((scores(c0, MXU_DIM) + row).astype(BF16))
                    vt = vt_ref[0, hh * V_ROWS:(hh + 1) * V_ROWS, pl.ds(koff + c0, MXU_DIM)]
                    acc = acc + jnp.dot(vt, p, preferred_element_type=F32)
                new.append((m_new, acc))
            return tuple(new)

        def pair(t, carry):
            scores_into(1, 2 * t + 1)
            carry = attend(0, 2 * t, carry, False)
            scores_into(0, 2 * t + 2)
            return attend(1, 2 * t + 1, carry, False)

        def odd_tail(carry):
            scores_into(1, i)
            return attend(1, i, attend(0, i - 1, carry, False), True)

        def even_tail(carry):
            return attend(0, i, carry, True)

        init = tuple((jnp.full((1, T), NEG, F32), jnp.zeros((V_ROWS, T), F32)) for _ in range(HEADS_PER_VREG))
        scores_into(0, 0)
        carry = lax.fori_loop(0, i // 2, pair, init)
        carry = lax.cond(i % 2 == 1, odd_tail, even_tail, carry)
        o_t = jnp.concatenate([acc[0:HEAD_DIM] * (1.0 / acc[HEAD_DIM:HEAD_DIM + 1]) for (_, acc) in carry], axis=0)
        gate = _sigmoid(g_ref[0, pl.ds(qoff, T), :].astype(F32))
        o_ref[0, pl.ds(qoff, T), :] = (o_t.T * gate).astype(BF16)
        return 0

    lax.fori_loop(0, S // T, q_block, 0)


def _fox_attention(pf, vt, ck, cq):
    B, S, _ = pf.shape
    nhp = FOX_W // LANES
    assert S % FOX_TILE == 0
    return pl.pallas_call(
        _fox_kernel,
        grid=(B, nhp),
        in_specs=[pl.BlockSpec((1, S, LANES), lambda b, h: (b, 0, h)),
                  pl.BlockSpec((1, S, LANES), lambda b, h: (b, 0, nhp + h)),
                  pl.BlockSpec((1, HEADS_PER_VREG * V_ROWS, S), lambda b, h: (b, h, 0)),
                  pl.BlockSpec((1, S, LANES), lambda b, h: (b, 0, 2 * nhp + h)),
                  pl.BlockSpec((1, 1, S, HEADS_PER_VREG), lambda b, h: (b, h, 0, 0)),
                  pl.BlockSpec((1, 1, HEADS_PER_VREG, S), lambda b, h: (b, h, 0, 0))],
        out_specs=pl.BlockSpec((1, S, LANES), lambda b, h: (b, 0, h)),
        out_shape=jax.ShapeDtypeStruct((B, S, FOX_W), BF16),
        scratch_shapes=[pltpu.VMEM((2, HEADS_PER_VREG, FOX_TILE, FOX_TILE), F32)],
        compiler_params=_params("arbitrary", "arbitrary"),
        name="fox",
    )(pf, pf, vt, pf, ck, cq)


def _dil_kernel(q_ref, k_ref, v_ref, o_ref, m_s, l_s, a_s, q4, k4, v4, m4, l4, a4):
    S = q_ref.shape[1]
    W = LANES
    lane = lax.broadcasted_iota(jnp.int32, (W, LANES), 1)
    head0 = lane < HEAD_DIM
    ri = lax.broadcasted_iota(jnp.int32, (2 * W, 2 * W), 0) % W
    ci = lax.broadcasted_iota(jnp.int32, (2 * W, 2 * W), 1)
    bias_rest = jnp.where((ci >= ri) & (ci <= ri + W), 0.0, NEG)
    bias_first = jnp.where((ci < W) & (ci <= ri), 0.0, NEG)
    ones = jnp.ones((2 * W, LANES), BF16)

    assert DILATED_PATTERNS == ((W, 1), (4 * W, 4), (16 * W, 16)) and S % (16 * W * 2) == 0
    assert (S // W) % DIL_UNROLL == 0
    L4 = S // 4

    def block_stats(q, kb, vb, first):
        qs = jnp.concatenate([jnp.where(head0, q, 0.0), jnp.where(head0, 0.0, q)], axis=0).astype(BF16)
        vb2 = jnp.concatenate([vb.astype(BF16), ones], axis=1)
        s = lax.dot_general(qs, kb.astype(BF16), NT_DIMS, preferred_element_type=F32)
        s = s + jnp.where(first, bias_first, bias_rest)
        m2 = jnp.max(s, axis=1, keepdims=True)
        out = jnp.dot(jnp.exp2((s - m2).astype(BF16)), vb2, preferred_element_type=F32)
        a_u = jnp.where(head0, out[0:W, 0:LANES], out[W:2 * W, 0:LANES])
        l_u = jnp.where(head0, out[0:W, LANES:2 * LANES], out[W:2 * W, LANES:2 * LANES])
        m_u = jnp.where(head0, m2[0:W], m2[W:2 * W])
        return m_u, l_u, a_u

    def merge(idx, stats):
        m_u, l_u, a_u = stats
        m_o = m4[idx, :]
        m_n = jnp.maximum(m_o, m_u)
        e_o = jnp.exp2(m_o - m_n)
        e_u = jnp.exp2(m_u - m_n)
        m4[idx, :] = m_n
        l4[idx, :] = l4[idx, :] * e_o + l_u * e_u
        a4[idx, :] = a4[idx, :] * e_o + a_u * e_u

    def for_blocks(unit):
        def group(t, _):
            for uu in range(DIL_UNROLL):
                unit(t * DIL_UNROLL + uu)
            return 0
        lax.fori_loop(0, S // W // DIL_UNROLL, group, 0)

    def unit1(u):
        qidx = pl.ds(pl.multiple_of(u * W, W), W)
        kidx = pl.ds(pl.multiple_of(jnp.maximum(u - 1, 0) * W, W), 2 * W)
        m_s[qidx, :], l_s[qidx, :], a_s[qidx, :] = block_stats(q_ref[0, qidx, :], k_ref[0, kidx, :],
                                                               v_ref[0, kidx, :], u == 0)
    for_blocks(unit1)

    def reorder(c, _):
        for r in range(4):
            src = pl.ds(c * (4 * W) + r, W, stride=4)
            dst = pl.ds(pl.multiple_of(r * L4 + c * W, W), W)
            q4[dst, :] = q_ref[0, src, :]
            k4[dst, :] = k_ref[0, src, :]
            v4[dst, :] = v_ref[0, src, :]
            m4[dst, :] = m_s[src, :]
            l4[dst, :] = l_s[src, :]
            a4[dst, :] = a_s[src, :]
        return 0
    lax.fori_loop(0, L4 // W, reorder, 0)

    nb4 = L4 // W

    def unit4(u):
        n = u % nb4
        base = (u // nb4) * L4
        qidx = pl.ds(pl.multiple_of(base + n * W, W), W)
        kidx = pl.ds(pl.multiple_of(base + jnp.maximum(n - 1, 0) * W, W), 2 * W)
        merge(qidx, block_stats(q4[qidx, :], k4[kidx, :], v4[kidx, :], n == 0))
    for_blocks(unit4)

    nb16 = S // (16 * W)

    def unit16(u):
        n = u % nb16
        r16 = u // nb16
        base = (r16 % 4) * L4 + r16 // 4
        qidx = pl.ds(base + n * (4 * W), W, stride=4)
        kidx = pl.ds(base + jnp.maximum(n - 1, 0) * (4 * W), 2 * W, stride=4)
        merge(qidx, block_stats(q4[qidx, :], k4[kidx, :], v4[kidx, :], n == 0))
    for_blocks(unit16)

    def finish(c, _):
        for r in range(4):
            src = pl.ds(pl.multiple_of(r * L4 + c * W, W), W)
            a_s[pl.ds(c * (4 * W) + r, W, stride=4), :] = a4[src, :] * (1.0 / l4[src, :])
        return 0
    lax.fori_loop(0, L4 // W, finish, 0)
    o_ref[0] = a_s[...].astype(BF16)


def _dilated_attention(pd):
    B, S, _ = pd.shape
    nhp = DIL_W // LANES
    return pl.pallas_call(
        _dil_kernel,
        grid=(B, nhp),
        in_specs=[pl.BlockSpec((1, S, LANES), lambda b, h: (b, 0, h)),
                  pl.BlockSpec((1, S, LANES), lambda b, h: (b, 0, nhp + h)),
                  pl.BlockSpec((1, S, LANES), lambda b, h: (b, 0, 2 * nhp + h))],
        out_specs=pl.BlockSpec((1, S, LANES), lambda b, h: (b, 0, h)),
        out_shape=jax.ShapeDtypeStruct((B, S, DIL_W), BF16),
        scratch_shapes=[pltpu.VMEM((S, LANES), F32)] * 9,
        compiler_params=_params("arbitrary", "arbitrary"),
        name="dil",
    )(pd, pd, pd)


def _out_kernel(of_ref, od_ref, x_ref, g1_ref, sc2_ref, sh2_ref, wo_ref, lng_ref, lnb_ref, wrh_ref, wrl_ref, br_ref,
                x1_ref, xs_ref, meta_ref, rt_ref, cnt_ref, h2_s, pos_v, pos_s, sem):
    tm = x_ref.shape[1]
    y = (jnp.dot(of_ref[0], wo_ref[0:FOX_W, :], preferred_element_type=F32)
         + jnp.dot(od_ref[0], wo_ref[FOX_W:FOX_W + DIL_W, :], preferred_element_type=F32))
    x1 = _layer_norm(DEEPNORM_ALPHA * x_ref[0] + g1_ref[0] * y, lng_ref[...], lnb_ref[...])
    x1_ref[0] = x1
    h2 = x1 * (1.0 + sc2_ref[0]) + sh2_ref[0]

    h_hi = h2.astype(BF16)
    h_lo = (h2 - h_hi.astype(F32)).astype(BF16)
    lg = (jnp.dot(h_hi, wrh_ref[...], preferred_element_type=F32)
          + jnp.dot(h_lo, wrh_ref[...], preferred_element_type=F32)
          + jnp.dot(h_hi, wrl_ref[...], preferred_element_type=F32)) + br_ref[...]
    lane = lax.broadcasted_iota(jnp.int32, (tm, LANES), 1)
    lanef = lane.astype(F32)
    far = float(LANES)
    is_g = lane < N_GROUPS
    gl = jnp.where(is_g, lg, NEG)
    gmax = jnp.max(gl, axis=1, keepdims=True)
    gidx = jnp.min(jnp.where(gl == gmax, lanef, far), axis=1, keepdims=True)
    p_g = 1.0 / jnp.sum(jnp.where(is_g, jnp.exp(gl - gmax), 0.0), axis=1, keepdims=True)
    e_lo = N_GROUPS + gidx * EXPERTS_PER_GROUP
    in_group = (lanef >= e_lo) & (lanef < e_lo + EXPERTS_PER_GROUP)
    el = jnp.where(in_group, lg, NEG)
    v1 = jnp.max(el, axis=1, keepdims=True)
    i1 = jnp.min(jnp.where(in_group & (el == v1), lanef, far), axis=1, keepdims=True)
    rest = in_group & (lanef != i1)
    el2 = jnp.where(rest, lg, NEG)
    v2 = jnp.max(el2, axis=1, keepdims=True)
    i2 = jnp.min(jnp.where(rest & (el2 == v2), lanef, far), axis=1, keepdims=True)
    e21 = jnp.exp(v2 - v1)
    w1 = p_g / (1.0 + e21)
    w2 = p_g * e21 / (1.0 + e21)

    pick1 = lanef == i1
    pick2 = lanef == i2
    onehot = jnp.where(pick1 | pick2, 1.0, 0.0)
    rr = lax.broadcasted_iota(jnp.int32, (tm, tm), 0)
    cc = lax.broadcasted_iota(jnp.int32, (tm, tm), 1)
    strict_lower = jnp.where(cc < rr, 1.0, 0.0).astype(BF16)
    before = jnp.dot(strict_lower, onehot.astype(BF16), preferred_element_type=F32)
    cnt = jnp.broadcast_to(jnp.sum(onehot, axis=0, keepdims=True), (SUBLANES, LANES))
    lane8 = lax.broadcasted_iota(jnp.int32, (SUBLANES, LANES), 1)
    incl = cnt
    for sh in (1, 2, 4, 8, 16, 32, 64):
        incl = incl + jnp.where(lane8 >= sh, pltpu.roll(incl, sh, 1), 0.0)
    start = before + (incl - cnt)[0:1]
    d1 = jnp.sum(jnp.where(pick1, start, 0.0), axis=1, keepdims=True)
    d2 = jnp.sum(jnp.where(pick2, start, 0.0), axis=1, keepdims=True)
    cnt_ref[...] = cnt

    packed = jnp.zeros((tm, LANES), F32)
    for k, val in enumerate((d1, d2, w1, w2)):
        packed = jnp.where(lane == k, val, packed)
    rt_ref[0] = packed
    meta = packed.T[0:SUBLANES, :]
    meta_ref[0] = meta
    pos_v[...] = (meta * SUBLANES).astype(jnp.int32)
    to_smem = pltpu.make_async_copy(pos_v, pos_s, sem)
    to_smem.start()
    for s in range(SUBLANES):
        h2_s[pl.ds(s, tm, stride=SUBLANES), :] = h2[:, s * LANES:(s + 1) * LANES]
    to_smem.wait()

    def place(t8, _):
        for u in range(SUBLANES):
            t = t8 * SUBLANES + u
            row = h2_s[pl.ds(pl.multiple_of(t * SUBLANES, SUBLANES), SUBLANES), :]
            for k in range(TOP_K):
                xs_ref[pl.ds(pl.multiple_of(pos_s[k, t], SUBLANES), SUBLANES), :] = row
        return 0

    lax.fori_loop(0, tm // SUBLANES, place, 0)


def _out_and_route(of, od, x, gate1, scale2, shift2, w_out, ln_g, ln_b, w_rg, b_rg, w_re, b_re, tm):
    B, S, D = x.shape
    n_r = N_GROUPS + N_EXPERTS
    w_r = jnp.pad(jnp.concatenate([w_rg, w_re], axis=1), ((0, 0), (0, LANES - n_r)))
    b_r = jnp.pad(jnp.concatenate([b_rg, b_re]), (0, LANES - n_r)).reshape(1, LANES)
    w_r_hi = w_r.astype(BF16)
    w_r_lo = (w_r - w_r_hi.astype(F32)).astype(BF16)
    nt = S // tm
    tile = lambda w: pl.BlockSpec((1, tm, w), lambda b, i: (b, i, 0))
    per_batch = pl.BlockSpec((1, 1, D), lambda b, i: (b, 0, 0))
    const = lambda shape: pl.BlockSpec(shape, lambda b, i: (0,) * len(shape))
    return pl.pallas_call(
        _out_kernel,
        grid=(B, S // tm),
        in_specs=[tile(FOX_W), tile(DIL_W), tile(D), per_batch, per_batch, per_batch,
                  const((D, D)), const((1, D)), const((1, D)), const((D, LANES)), const((D, LANES)),
                  const((1, LANES))],
        out_specs=[tile(D),
                   pl.BlockSpec((TOP_K * tm * SUBLANES, LANES), lambda b, i: (b * nt + i, 0)),
                   pl.BlockSpec((1, SUBLANES, tm), lambda b, i: (b * nt + i, 0, 0)),
                   tile(LANES),
                   pl.BlockSpec((SUBLANES, LANES), lambda b, i: (b * nt + i, 0))],
        out_shape=[jax.ShapeDtypeStruct((B, S, D), F32),
                   jax.ShapeDtypeStruct((TOP_K * B * S * SUBLANES, LANES), F32),
                   jax.ShapeDtypeStruct((B * nt, SUBLANES, tm), F32),
                   jax.ShapeDtypeStruct((B, S, LANES), F32),
                   jax.ShapeDtypeStruct((B * nt * SUBLANES, LANES), F32)],
        scratch_shapes=[pltpu.VMEM((tm * SUBLANES, LANES), F32), pltpu.VMEM((SUBLANES, tm), jnp.int32),
                        pltpu.SMEM((SUBLANES, tm), jnp.int32), pltpu.SemaphoreType.DMA(())],
        compiler_params=_params("arbitrary", "arbitrary"),
        name="out",
    )(of, od, x, gate1.reshape(B, 1, D), scale2.reshape(B, 1, D), shift2.reshape(B, 1, D),
      w_out.astype(BF16), ln_g.reshape(1, D), ln_b.reshape(1, D), w_r_hi, w_r_lo, b_r)


RUN_BITS = MOE_BLOCK.bit_length()
RUN_SMALL_BITS = 5
MAX_RUNS = 80
ST_EXPERT, ST_TILE, ST_OFF, ST_DONE = range(4)


def _copy_rows(src_ref, src_row, dst_ref, dst_row, n, sem, wait):
    def pieces(bits, pos):
        for bit in bits:
            size = 1 << bit

            @pl.when((n & size) != 0)
            def _(pos=pos, size=size):
                cp = pltpu.make_async_copy(
                    src_ref.at[pl.ds(pl.multiple_of((src_row + pos) * SUBLANES, SUBLANES), size * SUBLANES), :],
                    dst_ref.at[pl.ds(pl.multiple_of((dst_row + pos) * SUBLANES, SUBLANES), size * SUBLANES), :],
                    sem)
                if wait:
                    cp.wait()
                else:
                    cp.start()

            pos = pos + (n & size)

    small = 1 << RUN_SMALL_BITS
    pl.when(n >= small)(lambda: pieces(reversed(range(RUN_SMALL_BITS, RUN_BITS)), 0))
    pieces(reversed(range(RUN_SMALL_BITS)), n & ~(small - 1))


def _moe_kernel(blk_e_ref, nused_ref, cnt_ref, off_ref, tot_ref, x_ref, wg_ref, wu_ref, wd_ref, y_ref,
                xbuf0, xbuf1, ybuf0, ybuf1, mid_s, st, runs, sruns, gsem, ssem):
    del blk_e_ref
    i = pl.program_id(0)
    nused = nused_ref[0]
    D = xbuf0.shape[0] // MOE_BLOCK * LANES
    nk = D // MXU_DIM
    n_tiles = cnt_ref.shape[0] // N_EXPERTS
    rows_per_tile = x_ref.shape[0] // SUBLANES // n_tiles

    def gather_next(xdst, s):
        def next_expert(c):
            return c[0] + 1, 0, 0, 0

        def exhausted(c):
            return (c[0] < N_EXPERTS) & (c[3] >= tot_ref[jnp.minimum(c[0], N_EXPERTS - 1)])

        e, tile, off, done = lax.while_loop(
            exhausted, next_expert, (st[ST_EXPERT], st[ST_TILE], st[ST_OFF], st[ST_DONE]))
        ec = jnp.minimum(e, N_EXPERTS - 1)
        total = jnp.where(e < N_EXPERTS, tot_ref[ec], 0)

        def more(c):
            return (c[0] < MOE_BLOCK) & (c[4] < total)

        def take_run(c):
            filled, k, tile, off, done = c
            run = cnt_ref[tile * N_EXPERTS + ec]
            take = jnp.minimum(run - off, MOE_BLOCK - filled)
            src = tile * rows_per_tile + off_ref[tile * N_EXPERTS + ec] + off
            _copy_rows(x_ref, src, xdst, filled, take, gsem.at[s], wait=False)
            base = (s * MAX_RUNS + k) * 3
            runs[base] = src
            runs[base + 1] = filled
            runs[base + 2] = take
            run_done = off + take >= run
            return (filled + take, k + jnp.where(take > 0, 1, 0), jnp.where(run_done, tile + 1, tile),
                    jnp.where(run_done, 0, off + take), done + take)

        filled, k, tile, off, done = lax.while_loop(more, take_run, (0, 0, tile, off, done))
        _copy_rows(x_ref, 0, xdst, filled, jnp.where(total > 0, MOE_BLOCK - filled, 0), gsem.at[s], wait=False)
        runs[(2 * MAX_RUNS + s) * 3] = k
        st[ST_EXPERT] = e
        st[ST_TILE] = tile
        st[ST_OFF] = off
        st[ST_DONE] = done

    def wait_gather(xdst, s):
        pltpu.make_async_copy(x_ref.at[pl.ds(0, MOE_BLOCK * SUBLANES), :], xdst, gsem.at[s]).wait()

    def scatter_block(ysrc, s):
        def one(k, rows):
            base = (s * MAX_RUNS + k) * 3
            _copy_rows(ysrc, runs[base + 1], y_ref, runs[base], runs[base + 2], ssem.at[s], wait=False)
            return rows + runs[base + 2]

        sruns[s] = lax.fori_loop(0, runs[(2 * MAX_RUNS + s) * 3], one, 0)

    def wait_scatter(ysrc, s):
        n = pl.multiple_of(sruns[s] * SUBLANES, SUBLANES)
        pltpu.make_async_copy(ysrc.at[pl.ds(0, n), :], y_ref.at[pl.ds(0, n), :], ssem.at[s]).wait()

    def compute(xcur, ycur):
        xk = [jnp.concatenate([xcur[pl.ds(2 * j, MOE_BLOCK, stride=SUBLANES), :],
                               xcur[pl.ds(2 * j + 1, MOE_BLOCK, stride=SUBLANES), :]], axis=1).astype(BF16)
              for j in range(nk)]
        for c in range(D_EXPERT // MXU_DIM):
            cols = slice(c * MXU_DIM, (c + 1) * MXU_DIM)
            g = sum(jnp.dot(xk[j], wg_ref[0, MXU_DIM * j:MXU_DIM * (j + 1), cols].astype(BF16),
                            preferred_element_type=F32) for j in range(nk))
            u = sum(jnp.dot(xk[j], wu_ref[0, MXU_DIM * j:MXU_DIM * (j + 1), cols].astype(BF16),
                            preferred_element_type=F32) for j in range(nk))
            mid_s[:, cols] = (g * _sigmoid(g) * u).astype(BF16)
        mid = mid_s[...]
        for c in range(D // MXU_DIM):
            y = jnp.dot(mid, wd_ref[0, :, c * MXU_DIM:(c + 1) * MXU_DIM].astype(BF16), preferred_element_type=F32)
            ycur[pl.ds(2 * c, MOE_BLOCK, stride=SUBLANES), :] = y[:, 0:LANES]
            ycur[pl.ds(2 * c + 1, MOE_BLOCK, stride=SUBLANES), :] = y[:, LANES:2 * LANES]

    def step(s, xcur, xnxt, ycur, yprv):
        o = 1 - s
        wait_gather(xcur, s)

        @pl.when(i + 1 < nused)
        def _():
            gather_next(xnxt, o)

        @pl.when(i >= 2)
        def _():
            wait_scatter(ycur, s)

        compute(xcur, ycur)
        scatter_block(ycur, s)

        @pl.when(i + 1 == nused)
        def _():
            @pl.when(i >= 1)
            def _():
                wait_scatter(yprv, o)

            wait_scatter(ycur, s)

    @pl.when(i == 0)
    def _():
        for f in range(4):
            st[f] = 0
        gather_next(xbuf0, 0)

    @pl.when((i < nused) & (i % 2 == 0))
    def _():
        step(0, xbuf0, xbuf1, ybuf0, ybuf1)

    @pl.when((i < nused) & (i % 2 == 1))
    def _():
        step(1, xbuf1, xbuf0, ybuf1, ybuf0)


def _expert_mlp(x_sorted, cnt, off, tot, blk_e, nused, w_gate, w_up, w_down):
    D = w_gate.shape[1]
    nblk = blk_e.shape[0]
    assert cnt.shape[0] // N_EXPERTS + 2 <= MAX_RUNS
    wmap = lambda i, be, nu, c, o, t: (be[i], 0, 0)
    return pl.pallas_call(
        _moe_kernel,
        grid_spec=pltpu.PrefetchScalarGridSpec(
            num_scalar_prefetch=5,
            grid=(nblk,),
            in_specs=[pl.BlockSpec(memory_space=pl.ANY),
                      pl.BlockSpec((1, D, D_EXPERT), wmap),
                      pl.BlockSpec((1, D, D_EXPERT), wmap),
                      pl.BlockSpec((1, D_EXPERT, D), wmap)],
            out_specs=pl.BlockSpec(memory_space=pl.ANY),
            scratch_shapes=[pltpu.VMEM((MOE_BLOCK * SUBLANES, LANES), F32)] * 4
            + [pltpu.VMEM((MOE_BLOCK, D_EXPERT), BF16),
               pltpu.SMEM((4,), jnp.int32),
               pltpu.SMEM(((2 * MAX_RUNS + 2) * 3,), jnp.int32), pltpu.SMEM((2,), jnp.int32),
               pltpu.SemaphoreType.DMA((2,)), pltpu.SemaphoreType.DMA((2,))]),
        out_shape=jax.ShapeDtypeStruct(x_sorted.shape, F32),
        compiler_params=_params("arbitrary"),
        name="moe",
    )(blk_e, nused, cnt, off, tot, x_sorted, w_gate, w_up, w_down)


def _final_kernel(pos_ref, rt_ref, y_ref, x1_ref, g2_ref, lng_ref, lnb_ref, o_ref, pick_s):
    tm = x1_ref.shape[1]

    def unsort(t8, _):
        for u in range(SUBLANES):
            t = t8 * SUBLANES + u
            for k in range(TOP_K):
                pick_s[k, pl.ds(pl.multiple_of(t * SUBLANES, SUBLANES), SUBLANES), :] = (
                    y_ref[pl.ds(pl.multiple_of(pos_ref[k, t], SUBLANES), SUBLANES), :])
        return 0

    lax.fori_loop(0, tm // SUBLANES, unsort, 0)
    wts = rt_ref[0]
    y = jnp.concatenate([wts[:, TOP_K:TOP_K + 1] * pick_s[0, pl.ds(s, tm, stride=SUBLANES), :]
                         + wts[:, TOP_K + 1:TOP_K + 2] * pick_s[1, pl.ds(s, tm, stride=SUBLANES), :]
                         for s in range(SUBLANES)], axis=1)
    o_ref[0] = _layer_norm(DEEPNORM_ALPHA * x1_ref[0] + g2_ref[0] * y, lng_ref[...], lnb_ref[...])


def _combine_and_norm(y_sorted, meta, route, x1, gate2, ln_g, ln_b, tm):
    B, S, D = x1.shape
    nt = S // tm
    pos = (meta * SUBLANES).astype(jnp.int32).reshape(B * nt * SUBLANES, tm)
    return pl.pallas_call(
        _final_kernel,
        grid=(B, nt),
        in_specs=[pl.BlockSpec((SUBLANES, tm), lambda b, i: (b * nt + i, 0), memory_space=pltpu.SMEM),
                  pl.BlockSpec((1, tm, LANES), lambda b, i: (b, i, 0)),
                  pl.BlockSpec((TOP_K * tm * SUBLANES, LANES), lambda b, i: (b * nt + i, 0)),
                  pl.BlockSpec((1, tm, D), lambda b, i: (b, i, 0)),
                  pl.BlockSpec((1, 1, D), lambda b, i: (b, 0, 0)),
                  pl.BlockSpec((1, D), lambda b, i: (0, 0)),
                  pl.BlockSpec((1, D), lambda b, i: (0, 0))],
        out_specs=pl.BlockSpec((1, tm, D), lambda b, i: (b, i, 0)),
        out_shape=jax.ShapeDtypeStruct((B, S, D), F32),
        scratch_shapes=[pltpu.VMEM((TOP_K, tm * SUBLANES, LANES), F32)],
        compiler_params=_params("arbitrary", "arbitrary"),
        name="final",
    )(pos, route, y_sorted, x1, gate2.reshape(B, 1, D), ln_g.reshape(1, D), ln_b.reshape(1, D))


def kernel(x, c, positions, w_ada, b_ada, w_in, b_forget, w_out, ln1_g, ln1_b, w_router_group, b_router_group,
           w_router_expert, b_router_expert, w_up, w_gate, w_down, ln2_g, ln2_b):
    B, S, D = x.shape
    N = B * S
    assert D == FOX_W + DIL_W and S % 2048 == 0
    mod = _modulation(c, w_ada, b_ada)
    shift1, scale1, gate1, shift2, scale2, gate2 = jnp.split(mod, 6, axis=-1)

    pf, vt, pd, log_f = _projection(x, positions, scale1, shift1, w_in, b_forget, tm=512)
    ck, cq = _cumulative_gate(log_f)
    of = _fox_attention(pf, vt, ck, cq)
    od = _dilated_attention(pd)

    tm = 512
    x1, x_sorted, meta, route, tile_cnt = _out_and_route(
        of, od, x, gate1, scale2, shift2, w_out, ln1_g, ln1_b,
        w_router_group, b_router_group, w_router_expert, b_router_expert, tm=tm)

    cnt = tile_cnt[::SUBLANES, N_GROUPS:N_GROUPS + N_EXPERTS].astype(jnp.int32)
    off = jnp.cumsum(cnt, axis=1) - cnt
    tot = jnp.sum(cnt, axis=0)
    pend = jnp.cumsum((tot + MOE_BLOCK - 1) // MOE_BLOCK * MOE_BLOCK)
    nblk = (N * TOP_K) // MOE_BLOCK + N_EXPERTS
    nused = pend[-1:] // MOE_BLOCK
    blk_start = jnp.arange(nblk, dtype=jnp.int32) * MOE_BLOCK
    blk_e = jnp.sum((pend[None, :] <= jnp.minimum(blk_start, pend[-1] - 1)[:, None]).astype(jnp.int32), axis=1)

    y_sorted = _expert_mlp(x_sorted, cnt.reshape(-1), off.reshape(-1), tot, blk_e, nused, w_gate, w_up, w_down)
    return _combine_and_norm(y_sorted, meta, route, x1, gate2, ln2_g, ln2_b, tm=tm)
```

```python
import jax
import jax.numpy as jnp
import numpy as np
from jax import lax
from jax.experimental import pallas as pl
from jax.experimental.pallas import tpu as pltpu

HEAD_DIM = 64
FOX_HEADS = 8
DIL_HEADS = 8
FOX_W = FOX_HEADS * HEAD_DIM
DIL_W = DIL_HEADS * HEAD_DIM
DILATED_PATTERNS = ((128, 1), (512, 4), (2048, 16))
ROPE_THETA = 500000.0
ROT_DIM = HEAD_DIM // 4
N_GROUPS = 4
EXPERTS_PER_GROUP = 8
N_EXPERTS = N_GROUPS * EXPERTS_PER_GROUP
TOP_K = 2
D_EXPERT = 512
MOE_BLOCK = 512
LN_EPS = 1e-5
NEG = -1e30
DEPTH = 1
DEEPNORM_ALPHA = (2 * DEPTH) ** 0.25
QK_SCALE = HEAD_DIM ** -0.5
LOG2E = 1.4426950408889634

LANES = 128
SUBLANES = 8
BF16_SUBLANES = 16
MXU_DIM = 256
HEADS_PER_VREG = LANES // HEAD_DIM
V_ROWS = HEAD_DIM + BF16_SUBLANES
FOX_TILE = 512
DIL_UNROLL = 32
VMEM_LIMIT = 56 * 1024 * 1024

F32 = jnp.float32
BF16 = jnp.bfloat16
HIGHEST = lax.Precision.HIGHEST
NT_DIMS = (((1,), (1,)), ((), ()))


def _params(*sem):
    return pltpu.CompilerParams(dimension_semantics=sem, vmem_limit_bytes=VMEM_LIMIT)


def _sigmoid(v):
    return 1.0 / (1.0 + jnp.exp(-v))


def _layer_norm(v, g, b):
    mu = jnp.mean(v, axis=-1, keepdims=True)
    d = v - mu
    var = jnp.mean(d * d, axis=-1, keepdims=True)
    return d * lax.rsqrt(var + LN_EPS) * g + b


def _mod_kernel(c_ref, w_ref, b_ref, o_ref):
    c = c_ref[...]
    o_ref[...] = jnp.dot(c * _sigmoid(c), w_ref[...], precision=HIGHEST,
                         preferred_element_type=F32) + b_ref[...]


def _modulation(c, w_ada, b_ada):
    B, D = c.shape
    cols = w_ada.shape[1]
    tn = 1024
    return pl.pallas_call(
        _mod_kernel,
        grid=(cols // tn,),
        in_specs=[pl.BlockSpec((B, D), lambda j: (0, 0)),
                  pl.BlockSpec((D, tn), lambda j: (0, j)),
                  pl.BlockSpec((1, tn), lambda j: (0, j))],
        out_specs=pl.BlockSpec((B, tn), lambda j: (0, j)),
        out_shape=jax.ShapeDtypeStruct((B, cols), F32),
        compiler_params=_params("arbitrary"),
        name="mod",
    )(c, w_ada, b_ada.reshape(1, cols))


def _proj_kernel(x_ref, pos_ref, sc_ref, sh_ref, wf_ref, wvt_ref, one_ref, wd_ref, wff_ref, bf_ref, invf_ref,
                 sgn_ref, pf_ref, vt_ref, pd_ref, lf_ref):
    tm = x_ref.shape[1]
    h = (x_ref[0] * (1.0 + sc_ref[0]) + sh_ref[0]).astype(BF16)

    for ci in range(3):
        acc = jnp.dot(h, wf_ref[:, ci * FOX_W:(ci + 1) * FOX_W], preferred_element_type=F32)
        if ci == 0:
            acc = acc * (QK_SCALE * LOG2E)
        pf_ref[0, :, ci * FOX_W:(ci + 1) * FOX_W] = acc.astype(BF16)
    vt = lax.dot_general(wvt_ref[...], h, NT_DIMS, preferred_element_type=F32) + one_ref[...]
    vt_ref[0] = vt.astype(BF16)

    z = jnp.dot(h, wff_ref[...], preferred_element_type=F32) + bf_ref[...]
    lf_ref[0] = (jnp.minimum(z, 0.0) - jnp.log1p(jnp.exp(-jnp.abs(z)))) * LOG2E

    ang = pos_ref[0].astype(F32) * invf_ref[...]
    cs = jnp.cos(ang)
    sn = jnp.sin(ang) * sgn_ref[...]
    lane = lax.broadcasted_iota(jnp.int32, (tm, LANES), 1)
    first_half = (lane % HEAD_DIM) < (ROT_DIM // 2)
    for ci in range(3):
        acc = jnp.dot(h, wd_ref[:, ci * DIL_W:(ci + 1) * DIL_W], preferred_element_type=F32)
        if ci == 2:
            pd_ref[0, :, ci * DIL_W:(ci + 1) * DIL_W] = acc
            continue
        for j in range(DIL_W // LANES):
            t = acc[:, j * LANES:(j + 1) * LANES]
            partner = jnp.where(first_half, pltpu.roll(t, LANES - ROT_DIM // 2, 1),
                                pltpu.roll(t, ROT_DIM // 2, 1))
            r = t * cs + partner * sn
            if ci == 0:
                r = r * (QK_SCALE * LOG2E)
            pd_ref[0, :, ci * DIL_W + j * LANES:ci * DIL_W + (j + 1) * LANES] = r


def _projection(x, positions, scale1, shift1, w_in, b_forget, tm):
    B, S, D = x.shape
    o = np.cumsum((0, FOX_W, FOX_W, FOX_W, FOX_W, FOX_HEADS, DIL_W, DIL_W, DIL_W))
    w_fox = jnp.concatenate([w_in[:, o[0]:o[2]], w_in[:, o[3]:o[4]]], axis=1).astype(BF16)
    w_vt = jnp.pad(w_in[:, o[2]:o[3]].T.reshape(FOX_HEADS, HEAD_DIM, D),
                   ((0, 0), (0, V_ROWS - HEAD_DIM), (0, 0))).reshape(FOX_HEADS * V_ROWS, D).astype(BF16)
    ones_row = jnp.asarray((np.arange(FOX_HEADS * V_ROWS) % V_ROWS == HEAD_DIM).astype(np.float32)
                           ).reshape(FOX_HEADS * V_ROWS, 1)
    w_ff = jnp.pad(w_in[:, o[4]:o[5]], ((0, 0), (0, LANES - FOX_HEADS))).astype(BF16)
    w_dil = w_in[:, o[5]:o[8]].astype(BF16)
    b_f = jnp.pad(b_forget, (0, LANES - FOX_HEADS)).reshape(1, LANES)
    e = np.arange(LANES) % HEAD_DIM
    inv_freq = ROPE_THETA ** (-jnp.arange(0, ROT_DIM, 2, dtype=F32) / ROT_DIM)
    invf = jnp.where(e < ROT_DIM, jnp.tile(inv_freq, LANES // (ROT_DIM // 2)), 0.0).reshape(1, LANES)
    sgn = jnp.asarray(np.where(e < ROT_DIM // 2, -1.0, np.where(e < ROT_DIM, 1.0, 0.0)), F32).reshape(1, LANES)
    const = lambda shape: pl.BlockSpec(shape, lambda b, i: (0,) * len(shape))
    return pl.pallas_call(
        _proj_kernel,
        grid=(B, S // tm),
        in_specs=[pl.BlockSpec((1, tm, D), lambda b, i: (b, i, 0)),
                  pl.BlockSpec((1, tm, 1), lambda b, i: (b, i, 0)),
                  pl.BlockSpec((1, 1, D), lambda b, i: (b, 0, 0)),
                  pl.BlockSpec((1, 1, D), lambda b, i: (b, 0, 0)),
                  const((D, 3 * FOX_W)), const((FOX_HEADS * V_ROWS, D)), const((FOX_HEADS * V_ROWS, 1)),
                  const((D, 3 * DIL_W)), const((D, LANES)),
                  const((1, LANES)), const((1, LANES)), const((1, LANES))],
        out_specs=[pl.BlockSpec((1, tm, 3 * FOX_W), lambda b, i: (b, i, 0)),
                   pl.BlockSpec((1, FOX_HEADS * V_ROWS, tm), lambda b, i: (b, 0, i)),
                   pl.BlockSpec((1, tm, 3 * DIL_W), lambda b, i: (b, i, 0)),
                   pl.BlockSpec((1, tm, LANES), lambda b, i: (b, i, 0))],
        out_shape=[jax.ShapeDtypeStruct((B, S, 3 * FOX_W), BF16),
                   jax.ShapeDtypeStruct((B, FOX_HEADS * V_ROWS, S), BF16),
                   jax.ShapeDtypeStruct((B, S, 3 * DIL_W), F32),
                   jax.ShapeDtypeStruct((B, S, LANES), F32)],
        compiler_params=_params("arbitrary", "arbitrary"),
        name="proj",
    )(x, positions.reshape(B, S, 1), scale1.reshape(B, 1, D), shift1.reshape(B, 1, D),
      w_fox, w_vt, ones_row, w_dil, w_ff, b_f, invf, sgn)


def _cum_kernel(lf_ref, ck_ref, cq_ref):
    S = lf_ref.shape[1]
    r = lax.broadcasted_iota(jnp.int32, (LANES, LANES), 0)
    c = lax.broadcasted_iota(jnp.int32, (LANES, LANES), 1)
    tri = (c <= r).astype(F32)

    carry = jnp.zeros((1, LANES), F32)
    for j in range(S // LANES):
        rows = slice(j * LANES, (j + 1) * LANES)
        cum = jnp.dot(tri, lf_ref[0, rows, :], precision=HIGHEST, preferred_element_type=F32) + carry
        cum_t = cum.T
        for hp in range(FOX_HEADS // HEADS_PER_VREG):
            pair = slice(hp * HEADS_PER_VREG, (hp + 1) * HEADS_PER_VREG)
            ck_ref[0, hp, rows, :] = cum[:, pair]
            cq_ref[0, hp, :, rows] = cum_t[pair, :]
        carry = cum[LANES - 1:LANES, :]


def _cumulative_gate(log_f):
    B, S, _ = log_f.shape
    nhp = FOX_HEADS // HEADS_PER_VREG
    return pl.pallas_call(
        _cum_kernel,
        grid=(B,),
        in_specs=[pl.BlockSpec((1, S, LANES), lambda b: (b, 0, 0))],
        out_specs=[pl.BlockSpec((1, nhp, S, HEADS_PER_VREG), lambda b: (b, 0, 0, 0)),
                   pl.BlockSpec((1, nhp, HEADS_PER_VREG, S), lambda b: (b, 0, 0, 0))],
        out_shape=[jax.ShapeDtypeStruct((B, nhp, S, HEADS_PER_VREG), F32),
                   jax.ShapeDtypeStruct((B, nhp, HEADS_PER_VREG, S), F32)],
        compiler_params=_params("arbitrary"),
        name="cum",
    )(log_f)


def _fox_kernel(q_ref, k_ref, vt_ref, g_ref, ck_ref, cq_ref, o_ref, s_scr):
    S = q_ref.shape[1]
    T = FOX_TILE
    lane = lax.broadcasted_iota(jnp.int32, (T, LANES), 1)

    def head_queries(qoff):
        q = q_ref[0, pl.ds(qoff, T), :]
        return [jnp.where((lane // HEAD_DIM) == hh, q, jnp.zeros_like(q)) for hh in range(HEADS_PER_VREG)]

    def q_block(i, _):
        qoff = pl.multiple_of(i * T, T)
        qh = head_queries(qoff)
        cq = [cq_ref[0, 0, hh:hh + 1, pl.ds(qoff, T)] for hh in range(HEADS_PER_VREG)]

        def scores_into(buf, b, qh=qh):
            kb = k_ref[0, pl.ds(pl.multiple_of(b * T, T), T), :]
            for hh in range(HEADS_PER_VREG):
                s_scr[buf, hh] = lax.dot_general(kb, qh[hh], NT_DIMS, preferred_element_type=F32)

        def attend(buf, b, carry, masked):
            koff = pl.multiple_of(b * T, T)
            new = []
            for hh in range(HEADS_PER_VREG):
                m, acc = carry[hh]

                def scores(c0, rows):
                    s = s_scr[buf, hh, c0:c0 + rows, :] - ck_ref[0, 0, pl.ds(koff + c0, rows), hh:hh + 1]
                    if masked:
                        s = jnp.where(lax.broadcasted_iota(jnp.int32, (rows, T), 0) + c0
                                      <= lax.broadcasted_iota(jnp.int32, (rows, T), 1), s, NEG)
                    return s

                m_new = jnp.maximum(m, cq[hh] + jnp.max(scores(0, T), axis=0, keepdims=True))
                row = cq[hh] - m_new
                acc = jnp.exp2(m - m_new) * acc
                for c0 in range(0, T, MXU_DIM):
                    p = jnp.exp2((scores(c0, MXU_DIM) + row).astype(BF16))
                    vt = vt_ref[0, hh * V_ROWS:(hh + 1) * V_ROWS, pl.ds(koff + c0, MXU_DIM)]
                    acc = acc + jnp.dot(vt, p, preferred_element_type=F32)
                new.append((m_new, acc))
            return tuple(new)

        def pair(t, carry):
            scores_into(1, 2 * t + 1)
            carry = attend(0, 2 * t, carry, False)
            scores_into(0, 2 * t + 2)
            return attend(1, 2 * t + 1, carry, False)

        def odd_tail(carry):
            scores_into(1, i)
            carry = attend(0, i - 1, carry, False)
            nxt = pl.multiple_of(jnp.minimum(i + 1, S // T - 1) * T, T)
            scores_into(0, 0, head_queries(nxt))
            return attend(1, i, carry, True)

        def even_tail(carry):
            return attend(0, i, carry, True)

        init = tuple((jnp.full((1, T), NEG, F32), jnp.zeros((V_ROWS, T), F32)) for _ in range(HEADS_PER_VREG))

        @pl.when((i % 2 == 1) | (i == 0))
        def _():
            scores_into(0, 0)

        carry = lax.fori_loop(0, i // 2, pair, init)
        carry = lax.cond(i % 2 == 1, odd_tail, even_tail, carry)
        o_t = jnp.concatenate([acc[0:HEAD_DIM] * (1.0 / acc[HEAD_DIM:HEAD_DIM + 1]) for (_, acc) in carry], axis=0)
        gate = _sigmoid(g_ref[0, pl.ds(qoff, T), :].astype(F32))
        o_ref[0, pl.ds(qoff, T), :] = (o_t.T * gate).astype(BF16)
        return 0

    lax.fori_loop(0, S // T, q_block, 0)


def _fox_attention(pf, vt, ck, cq):
    B, S, _ = pf.shape
    nhp = FOX_W // LANES
    assert S % FOX_TILE == 0
    return pl.pallas_call(
        _fox_kernel,
        grid=(B, nhp),
        in_specs=[pl.BlockSpec((1, S, LANES), lambda b, h: (b, 0, h)),
                  pl.BlockSpec((1, S, LANES), lambda b, h: (b, 0, nhp + h)),
                  pl.BlockSpec((1, HEADS_PER_VREG * V_ROWS, S), lambda b, h: (b, h, 0)),
                  pl.BlockSpec((1, S, LANES), lambda b, h: (b, 0, 2 * nhp + h)),
                  pl.BlockSpec((1, 1, S, HEADS_PER_VREG), lambda b, h: (b, h, 0, 0)),
                  pl.BlockSpec((1, 1, HEADS_PER_VREG, S), lambda b, h: (b, h, 0, 0))],
        out_specs=pl.BlockSpec((1, S, LANES), lambda b, h: (b, 0, h)),
        out_shape=jax.ShapeDtypeStruct((B, S, FOX_W), BF16),
        scratch_shapes=[pltpu.VMEM((2, HEADS_PER_VREG, FOX_TILE, FOX_TILE), F32)],
        compiler_params=_params("arbitrary", "arbitrary"),
        name="fox",
    )(pf, pf, vt, pf, ck, cq)


def _dil_kernel(q_ref, k_ref, v_ref, o_ref, m_s, l_s, a_s, q4, k4, v4, m4, l4, a4):
    S = q_ref.shape[1]
    W = LANES
    lane = lax.broadcasted_iota(jnp.int32, (W, LANES), 1)
    head0 = lane < HEAD_DIM
    ri = lax.broadcasted_iota(jnp.int32, (2 * W, 2 * W), 0) % W
    ci = lax.broadcasted_iota(jnp.int32, (2 * W, 2 * W), 1)
    bias_rest = jnp.where((ci >= ri) & (ci <= ri + W), 0.0, NEG)
    bias_first = jnp.where((ci < W) & (ci <= ri), 0.0, NEG)
    ones = jnp.ones((2 * W, LANES), BF16)

    assert DILATED_PATTERNS == ((W, 1), (4 * W, 4), (16 * W, 16)) and S % (16 * W * 2) == 0
    assert (S // W) % DIL_UNROLL == 0
    L4 = S // 4

    def block_stats(q, kb, vb, first):
        qs = jnp.concatenate([jnp.where(head0, q, 0.0), jnp.where(head0, 0.0, q)], axis=0).astype(BF16)
        vb2 = jnp.concatenate([vb.astype(BF16), ones], axis=1)
        s = lax.dot_general(qs, kb.astype(BF16), NT_DIMS, preferred_element_type=F32)
        s = s + jnp.where(first, bias_first, bias_rest)
        m2 = jnp.max(s, axis=1, keepdims=True)
        out = jnp.dot(jnp.exp2((s - m2).astype(BF16)), vb2, preferred_element_type=F32)
        a_u = jnp.where(head0, out[0:W, 0:LANES], out[W:2 * W, 0:LANES])
        l_u = jnp.where(head0, out[0:W, LANES:2 * LANES], out[W:2 * W, LANES:2 * LANES])
        m_u = jnp.where(head0, m2[0:W], m2[W:2 * W])
        return m_u, l_u, a_u

    def merge(idx, stats):
        m_u, l_u, a_u = stats
        m_o = m4[idx, :]
        m_n = jnp.maximum(m_o, m_u)
        e_o = jnp.exp2(m_o - m_n)
        e_u = jnp.exp2(m_u - m_n)
        m4[idx, :] = m_n
        l4[idx, :] = l4[idx, :] * e_o + l_u * e_u
        a4[idx, :] = a4[idx, :] * e_o + a_u * e_u

    def for_blocks(unit):
        def group(t, _):
            for uu in range(DIL_UNROLL):
                unit(t * DIL_UNROLL + uu)
            return 0
        lax.fori_loop(0, S // W // DIL_UNROLL, group, 0)

    def unit1(u):
        qidx = pl.ds(pl.multiple_of(u * W, W), W)
        kidx = pl.ds(pl.multiple_of(jnp.maximum(u - 1, 0) * W, W), 2 * W)
        m_s[qidx, :], l_s[qidx, :], a_s[qidx, :] = block_stats(q_ref[0, qidx, :], k_ref[0, kidx, :],
                                                               v_ref[0, kidx, :], u == 0)
    for_blocks(unit1)

    def reorder(c, _):
        for r in range(4):
            src = pl.ds(c * (4 * W) + r, W, stride=4)
            dst = pl.ds(pl.multiple_of(r * L4 + c * W, W), W)
            q4[dst, :] = q_ref[0, src, :]
            k4[dst, :] = k_ref[0, src, :]
            v4[dst, :] = v_ref[0, src, :]
            m4[dst, :] = m_s[src, :]
            l4[dst, :] = l_s[src, :]
            a4[dst, :] = a_s[src, :]
        return 0
    lax.fori_loop(0, L4 // W, reorder, 0)

    nb4 = L4 // W

    def unit4(u):
        n = u % nb4
        base = (u // nb4) * L4
        qidx = pl.ds(pl.multiple_of(base + n * W, W), W)
        kidx = pl.ds(pl.multiple_of(base + jnp.maximum(n - 1, 0) * W, W), 2 * W)
        merge(qidx, block_stats(q4[qidx, :], k4[kidx, :], v4[kidx, :], n == 0))
    for_blocks(unit4)

    nb16 = S // (16 * W)

    def unit16(u):
        n = u % nb16
        r16 = u // nb16
        base = (r16 % 4) * L4 + r16 // 4
        qidx = pl.ds(base + n * (4 * W), W, stride=4)
        kidx = pl.ds(base + jnp.maximum(n - 1, 0) * (4 * W), 2 * W, stride=4)
        merge(qidx, block_stats(q4[qidx, :], k4[kidx, :], v4[kidx, :], n == 0))
    for_blocks(unit16)

    def finish(c, _):
        for r in range(4):
            src = pl.ds(pl.multiple_of(r * L4 + c * W, W), W)
            a_s[pl.ds(c * (4 * W) + r, W, stride=4), :] = a4[src, :] * (1.0 / l4[src, :])
        return 0
    lax.fori_loop(0, L4 // W, finish, 0)
    o_ref[0] = a_s[...].astype(BF16)


def _dilated_attention(pd):
    B, S, _ = pd.shape
    nhp = DIL_W // LANES
    return pl.pallas_call(
        _dil_kernel,
        grid=(B, nhp),
        in_specs=[pl.BlockSpec((1, S, LANES), lambda b, h: (b, 0, h)),
                  pl.BlockSpec((1, S, LANES), lambda b, h: (b, 0, nhp + h)),
                  pl.BlockSpec((1, S, LANES), lambda b, h: (b, 0, 2 * nhp + h))],
        out_specs=pl.BlockSpec((1, S, LANES), lambda b, h: (b, 0, h)),
        out_shape=jax.ShapeDtypeStruct((B, S, DIL_W), BF16),
        scratch_shapes=[pltpu.VMEM((S, LANES), F32)] * 9,
        compiler_params=_params("arbitrary", "arbitrary"),
        name="dil",
    )(pd, pd, pd)


def _out_kernel(of_ref, od_ref, x_ref, g1_ref, sc2_ref, sh2_ref, wo_ref, lng_ref, lnb_ref, wrh_ref, wrl_ref, br_ref,
                x1_ref, xs_ref, meta_ref, rt_ref, cnt_ref, h2_s, pos_v, pos_s, sem):
    tm = x_ref.shape[1]
    y = (jnp.dot(of_ref[0], wo_ref[0:FOX_W, :], preferred_element_type=F32)
         + jnp.dot(od_ref[0], wo_ref[FOX_W:FOX_W + DIL_W, :], preferred_element_type=F32))
    x1 = _layer_norm(DEEPNORM_ALPHA * x_ref[0] + g1_ref[0] * y, lng_ref[...], lnb_ref[...])
    x1_ref[0] = x1
    h2 = x1 * (1.0 + sc2_ref[0]) + sh2_ref[0]

    h_hi = h2.astype(BF16)
    h_lo = (h2 - h_hi.astype(F32)).astype(BF16)
    lg = (jnp.dot(h_hi, wrh_ref[...], preferred_element_type=F32)
          + jnp.dot(h_lo, wrh_ref[...], preferred_element_type=F32)
          + jnp.dot(h_hi, wrl_ref[...], preferred_element_type=F32)) + br_ref[...]
    lane = lax.broadcasted_iota(jnp.int32, (tm, LANES), 1)
    lanef = lane.astype(F32)
    far = float(LANES)
    is_g = lane < N_GROUPS
    gl = jnp.where(is_g, lg, NEG)
    gmax = jnp.max(gl, axis=1, keepdims=True)
    gidx = jnp.min(jnp.where(gl == gmax, lanef, far), axis=1, keepdims=True)
    p_g = 1.0 / jnp.sum(jnp.where(is_g, jnp.exp(gl - gmax), 0.0), axis=1, keepdims=True)
    e_lo = N_GROUPS + gidx * EXPERTS_PER_GROUP
    in_group = (lanef >= e_lo) & (lanef < e_lo + EXPERTS_PER_GROUP)
    el = jnp.where(in_group, lg, NEG)
    v1 = jnp.max(el, axis=1, keepdims=True)
    i1 = jnp.min(jnp.where(in_group & (el == v1), lanef, far), axis=1, keepdims=True)
    rest = in_group & (lanef != i1)
    el2 = jnp.where(rest, lg, NEG)
    v2 = jnp.max(el2, axis=1, keepdims=True)
    i2 = jnp.min(jnp.where(rest & (el2 == v2), lanef, far), axis=1, keepdims=True)
    e21 = jnp.exp(v2 - v1)
    w1 = p_g / (1.0 + e21)
    w2 = p_g * e21 / (1.0 + e21)

    pick1 = lanef == i1
    pick2 = lanef == i2
    onehot = jnp.where(pick1 | pick2, 1.0, 0.0)
    rr = lax.broadcasted_iota(jnp.int32, (tm, tm), 0)
    cc = lax.broadcasted_iota(jnp.int32, (tm, tm), 1)
    strict_lower = jnp.where(cc < rr, 1.0, 0.0).astype(BF16)
    before = jnp.dot(strict_lower, onehot.astype(BF16), preferred_element_type=F32)
    cnt = jnp.broadcast_to(jnp.sum(onehot, axis=0, keepdims=True), (SUBLANES, LANES))
    lane8 = lax.broadcasted_iota(jnp.int32, (SUBLANES, LANES), 1)
    incl = cnt
    for sh in (1, 2, 4, 8, 16, 32, 64):
        incl = incl + jnp.where(lane8 >= sh, pltpu.roll(incl, sh, 1), 0.0)
    start = before + (incl - cnt)[0:1]
    d1 = jnp.sum(jnp.where(pick1, start, 0.0), axis=1, keepdims=True)
    d2 = jnp.sum(jnp.where(pick2, start, 0.0), axis=1, keepdims=True)
    cnt_ref[...] = cnt

    packed = jnp.zeros((tm, LANES), F32)
    for k, val in enumerate((d1, d2, w1, w2)):
        packed = jnp.where(lane == k, val, packed)
    rt_ref[0] = packed
    meta = packed.T[0:SUBLANES, :]
    meta_ref[0] = meta
    pos_v[...] = (meta * SUBLANES).astype(jnp.int32)
    to_smem = pltpu.make_async_copy(pos_v, pos_s, sem)
    to_smem.start()
    for s in range(SUBLANES):
        h2_s[pl.ds(s, tm, stride=SUBLANES), :] = h2[:, s * LANES:(s + 1) * LANES]
    to_smem.wait()

    def place(t8, _):
        for u in range(SUBLANES):
            t = t8 * SUBLANES + u
            row = h2_s[pl.ds(pl.multiple_of(t * SUBLANES, SUBLANES), SUBLANES), :]
            for k in range(TOP_K):
                xs_ref[pl.ds(pl.multiple_of(pos_s[k, t], SUBLANES), SUBLANES), :] = row
        return 0

    lax.fori_loop(0, tm // SUBLANES, place, 0)


def _out_and_route(of, od, x, gate1, scale2, shift2, w_out, ln_g, ln_b, w_rg, b_rg, w_re, b_re, tm):
    B, S, D = x.shape
    n_r = N_GROUPS + N_EXPERTS
    w_r = jnp.pad(jnp.concatenate([w_rg, w_re], axis=1), ((0, 0), (0, LANES - n_r)))
    b_r = jnp.pad(jnp.concatenate([b_rg, b_re]), (0, LANES - n_r)).reshape(1, LANES)
    w_r_hi = w_r.astype(BF16)
    w_r_lo = (w_r - w_r_hi.astype(F32)).astype(BF16)
    nt = S // tm
    tile = lambda w: pl.BlockSpec((1, tm, w), lambda b, i: (b, i, 0))
    per_batch = pl.BlockSpec((1, 1, D), lambda b, i: (b, 0, 0))
    const = lambda shape: pl.BlockSpec(shape, lambda b, i: (0,) * len(shape))
    return pl.pallas_call(
        _out_kernel,
        grid=(B, S // tm),
        in_specs=[tile(FOX_W), tile(DIL_W), tile(D), per_batch, per_batch, per_batch,
                  const((D, D)), const((1, D)), const((1, D)), const((D, LANES)), const((D, LANES)),
                  const((1, LANES))],
        out_specs=[tile(D),
                   pl.BlockSpec((TOP_K * tm * SUBLANES, LANES), lambda b, i: (b * nt + i, 0)),
                   pl.BlockSpec((1, SUBLANES, tm), lambda b, i: (b * nt + i, 0, 0)),
                   tile(LANES),
                   pl.BlockSpec((SUBLANES, LANES), lambda b, i: (b * nt + i, 0))],
        out_shape=[jax.ShapeDtypeStruct((B, S, D), F32),
                   jax.ShapeDtypeStruct((TOP_K * B * S * SUBLANES, LANES), F32),
                   jax.ShapeDtypeStruct((B * nt, SUBLANES, tm), F32),
                   jax.ShapeDtypeStruct((B, S, LANES), F32),
                   jax.ShapeDtypeStruct((B * nt * SUBLANES, LANES), F32)],
        scratch_shapes=[pltpu.VMEM((tm * SUBLANES, LANES), F32), pltpu.VMEM((SUBLANES, tm), jnp.int32),
                        pltpu.SMEM((SUBLANES, tm), jnp.int32), pltpu.SemaphoreType.DMA(())],
        compiler_params=_params("arbitrary", "arbitrary"),
        name="out",
    )(of, od, x, gate1.reshape(B, 1, D), scale2.reshape(B, 1, D), shift2.reshape(B, 1, D),
      w_out.astype(BF16), ln_g.reshape(1, D), ln_b.reshape(1, D), w_r_hi, w_r_lo, b_r)


RUN_BITS = MOE_BLOCK.bit_length()
RUN_SMALL_BITS = 5
MAX_RUNS = 80
ST_EXPERT, ST_TILE, ST_OFF, ST_DONE = range(4)


def _copy_rows(src_ref, src_row, dst_ref, dst_row, n, sem, wait):
    def pieces(bits, pos):
        for bit in bits:
            size = 1 << bit

            @pl.when((n & size) != 0)
            def _(pos=pos, size=size):
                cp = pltpu.make_async_copy(
                    src_ref.at[pl.ds(pl.multiple_of((src_row + pos) * SUBLANES, SUBLANES), size * SUBLANES), :],
                    dst_ref.at[pl.ds(pl.multiple_of((dst_row + pos) * SUBLANES, SUBLANES), size * SUBLANES), :],
                    sem)
                if wait:
                    cp.wait()
                else:
                    cp.start()

            pos = pos + (n & size)

    small = 1 << RUN_SMALL_BITS
    pl.when(n >= small)(lambda: pieces(reversed(range(RUN_SMALL_BITS, RUN_BITS)), 0))
    pieces(reversed(range(RUN_SMALL_BITS)), n & ~(small - 1))


def _moe_kernel(blk_e_ref, nused_ref, cnt_ref, off_ref, tot_ref, x_ref, wg_ref, wu_ref, wd_ref, y_ref,
                xbuf0, xbuf1, ybuf0, ybuf1, mid_s, st, runs, sruns, gsem, ssem):
    del blk_e_ref
    i = pl.program_id(0)
    nused = nused_ref[0]
    D = xbuf0.shape[0] // MOE_BLOCK * LANES
    nk = D // MXU_DIM
    n_tiles = cnt_ref.shape[0] // N_EXPERTS
    rows_per_tile = x_ref.shape[0] // SUBLANES // n_tiles

    def gather_next(xdst, s):
        def next_expert(c):
            return c[0] + 1, 0, 0, 0

        def exhausted(c):
            return (c[0] < N_EXPERTS) & (c[3] >= tot_ref[jnp.minimum(c[0], N_EXPERTS - 1)])

        e, tile, off, done = lax.while_loop(
            exhausted, next_expert, (st[ST_EXPERT], st[ST_TILE], st[ST_OFF], st[ST_DONE]))
        ec = jnp.minimum(e, N_EXPERTS - 1)
        total = jnp.where(e < N_EXPERTS, tot_ref[ec], 0)

        def more(c):
            return (c[0] < MOE_BLOCK) & (c[4] < total)

        def take_run(c):
            filled, k, tile, off, done = c
            run = cnt_ref[tile * N_EXPERTS + ec]
            take = jnp.minimum(run - off, MOE_BLOCK - filled)
            src = tile * rows_per_tile + off_ref[tile * N_EXPERTS + ec] + off
            _copy_rows(x_ref, src, xdst, filled, take, gsem.at[s], wait=False)
            base = (s * MAX_RUNS + k) * 3
            runs[base] = src
            runs[base + 1] = filled
            runs[base + 2] = take
            run_done = off + take >= run
            return (filled + take, k + jnp.where(take > 0, 1, 0), jnp.where(run_done, tile + 1, tile),
                    jnp.where(run_done, 0, off + take), done + take)

        filled, k, tile, off, done = lax.while_loop(more, take_run, (0, 0, tile, off, done))
        _copy_rows(x_ref, 0, xdst, filled, jnp.where(total > 0, MOE_BLOCK - filled, 0), gsem.at[s], wait=False)
        runs[(2 * MAX_RUNS + s) * 3] = k
        st[ST_EXPERT] = e
        st[ST_TILE] = tile
        st[ST_OFF] = off
        st[ST_DONE] = done

    def wait_gather(xdst, s):
        pltpu.make_async_copy(x_ref.at[pl.ds(0, MOE_BLOCK * SUBLANES), :], xdst, gsem.at[s]).wait()

    def scatter_block(ysrc, s):
        def one(k, rows):
            base = (s * MAX_RUNS + k) * 3
            _copy_rows(ysrc, runs[base + 1], y_ref, runs[base], runs[base + 2], ssem.at[s], wait=False)
            return rows + runs[base + 2]

        sruns[s] = lax.fori_loop(0, runs[(2 * MAX_RUNS + s) * 3], one, 0)

    def wait_scatter(ysrc, s):
        n = pl.multiple_of(sruns[s] * SUBLANES, SUBLANES)
        pltpu.make_async_copy(ysrc.at[pl.ds(0, n), :], y_ref.at[pl.ds(0, n), :], ssem.at[s]).wait()

    def compute(xcur, ycur):
        xk = [jnp.concatenate([xcur[pl.ds(2 * j, MOE_BLOCK, stride=SUBLANES), :],
                               xcur[pl.ds(2 * j + 1, MOE_BLOCK, stride=SUBLANES), :]], axis=1).astype(BF16)
              for j in range(nk)]
        for c in range(D_EXPERT // MXU_DIM):
            cols = slice(c * MXU_DIM, (c + 1) * MXU_DIM)
            g = sum(jnp.dot(xk[j], wg_ref[0, MXU_DIM * j:MXU_DIM * (j + 1), cols].astype(BF16),
                            preferred_element_type=F32) for j in range(nk))
            u = sum(jnp.dot(xk[j], wu_ref[0, MXU_DIM * j:MXU_DIM * (j + 1), cols].astype(BF16),
                            preferred_element_type=F32) for j in range(nk))
            mid_s[:, cols] = (g * _sigmoid(g) * u).astype(BF16)
        mid = mid_s[...]
        for c in range(D // MXU_DIM):
            y = jnp.dot(mid, wd_ref[0, :, c * MXU_DIM:(c + 1) * MXU_DIM].astype(BF16), preferred_element_type=F32)
            ycur[pl.ds(2 * c, MOE_BLOCK, stride=SUBLANES), :] = y[:, 0:LANES]
            ycur[pl.ds(2 * c + 1, MOE_BLOCK, stride=SUBLANES), :] = y[:, LANES:2 * LANES]

    def step(s, xcur, xnxt, ycur, yprv):
        o = 1 - s
        wait_gather(xcur, s)

        @pl.when(i + 1 < nused)
        def _():
            gather_next(xnxt, o)

        @pl.when(i >= 2)
        def _():
            wait_scatter(ycur, s)

        compute(xcur, ycur)
        scatter_block(ycur, s)

        @pl.when(i + 1 == nused)
        def _():
            @pl.when(i >= 1)
            def _():
                wait_scatter(yprv, o)

            wait_scatter(ycur, s)

    @pl.when(i == 0)
    def _():
        for f in range(4):
            st[f] = 0
        gather_next(xbuf0, 0)

    @pl.when((i < nused) & (i % 2 == 0))
    def _():
        step(0, xbuf0, xbuf1, ybuf0, ybuf1)

    @pl.when((i < nused) & (i % 2 == 1))
    def _():
        step(1, xbuf1, xbuf0, ybuf1, ybuf0)


def _expert_mlp(x_sorted, cnt, off, tot, blk_e, nused, w_gate, w_up, w_down):
    D = w_gate.shape[1]
    nblk = blk_e.shape[0]
    assert cnt.shape[0] // N_EXPERTS + 2 <= MAX_RUNS
    wmap = lambda i, be, nu, c, o, t: (be[i], 0, 0)
    return pl.pallas_call(
        _moe_kernel,
        grid_spec=pltpu.PrefetchScalarGridSpec(
            num_scalar_prefetch=5,
            grid=(nblk,),
            in_specs=[pl.BlockSpec(memory_space=pl.ANY),
                      pl.BlockSpec((1, D, D_EXPERT), wmap),
                      pl.BlockSpec((1, D, D_EXPERT), wmap),
                      pl.BlockSpec((1, D_EXPERT, D), wmap)],
            out_specs=pl.BlockSpec(memory_space=pl.ANY),
            scratch_shapes=[pltpu.VMEM((MOE_BLOCK * SUBLANES, LANES), F32)] * 4
            + [pltpu.VMEM((MOE_BLOCK, D_EXPERT), BF16),
               pltpu.SMEM((4,), jnp.int32),
               pltpu.SMEM(((2 * MAX_RUNS + 2) * 3,), jnp.int32), pltpu.SMEM((2,), jnp.int32),
               pltpu.SemaphoreType.DMA((2,)), pltpu.SemaphoreType.DMA((2,))]),
        out_shape=jax.ShapeDtypeStruct(x_sorted.shape, F32),
        compiler_params=_params("arbitrary"),
        name="moe",
    )(blk_e, nused, cnt, off, tot, x_sorted, w_gate, w_up, w_down)


def _final_kernel(pos_ref, rt_ref, y_ref, x1_ref, g2_ref, lng_ref, lnb_ref, o_ref, pick_s):
    tm = x1_ref.shape[1]

    def unsort(t8, _):
        for u in range(SUBLANES):
            t = t8 * SUBLANES + u
            for k in range(TOP_K):
                pick_s[k, pl.ds(pl.multiple_of(t * SUBLANES, SUBLANES), SUBLANES), :] = (
                    y_ref[pl.ds(pl.multiple_of(pos_ref[k, t], SUBLANES), SUBLANES), :])
        return 0

    lax.fori_loop(0, tm // SUBLANES, unsort, 0)
    wts = rt_ref[0]
    y = jnp.concatenate([wts[:, TOP_K:TOP_K + 1] * pick_s[0, pl.ds(s, tm, stride=SUBLANES), :]
                         + wts[:, TOP_K + 1:TOP_K + 2] * pick_s[1, pl.ds(s, tm, stride=SUBLANES), :]
                         for s in range(SUBLANES)], axis=1)
    o_ref[0] = _layer_norm(DEEPNORM_ALPHA * x1_ref[0] + g2_ref[0] * y, lng_ref[...], lnb_ref[...])


def _combine_and_norm(y_sorted, meta, route, x1, gate2, ln_g, ln_b, tm):
    B, S, D = x1.shape
    nt = S // tm
    pos = (meta * SUBLANES).astype(jnp.int32).reshape(B * nt * SUBLANES, tm)
    return pl.pallas_call(
        _final_kernel,
        grid=(B, nt),
        in_specs=[pl.BlockSpec((SUBLANES, tm), lambda b, i: (b * nt + i, 0), memory_space=pltpu.SMEM),
                  pl.BlockSpec((1, tm, LANES), lambda b, i: (b, i, 0)),
                  pl.BlockSpec((TOP_K * tm * SUBLANES, LANES), lambda b, i: (b * nt + i, 0)),
                  pl.BlockSpec((1, tm, D), lambda b, i: (b, i, 0)),
                  pl.BlockSpec((1, 1, D), lambda b, i: (b, 0, 0)),
                  pl.BlockSpec((1, D), lambda b, i: (0, 0)),
                  pl.BlockSpec((1, D), lambda b, i: (0, 0))],
        out_specs=pl.BlockSpec((1, tm, D), lambda b, i: (b, i, 0)),
        out_shape=jax.ShapeDtypeStruct((B, S, D), F32),
        scratch_shapes=[pltpu.VMEM((TOP_K, tm * SUBLANES, LANES), F32)],
        compiler_params=_params("arbitrary", "arbitrary"),
        name="final",
    )(pos, route, y_sorted, x1, gate2.reshape(B, 1, D), ln_g.reshape(1, D), ln_b.reshape(1, D))


def kernel(x, c, positions, w_ada, b_ada, w_in, b_forget, w_out, ln1_g, ln1_b, w_router_group, b_router_group,
           w_router_expert, b_router_expert, w_up, w_gate, w_down, ln2_g, ln2_b):
    B, S, D = x.shape
    N = B * S
    assert D == FOX_W + DIL_W and S % 2048 == 0
    mod = _modulation(c, w_ada, b_ada)
    shift1, scale1, gate1, shift2, scale2, gate2 = jnp.split(mod, 6, axis=-1)

    pf, vt, pd, log_f = _projection(x, positions, scale1, shift1, w_in, b_forget, tm=512)
    ck, cq = _cumulative_gate(log_f)
    of = _fox_attention(pf, vt, ck, cq)
    od = _dilated_attention(pd)

    tm = 512
    x1, x_sorted, meta, route, tile_cnt = _out_and_route(
        of, od, x, gate1, scale2, shift2, w_out, ln1_g, ln1_b,
        w_router_group, b_router_group, w_router_expert, b_router_expert, tm=tm)

    cnt = tile_cnt[::SUBLANES, N_GROUPS:N_GROUPS + N_EXPERTS].astype(jnp.int32)
    off = jnp.cumsum(cnt, axis=1) - cnt
    tot = jnp.sum(cnt, axis=0)
    pend = jnp.cumsum((tot + MOE_BLOCK - 1) // MOE_BLOCK * MOE_BLOCK)
    nblk = (N * TOP_K) // MOE_BLOCK + N_EXPERTS
    nused = pend[-1:] // MOE_BLOCK
    blk_start = jnp.arange(nblk, dtype=jnp.int32) * MOE_BLOCK
    blk_e = jnp.sum((pend[None, :] <= jnp.minimum(blk_start, pend[-1] - 1)[:, None]).astype(jnp.int32), axis=1)

    y_sorted = _expert_mlp(x_sorted, cnt.reshape(-1), off.reshape(-1), tot, blk_e, nused, w_gate, w_up, w_down)
    return _combine_and_norm(y_sorted, meta, route, x1, gate2, ln2_g, ln2_b, tm=tm)
```

```python
import functools

import jax
import jax.numpy as jnp
import numpy as np
from jax import lax
from jax.experimental import pallas as pl
from jax.experimental.pallas import tpu as pltpu

HEAD_DIM = 64
FOX_HEADS = 8
DIL_HEADS = 8
FOX_W = FOX_HEADS * HEAD_DIM
DIL_W = DIL_HEADS * HEAD_DIM
DILATED_PATTERNS = ((128, 1), (512, 4), (2048, 16))
ROPE_THETA = 500000.0
ROT_DIM = HEAD_DIM // 4
N_GROUPS = 4
EXPERTS_PER_GROUP = 8
N_EXPERTS = N_GROUPS * EXPERTS_PER_GROUP
TOP_K = 2
D_EXPERT = 512
MOE_BLOCK = 512
LN_EPS = 1e-5
NEG = -1e30
DEPTH = 1
DEEPNORM_ALPHA = (2 * DEPTH) ** 0.25
QK_SCALE = HEAD_DIM ** -0.5
LOG2E = 1.4426950408889634

LANES = 128
SUBLANES = 8
BF16_SUBLANES = 16
MXU_DIM = 256
HEADS_PER_VREG = LANES // HEAD_DIM
V_ROWS = HEAD_DIM + BF16_SUBLANES
FOX_TILE = 512
DIL_UNROLL = 32
VMEM_LIMIT = 56 * 1024 * 1024

F32 = jnp.float32
BF16 = jnp.bfloat16
HIGHEST = lax.Precision.HIGHEST
NT_DIMS = (((1,), (1,)), ((), ()))


def _params(*sem):
    return pltpu.CompilerParams(dimension_semantics=sem, vmem_limit_bytes=VMEM_LIMIT)


def _sigmoid(v):
    return 1.0 / (1.0 + jnp.exp(-v))


def _layer_norm(v, g, b):
    mu = jnp.mean(v, axis=-1, keepdims=True)
    d = v - mu
    var = jnp.mean(d * d, axis=-1, keepdims=True)
    return d * lax.rsqrt(var + LN_EPS) * g + b


def _mod_kernel(c_ref, w_ref, b_ref, o_ref):
    c = c_ref[...]
    o_ref[...] = jnp.dot(c * _sigmoid(c), w_ref[...], precision=HIGHEST,
                         preferred_element_type=F32) + b_ref[...]


def _modulation(c, w_ada, b_ada):
    B, D = c.shape
    cols = w_ada.shape[1]
    tn = 1024
    return pl.pallas_call(
        _mod_kernel,
        grid=(cols // tn,),
        in_specs=[pl.BlockSpec((B, D), lambda j: (0, 0)),
                  pl.BlockSpec((D, tn), lambda j: (0, j)),
                  pl.BlockSpec((1, tn), lambda j: (0, j))],
        out_specs=pl.BlockSpec((B, tn), lambda j: (0, j)),
        out_shape=jax.ShapeDtypeStruct((B, cols), F32),
        compiler_params=_params("arbitrary"),
        name="mod",
    )(c, w_ada, b_ada.reshape(1, cols))


def _proj_kernel(x_ref, pos_ref, sc_ref, sh_ref, wf_ref, wvt_ref, one_ref, wd_ref, wff_ref, bf_ref, invf_ref,
                 sgn_ref, pf_ref, vt_ref, pd_ref, lf_ref):
    tm = x_ref.shape[1]
    h = (x_ref[0] * (1.0 + sc_ref[0]) + sh_ref[0]).astype(BF16)

    for ci in range(3):
        acc = jnp.dot(h, wf_ref[:, ci * FOX_W:(ci + 1) * FOX_W], preferred_element_type=F32)
        if ci == 0:
            acc = acc * (QK_SCALE * LOG2E)
        pf_ref[0, :, ci * FOX_W:(ci + 1) * FOX_W] = acc.astype(BF16)
    vt = lax.dot_general(wvt_ref[...], h, NT_DIMS, preferred_element_type=F32) + one_ref[...]
    vt_ref[0] = vt.astype(BF16)

    z = jnp.dot(h, wff_ref[...], preferred_element_type=F32) + bf_ref[...]
    lf_ref[0] = (jnp.minimum(z, 0.0) - jnp.log1p(jnp.exp(-jnp.abs(z)))) * LOG2E

    ang = pos_ref[0].astype(F32) * invf_ref[...]
    cs = jnp.cos(ang)
    sn = jnp.sin(ang) * sgn_ref[...]
    lane = lax.broadcasted_iota(jnp.int32, (tm, LANES), 1)
    first_half = (lane % HEAD_DIM) < (ROT_DIM // 2)
    for ci in range(3):
        acc = jnp.dot(h, wd_ref[:, ci * DIL_W:(ci + 1) * DIL_W], preferred_element_type=F32)
        if ci == 2:
            pd_ref[0, :, ci * DIL_W:(ci + 1) * DIL_W] = acc
            continue
        for j in range(DIL_W // LANES):
            t = acc[:, j * LANES:(j + 1) * LANES]
            partner = jnp.where(first_half, pltpu.roll(t, LANES - ROT_DIM // 2, 1),
                                pltpu.roll(t, ROT_DIM // 2, 1))
            r = t * cs + partner * sn
            if ci == 0:
                r = r * (QK_SCALE * LOG2E)
            pd_ref[0, :, ci * DIL_W + j * LANES:ci * DIL_W + (j + 1) * LANES] = r


def _projection(x, positions, scale1, shift1, w_in, b_forget, tm):
    B, S, D = x.shape
    o = np.cumsum((0, FOX_W, FOX_W, FOX_W, FOX_W, FOX_HEADS, DIL_W, DIL_W, DIL_W))
    w_fox = jnp.concatenate([w_in[:, o[0]:o[2]], w_in[:, o[3]:o[4]]], axis=1).astype(BF16)
    w_vt = jnp.pad(w_in[:, o[2]:o[3]].T.reshape(FOX_HEADS, HEAD_DIM, D),
                   ((0, 0), (0, V_ROWS - HEAD_DIM), (0, 0))).reshape(FOX_HEADS * V_ROWS, D).astype(BF16)
    ones_row = jnp.asarray((np.arange(FOX_HEADS * V_ROWS) % V_ROWS == HEAD_DIM).astype(np.float32)
                           ).reshape(FOX_HEADS * V_ROWS, 1)
    w_ff = jnp.pad(w_in[:, o[4]:o[5]], ((0, 0), (0, LANES - FOX_HEADS))).astype(BF16)
    w_dil = w_in[:, o[5]:o[8]].astype(BF16)
    b_f = jnp.pad(b_forget, (0, LANES - FOX_HEADS)).reshape(1, LANES)
    e = np.arange(LANES) % HEAD_DIM
    inv_freq = ROPE_THETA ** (-jnp.arange(0, ROT_DIM, 2, dtype=F32) / ROT_DIM)
    invf = jnp.where(e < ROT_DIM, jnp.tile(inv_freq, LANES // (ROT_DIM // 2)), 0.0).reshape(1, LANES)
    sgn = jnp.asarray(np.where(e < ROT_DIM // 2, -1.0, np.where(e < ROT_DIM, 1.0, 0.0)), F32).reshape(1, LANES)
    const = lambda shape: pl.BlockSpec(shape, lambda b, i: (0,) * len(shape))
    return pl.pallas_call(
        _proj_kernel,
        grid=(B, S // tm),
        in_specs=[pl.BlockSpec((1, tm, D), lambda b, i: (b, i, 0)),
                  pl.BlockSpec((1, tm, 1), lambda b, i: (b, i, 0)),
                  pl.BlockSpec((1, 1, D), lambda b, i: (b, 0, 0)),
                  pl.BlockSpec((1, 1, D), lambda b, i: (b, 0, 0)),
                  const((D, 3 * FOX_W)), const((FOX_HEADS * V_ROWS, D)), const((FOX_HEADS * V_ROWS, 1)),
                  const((D, 3 * DIL_W)), const((D, LANES)),
                  const((1, LANES)), const((1, LANES)), const((1, LANES))],
        out_specs=[pl.BlockSpec((1, tm, 3 * FOX_W), lambda b, i: (b, i, 0)),
                   pl.BlockSpec((1, FOX_HEADS * V_ROWS, tm), lambda b, i: (b, 0, i)),
                   pl.BlockSpec((1, tm, 3 * DIL_W), lambda b, i: (b, i, 0)),
                   pl.BlockSpec((1, tm, LANES), lambda b, i: (b, i, 0))],
        out_shape=[jax.ShapeDtypeStruct((B, S, 3 * FOX_W), BF16),
                   jax.ShapeDtypeStruct((B, FOX_HEADS * V_ROWS, S), BF16),
                   jax.ShapeDtypeStruct((B, S, 3 * DIL_W), F32),
                   jax.ShapeDtypeStruct((B, S, LANES), F32)],
        compiler_params=_params("arbitrary", "arbitrary"),
        name="proj",
    )(x, positions.reshape(B, S, 1), scale1.reshape(B, 1, D), shift1.reshape(B, 1, D),
      w_fox, w_vt, ones_row, w_dil, w_ff, b_f, invf, sgn)


def _cum_kernel(lf_ref, ck_ref, cq_ref):
    S = lf_ref.shape[1]
    r = lax.broadcasted_iota(jnp.int32, (LANES, LANES), 0)
    c = lax.broadcasted_iota(jnp.int32, (LANES, LANES), 1)
    tri = (c <= r).astype(F32)

    carry = jnp.zeros((1, LANES), F32)
    for j in range(S // LANES):
        rows = slice(j * LANES, (j + 1) * LANES)
        cum = jnp.dot(tri, lf_ref[0, rows, :], precision=HIGHEST, preferred_element_type=F32) + carry
        cum_t = cum.T
        for hp in range(FOX_HEADS // HEADS_PER_VREG):
            pair = slice(hp * HEADS_PER_VREG, (hp + 1) * HEADS_PER_VREG)
            ck_ref[0, hp, rows, :] = cum[:, pair]
            cq_ref[0, hp, :, rows] = cum_t[pair, :]
        carry = cum[LANES - 1:LANES, :]


def _cumulative_gate(log_f):
    B, S, _ = log_f.shape
    nhp = FOX_HEADS // HEADS_PER_VREG
    return pl.pallas_call(
        _cum_kernel,
        grid=(B,),
        in_specs=[pl.BlockSpec((1, S, LANES), lambda b: (b, 0, 0))],
        out_specs=[pl.BlockSpec((1, nhp, S, HEADS_PER_VREG), lambda b: (b, 0, 0, 0)),
                   pl.BlockSpec((1, nhp, HEADS_PER_VREG, S), lambda b: (b, 0, 0, 0))],
        out_shape=[jax.ShapeDtypeStruct((B, nhp, S, HEADS_PER_VREG), F32),
                   jax.ShapeDtypeStruct((B, nhp, HEADS_PER_VREG, S), F32)],
        compiler_params=_params("arbitrary"),
        name="cum",
    )(log_f)


def _fox_kernel(q_ref, k_ref, vt_ref, g_ref, ck_ref, cq_ref, o_ref, s_scr):
    S = q_ref.shape[1]
    T = FOX_TILE
    lane = lax.broadcasted_iota(jnp.int32, (T, LANES), 1)

    def head_queries(qoff):
        q = q_ref[0, pl.ds(qoff, T), :]
        return [jnp.where((lane // HEAD_DIM) == hh, q, jnp.zeros_like(q)) for hh in range(HEADS_PER_VREG)]

    def q_block(i, b0, odd):
        b1 = 1 - b0
        qoff = pl.multiple_of(i * T, T)
        qh = head_queries(qoff)
        cq = [cq_ref[0, 0, hh:hh + 1, pl.ds(qoff, T)] for hh in range(HEADS_PER_VREG)]

        def scores_into(buf, b, qh=qh):
            kb = k_ref[0, pl.ds(pl.multiple_of(b * T, T), T), :]
            for hh in range(HEADS_PER_VREG):
                s_scr[buf, hh] = lax.dot_general(kb, qh[hh], NT_DIMS, preferred_element_type=F32)

        def attend(buf, b, carry, masked):
            koff = pl.multiple_of(b * T, T)
            new = []
            for hh in range(HEADS_PER_VREG):
                m, acc = carry[hh]

                def scores(c0, rows):
                    s = s_scr[buf, hh, c0:c0 + rows, :] - ck_ref[0, 0, pl.ds(koff + c0, rows), hh:hh + 1]
                    if masked:
                        s = jnp.where(lax.broadcasted_iota(jnp.int32, (rows, T), 0) + c0
                                      <= lax.broadcasted_iota(jnp.int32, (rows, T), 1), s, NEG)
                    return s

                m_new = jnp.maximum(m, cq[hh] + jnp.max(scores(0, T), axis=0, keepdims=True))
                row = cq[hh] - m_new
                acc = jnp.exp2(m - m_new) * acc
                for c0 in range(0, T, MXU_DIM):
                    p = jnp.exp2((scores(c0, MXU_DIM) + row).astype(BF16))
                    vt = vt_ref[0, hh * V_ROWS:(hh + 1) * V_ROWS, pl.ds(koff + c0, MXU_DIM)]
                    acc = acc + jnp.dot(vt, p, preferred_element_type=F32)
                new.append((m_new, acc))
            return tuple(new)

        def pair(t, carry):
            scores_into(b1, 2 * t + 1)
            carry = attend(b0, 2 * t, carry, False)
            scores_into(b0, 2 * t + 2)
            return attend(b1, 2 * t + 1, carry, False)

        def start_next(slot):
            nxt = pl.multiple_of(jnp.minimum(i + 1, S // T - 1) * T, T)
            scores_into(slot, 0, head_queries(nxt))

        init = tuple((jnp.full((1, T), NEG, F32), jnp.zeros((V_ROWS, T), F32)) for _ in range(HEADS_PER_VREG))

        @pl.when(i == 0)
        def _():
            scores_into(b0, 0)

        carry = lax.fori_loop(0, i // 2, pair, init)
        if odd:
            scores_into(b1, i)
            carry = attend(b0, i - 1, carry, False)
            start_next(b0)
            carry = attend(b1, i, carry, True)
        else:
            start_next(b1)
            carry = attend(b0, i, carry, True)
        o_t = jnp.concatenate([acc[0:HEAD_DIM] * (1.0 / acc[HEAD_DIM:HEAD_DIM + 1]) for (_, acc) in carry], axis=0)
        gate = _sigmoid(g_ref[0, pl.ds(qoff, T), :].astype(F32))
        o_ref[0, pl.ds(qoff, T), :] = (o_t.T * gate).astype(BF16)
        return 0

    variants = [functools.partial(q_block, b0=b0, odd=odd) for b0, odd in ((0, False), (1, True), (1, False), (0, True))]
    lax.fori_loop(0, S // T, lambda i, _: lax.switch(i % 4, variants, i), 0)


def _fox_attention(pf, vt, ck, cq):
    B, S, _ = pf.shape
    nhp = FOX_W // LANES
    assert S % FOX_TILE == 0
    return pl.pallas_call(
        _fox_kernel,
        grid=(B, nhp),
        in_specs=[pl.BlockSpec((1, S, LANES), lambda b, h: (b, 0, h)),
                  pl.BlockSpec((1, S, LANES), lambda b, h: (b, 0, nhp + h)),
                  pl.BlockSpec((1, HEADS_PER_VREG * V_ROWS, S), lambda b, h: (b, h, 0)),
                  pl.BlockSpec((1, S, LANES), lambda b, h: (b, 0, 2 * nhp + h)),
                  pl.BlockSpec((1, 1, S, HEADS_PER_VREG), lambda b, h: (b, h, 0, 0)),
                  pl.BlockSpec((1, 1, HEADS_PER_VREG, S), lambda b, h: (b, h, 0, 0))],
        out_specs=pl.BlockSpec((1, S, LANES), lambda b, h: (b, 0, h)),
        out_shape=jax.ShapeDtypeStruct((B, S, FOX_W), BF16),
        scratch_shapes=[pltpu.VMEM((2, HEADS_PER_VREG, FOX_TILE, FOX_TILE), F32)],
        compiler_params=_params("arbitrary", "arbitrary"),
        name="fox",
    )(pf, pf, vt, pf, ck, cq)


def _dil_kernel(q_ref, k_ref, v_ref, o_ref, m_s, l_s, a_s, q4, k4, v4, m4, l4, a4):
    S = q_ref.shape[1]
    W = LANES
    lane = lax.broadcasted_iota(jnp.int32, (W, LANES), 1)
    head0 = lane < HEAD_DIM
    ri = lax.broadcasted_iota(jnp.int32, (2 * W, 2 * W), 0) % W
    ci = lax.broadcasted_iota(jnp.int32, (2 * W, 2 * W), 1)
    bias_rest = jnp.where((ci >= ri) & (ci <= ri + W), 0.0, NEG)
    bias_first = jnp.where((ci < W) & (ci <= ri), 0.0, NEG)
    ones = jnp.ones((2 * W, LANES), BF16)

    assert DILATED_PATTERNS == ((W, 1), (4 * W, 4), (16 * W, 16)) and S % (16 * W * 2) == 0
    assert (S // W) % DIL_UNROLL == 0
    L4 = S // 4

    def block_stats(q, kb, vb, first):
        qs = jnp.concatenate([jnp.where(head0, q, 0.0), jnp.where(head0, 0.0, q)], axis=0).astype(BF16)
        vb2 = jnp.concatenate([vb.astype(BF16), ones], axis=1)
        s = lax.dot_general(qs, kb.astype(BF16), NT_DIMS, preferred_element_type=F32)
        s = s + jnp.where(first, bias_first, bias_rest)
        m2 = jnp.max(s, axis=1, keepdims=True)
        out = jnp.dot(jnp.exp2((s - m2).astype(BF16)), vb2, preferred_element_type=F32)
        a_u = jnp.where(head0, out[0:W, 0:LANES], out[W:2 * W, 0:LANES])
        l_u = jnp.where(head0, out[0:W, LANES:2 * LANES], out[W:2 * W, LANES:2 * LANES])
        m_u = jnp.where(head0, m2[0:W], m2[W:2 * W])
        return m_u, l_u, a_u

    def merge(idx, stats):
        m_u, l_u, a_u = stats
        m_o = m4[idx, :]
        m_n = jnp.maximum(m_o, m_u)
        e_o = jnp.exp2(m_o - m_n)
        e_u = jnp.exp2(m_u - m_n)
        m4[idx, :] = m_n
        l4[idx, :] = l4[idx, :] * e_o + l_u * e_u
        a4[idx, :] = a4[idx, :] * e_o + a_u * e_u

    def for_blocks(unit):
        def group(t, _):
            for uu in range(DIL_UNROLL):
                unit(t * DIL_UNROLL + uu)
            return 0
        lax.fori_loop(0, S // W // DIL_UNROLL, group, 0)

    def unit1(u):
        qidx = pl.ds(pl.multiple_of(u * W, W), W)
        kidx = pl.ds(pl.multiple_of(jnp.maximum(u - 1, 0) * W, W), 2 * W)
        m_s[qidx, :], l_s[qidx, :], a_s[qidx, :] = block_stats(q_ref[0, qidx, :], k_ref[0, kidx, :],
                                                               v_ref[0, kidx, :], u == 0)
    for_blocks(unit1)

    def reorder(c, _):
        for r in range(4):
            src = pl.ds(c * (4 * W) + r, W, stride=4)
            dst = pl.ds(pl.multiple_of(r * L4 + c * W, W), W)
            q4[dst, :] = q_ref[0, src, :]
            k4[dst, :] = k_ref[0, src, :]
            v4[dst, :] = v_ref[0, src, :]
            m4[dst, :] = m_s[src, :]
            l4[dst, :] = l_s[src, :]
            a4[dst, :] = a_s[src, :]
        return 0
    lax.fori_loop(0, L4 // W, reorder, 0)

    nb4 = L4 // W

    def unit4(u):
        n = u % nb4
        base = (u // nb4) * L4
        qidx = pl.ds(pl.multiple_of(base + n * W, W), W)
        kidx = pl.ds(pl.multiple_of(base + jnp.maximum(n - 1, 0) * W, W), 2 * W)
        merge(qidx, block_stats(q4[qidx, :], k4[kidx, :], v4[kidx, :], n == 0))
    for_blocks(unit4)

    nb16 = S // (16 * W)

    def unit16(u):
        n = u % nb16
        r16 = u // nb16
        base = (r16 % 4) * L4 + r16 // 4
        qidx = pl.ds(base + n * (4 * W), W, stride=4)
        kidx = pl.ds(base + jnp.maximum(n - 1, 0) * (4 * W), 2 * W, stride=4)
        merge(qidx, block_stats(q4[qidx, :], k4[kidx, :], v4[kidx, :], n == 0))
    for_blocks(unit16)

    def finish(c, _):
        for r in range(4):
            src = pl.ds(pl.multiple_of(r * L4 + c * W, W), W)
            a_s[pl.ds(c * (4 * W) + r, W, stride=4), :] = a4[src, :] * (1.0 / l4[src, :])
        return 0
    lax.fori_loop(0, L4 // W, finish, 0)
    o_ref[0] = a_s[...].astype(BF16)


def _dilated_attention(pd):
    B, S, _ = pd.shape
    nhp = DIL_W // LANES
    return pl.pallas_call(
        _dil_kernel,
        grid=(B, nhp),
        in_specs=[pl.BlockSpec((1, S, LANES), lambda b, h: (b, 0, h)),
                  pl.BlockSpec((1, S, LANES), lambda b, h: (b, 0, nhp + h)),
                  pl.BlockSpec((1, S, LANES), lambda b, h: (b, 0, 2 * nhp + h))],
        out_specs=pl.BlockSpec((1, S, LANES), lambda b, h: (b, 0, h)),
        out_shape=jax.ShapeDtypeStruct((B, S, DIL_W), BF16),
        scratch_shapes=[pltpu.VMEM((S, LANES), F32)] * 9,
        compiler_params=_params("arbitrary", "arbitrary"),
        name="dil",
    )(pd, pd, pd)


def _out_kernel(of_ref, od_ref, x_ref, g1_ref, sc2_ref, sh2_ref, wo_ref, lng_ref, lnb_ref, wrh_ref, wrl_ref, br_ref,
                x1_ref, xs_ref, meta_ref, rt_ref, cnt_ref, h2_s, pos_v, pos_s, sem):
    tm = x_ref.shape[1]
    y = (jnp.dot(of_ref[0], wo_ref[0:FOX_W, :], preferred_element_type=F32)
         + jnp.dot(od_ref[0], wo_ref[FOX_W:FOX_W + DIL_W, :], preferred_element_type=F32))
    x1 = _layer_norm(DEEPNORM_ALPHA * x_ref[0] + g1_ref[0] * y, lng_ref[...], lnb_ref[...])
    x1_ref[0] = x1
    h2 = x1 * (1.0 + sc2_ref[0]) + sh2_ref[0]

    h_hi = h2.astype(BF16)
    h_lo = (h2 - h_hi.astype(F32)).astype(BF16)
    lg = (jnp.dot(h_hi, wrh_ref[...], preferred_element_type=F32)
          + jnp.dot(h_lo, wrh_ref[...], preferred_element_type=F32)
          + jnp.dot(h_hi, wrl_ref[...], preferred_element_type=F32)) + br_ref[...]
    lane = lax.broadcasted_iota(jnp.int32, (tm, LANES), 1)
    lanef = lane.astype(F32)
    far = float(LANES)
    is_g = lane < N_GROUPS
    gl = jnp.where(is_g, lg, NEG)
    gmax = jnp.max(gl, axis=1, keepdims=True)
    gidx = jnp.min(jnp.where(gl == gmax, lanef, far), axis=1, keepdims=True)
    p_g = 1.0 / jnp.sum(jnp.where(is_g, jnp.exp(gl - gmax), 0.0), axis=1, keepdims=True)
    e_lo = N_GROUPS + gidx * EXPERTS_PER_GROUP
    in_group = (lanef >= e_lo) & (lanef < e_lo + EXPERTS_PER_GROUP)
    el = jnp.where(in_group, lg, NEG)
    v1 = jnp.max(el, axis=1, keepdims=True)
    i1 = jnp.min(jnp.where(in_group & (el == v1), lanef, far), axis=1, keepdims=True)
    rest = in_group & (lanef != i1)
    el2 = jnp.where(rest, lg, NEG)
    v2 = jnp.max(el2, axis=1, keepdims=True)
    i2 = jnp.min(jnp.where(rest & (el2 == v2), lanef, far), axis=1, keepdims=True)
    e21 = jnp.exp(v2 - v1)
    w1 = p_g / (1.0 + e21)
    w2 = p_g * e21 / (1.0 + e21)

    pick1 = lanef == i1
    pick2 = lanef == i2
    onehot = jnp.where(pick1 | pick2, 1.0, 0.0)
    rr = lax.broadcasted_iota(jnp.int32, (tm, tm), 0)
    cc = lax.broadcasted_iota(jnp.int32, (tm, tm), 1)
    strict_lower = jnp.where(cc < rr, 1.0, 0.0).astype(BF16)
    before = jnp.dot(strict_lower, onehot.astype(BF16), preferred_element_type=F32)
    cnt = jnp.broadcast_to(jnp.sum(onehot, axis=0, keepdims=True), (SUBLANES, LANES))
    lane8 = lax.broadcasted_iota(jnp.int32, (SUBLANES, LANES), 1)
    incl = cnt
    for sh in (1, 2, 4, 8, 16, 32, 64):
        incl = incl + jnp.where(lane8 >= sh, pltpu.roll(incl, sh, 1), 0.0)
    start = before + (incl - cnt)[0:1]
    d1 = jnp.sum(jnp.where(pick1, start, 0.0), axis=1, keepdims=True)
    d2 = jnp.sum(jnp.where(pick2, start, 0.0), axis=1, keepdims=True)
    cnt_ref[...] = cnt

    packed = jnp.zeros((tm, LANES), F32)
    for k, val in enumerate((d1, d2, w1, w2)):
        packed = jnp.where(lane == k, val, packed)
    rt_ref[0] = packed
    meta = packed.T[0:SUBLANES, :]
    meta_ref[0] = meta
    pos_v[...] = (meta * SUBLANES).astype(jnp.int32)
    to_smem = pltpu.make_async_copy(pos_v, pos_s, sem)
    to_smem.start()
    for s in range(SUBLANES):
        h2_s[pl.ds(s, tm, stride=SUBLANES), :] = h2[:, s * LANES:(s + 1) * LANES]
    to_smem.wait()

    def place(t8, _):
        for u in range(SUBLANES):
            t = t8 * SUBLANES + u
            row = h2_s[pl.ds(pl.multiple_of(t * SUBLANES, SUBLANES), SUBLANES), :]
            for k in range(TOP_K):
                xs_ref[pl.ds(pl.multiple_of(pos_s[k, t], SUBLANES), SUBLANES), :] = row
        return 0

    lax.fori_loop(0, tm // SUBLANES, place, 0)


def _out_and_route(of, od, x, gate1, scale2, shift2, w_out, ln_g, ln_b, w_rg, b_rg, w_re, b_re, tm):
    B, S, D = x.shape
    n_r = N_GROUPS + N_EXPERTS
    w_r = jnp.pad(jnp.concatenate([w_rg, w_re], axis=1), ((0, 0), (0, LANES - n_r)))
    b_r = jnp.pad(jnp.concatenate([b_rg, b_re]), (0, LANES - n_r)).reshape(1, LANES)
    w_r_hi = w_r.astype(BF16)
    w_r_lo = (w_r - w_r_hi.astype(F32)).astype(BF16)
    nt = S // tm
    tile = lambda w: pl.BlockSpec((1, tm, w), lambda b, i: (b, i, 0))
    per_batch = pl.BlockSpec((1, 1, D), lambda b, i: (b, 0, 0))
    const = lambda shape: pl.BlockSpec(shape, lambda b, i: (0,) * len(shape))
    return pl.pallas_call(
        _out_kernel,
        grid=(B, S // tm),
        in_specs=[tile(FOX_W), tile(DIL_W), tile(D), per_batch, per_batch, per_batch,
                  const((D, D)), const((1, D)), const((1, D)), const((D, LANES)), const((D, LANES)),
                  const((1, LANES))],
        out_specs=[tile(D),
                   pl.BlockSpec((TOP_K * tm * SUBLANES, LANES), lambda b, i: (b * nt + i, 0)),
                   pl.BlockSpec((1, SUBLANES, tm), lambda b, i: (b * nt + i, 0, 0)),
                   tile(LANES),
                   pl.BlockSpec((SUBLANES, LANES), lambda b, i: (b * nt + i, 0))],
        out_shape=[jax.ShapeDtypeStruct((B, S, D), F32),
                   jax.ShapeDtypeStruct((TOP_K * B * S * SUBLANES, LANES), F32),
                   jax.ShapeDtypeStruct((B * nt, SUBLANES, tm), F32),
                   jax.ShapeDtypeStruct((B, S, LANES), F32),
                   jax.ShapeDtypeStruct((B * nt * SUBLANES, LANES), F32)],
        scratch_shapes=[pltpu.VMEM((tm * SUBLANES, LANES), F32), pltpu.VMEM((SUBLANES, tm), jnp.int32),
                        pltpu.SMEM((SUBLANES, tm), jnp.int32), pltpu.SemaphoreType.DMA(())],
        compiler_params=_params("arbitrary", "arbitrary"),
        name="out",
    )(of, od, x, gate1.reshape(B, 1, D), scale2.reshape(B, 1, D), shift2.reshape(B, 1, D),
      w_out.astype(BF16), ln_g.reshape(1, D), ln_b.reshape(1, D), w_r_hi, w_r_lo, b_r)


RUN_BITS = MOE_BLOCK.bit_length()
RUN_SMALL_BITS = 5
MAX_RUNS = 80
ST_EXPERT, ST_TILE, ST_OFF, ST_DONE = range(4)


def _copy_rows(src_ref, src_row, dst_ref, dst_row, n, sem, wait):
    def pieces(bits, pos):
        for bit in bits:
            size = 1 << bit

            @pl.when((n & size) != 0)
            def _(pos=pos, size=size):
                cp = pltpu.make_async_copy(
                    src_ref.at[pl.ds(pl.multiple_of((src_row + pos) * SUBLANES, SUBLANES), size * SUBLANES), :],
                    dst_ref.at[pl.ds(pl.multiple_of((dst_row + pos) * SUBLANES, SUBLANES), size * SUBLANES), :],
                    sem)
                if wait:
                    cp.wait()
                else:
                    cp.start()

            pos = pos + (n & size)

    small = 1 << RUN_SMALL_BITS
    pl.when(n >= small)(lambda: pieces(reversed(range(RUN_SMALL_BITS, RUN_BITS)), 0))
    pieces(reversed(range(RUN_SMALL_BITS)), n & ~(small - 1))


def _moe_kernel(blk_e_ref, nused_ref, cnt_ref, off_ref, tot_ref, x_ref, wg_ref, wu_ref, wd_ref, y_ref,
                xbuf0, xbuf1, ybuf0, ybuf1, mid_s, st, runs, sruns, gsem, ssem):
    del blk_e_ref
    i = pl.program_id(0)
    nused = nused_ref[0]
    D = xbuf0.shape[0] // MOE_BLOCK * LANES
    nk = D // MXU_DIM
    n_tiles = cnt_ref.shape[0] // N_EXPERTS
    rows_per_tile = x_ref.shape[0] // SUBLANES // n_tiles

    def gather_next(xdst, s):
        def next_expert(c):
            return c[0] + 1, 0, 0, 0

        def exhausted(c):
            return (c[0] < N_EXPERTS) & (c[3] >= tot_ref[jnp.minimum(c[0], N_EXPERTS - 1)])

        e, tile, off, done = lax.while_loop(
            exhausted, next_expert, (st[ST_EXPERT], st[ST_TILE], st[ST_OFF], st[ST_DONE]))
        ec = jnp.minimum(e, N_EXPERTS - 1)
        total = jnp.where(e < N_EXPERTS, tot_ref[ec], 0)

        def more(c):
            return (c[0] < MOE_BLOCK) & (c[4] < total)

        def take_run(c):
            filled, k, tile, off, done = c
            run = cnt_ref[tile * N_EXPERTS + ec]
            take = jnp.minimum(run - off, MOE_BLOCK - filled)
            src = tile * rows_per_tile + off_ref[tile * N_EXPERTS + ec] + off
            _copy_rows(x_ref, src, xdst, filled, take, gsem.at[s], wait=False)
            base = (s * MAX_RUNS + k) * 3
            runs[base] = src
            runs[base + 1] = filled
            runs[base + 2] = take
            run_done = off + take >= run
            return (filled + take, k + jnp.where(take > 0, 1, 0), jnp.where(run_done, tile + 1, tile),
                    jnp.where(run_done, 0, off + take), done + take)

        filled, k, tile, off, done = lax.while_loop(more, take_run, (0, 0, tile, off, done))
        _copy_rows(x_ref, 0, xdst, filled, jnp.where(total > 0, MOE_BLOCK - filled, 0), gsem.at[s], wait=False)
        runs[(2 * MAX_RUNS + s) * 3] = k
        st[ST_EXPERT] = e
        st[ST_TILE] = tile
        st[ST_OFF] = off
        st[ST_DONE] = done

    def wait_gather(xdst, s):
        pltpu.make_async_copy(x_ref.at[pl.ds(0, MOE_BLOCK * SUBLANES), :], xdst, gsem.at[s]).wait()

    def scatter_block(ysrc, s):
        def one(k, rows):
            base = (s * MAX_RUNS + k) * 3
            _copy_rows(ysrc, runs[base + 1], y_ref, runs[base], runs[base + 2], ssem.at[s], wait=False)
            return rows + runs[base + 2]

        sruns[s] = lax.fori_loop(0, runs[(2 * MAX_RUNS + s) * 3], one, 0)

    def wait_scatter(ysrc, s):
        n = pl.multiple_of(sruns[s] * SUBLANES, SUBLANES)
        pltpu.make_async_copy(ysrc.at[pl.ds(0, n), :], y_ref.at[pl.ds(0, n), :], ssem.at[s]).wait()

    def compute(xcur, ycur):
        xk = [jnp.concatenate([xcur[pl.ds(2 * j, MOE_BLOCK, stride=SUBLANES), :],
                               xcur[pl.ds(2 * j + 1, MOE_BLOCK, stride=SUBLANES), :]], axis=1).astype(BF16)
              for j in range(nk)]
        for c in range(D_EXPERT // MXU_DIM):
            cols = slice(c * MXU_DIM, (c + 1) * MXU_DIM)
            g = sum(jnp.dot(xk[j], wg_ref[0, MXU_DIM * j:MXU_DIM * (j + 1), cols].astype(BF16),
                            preferred_element_type=F32) for j in range(nk))
            u = sum(jnp.dot(xk[j], wu_ref[0, MXU_DIM * j:MXU_DIM * (j + 1), cols].astype(BF16),
                            preferred_element_type=F32) for j in range(nk))
            mid_s[:, cols] = (g * _sigmoid(g) * u).astype(BF16)
        mid = mid_s[...]
        for c in range(D // MXU_DIM):
            y = jnp.dot(mid, wd_ref[0, :, c * MXU_DIM:(c + 1) * MXU_DIM].astype(BF16), preferred_element_type=F32)
            ycur[pl.ds(2 * c, MOE_BLOCK, stride=SUBLANES), :] = y[:, 0:LANES]
            ycur[pl.ds(2 * c + 1, MOE_BLOCK, stride=SUBLANES), :] = y[:, LANES:2 * LANES]

    def step(s, xcur, xnxt, ycur, yprv):
        o = 1 - s
        wait_gather(xcur, s)

        @pl.when(i + 1 < nused)
        def _():
            gather_next(xnxt, o)

        @pl.when(i >= 2)
        def _():
            wait_scatter(ycur, s)

        compute(xcur, ycur)
        scatter_block(ycur, s)

        @pl.when(i + 1 == nused)
        def _():
            @pl.when(i >= 1)
            def _():
                wait_scatter(yprv, o)

            wait_scatter(ycur, s)

    @pl.when(i == 0)
    def _():
        for f in range(4):
            st[f] = 0
        gather_next(xbuf0, 0)

    @pl.when((i < nused) & (i % 2 == 0))
    def _():
        step(0, xbuf0, xbuf1, ybuf0, ybuf1)

    @pl.when((i < nused) & (i % 2 == 1))
    def _():
        step(1, xbuf1, xbuf0, ybuf1, ybuf0)


def _expert_mlp(x_sorted, cnt, off, tot, blk_e, nused, w_gate, w_up, w_down):
    D = w_gate.shape[1]
    nblk = blk_e.shape[0]
    assert cnt.shape[0] // N_EXPERTS + 2 <= MAX_RUNS
    wmap = lambda i, be, nu, c, o, t: (be[i], 0, 0)
    return pl.pallas_call(
        _moe_kernel,
        grid_spec=pltpu.PrefetchScalarGridSpec(
            num_scalar_prefetch=5,
            grid=(nblk,),
            in_specs=[pl.BlockSpec(memory_space=pl.ANY),
                      pl.BlockSpec((1, D, D_EXPERT), wmap),
                      pl.BlockSpec((1, D, D_EXPERT), wmap),
                      pl.BlockSpec((1, D_EXPERT, D), wmap)],
            out_specs=pl.BlockSpec(memory_space=pl.ANY),
            scratch_shapes=[pltpu.VMEM((MOE_BLOCK * SUBLANES, LANES), F32)] * 4
            + [pltpu.VMEM((MOE_BLOCK, D_EXPERT), BF16),
               pltpu.SMEM((4,), jnp.int32),
               pltpu.SMEM(((2 * MAX_RUNS + 2) * 3,), jnp.int32), pltpu.SMEM((2,), jnp.int32),
               pltpu.SemaphoreType.DMA((2,)), pltpu.SemaphoreType.DMA((2,))]),
        out_shape=jax.ShapeDtypeStruct(x_sorted.shape, F32),
        compiler_params=_params("arbitrary"),
        name="moe",
    )(blk_e, nused, cnt, off, tot, x_sorted, w_gate, w_up, w_down)


def _final_kernel(pos_ref, rt_ref, y_ref, x1_ref, g2_ref, lng_ref, lnb_ref, o_ref, pick_s):
    tm = x1_ref.shape[1]

    def unsort(t8, _):
        for u in range(SUBLANES):
            t = t8 * SUBLANES + u
            for k in range(TOP_K):
                pick_s[k, pl.ds(pl.multiple_of(t * SUBLANES, SUBLANES), SUBLANES), :] = (
                    y_ref[pl.ds(pl.multiple_of(pos_ref[k, t], SUBLANES), SUBLANES), :])
        return 0

    lax.fori_loop(0, tm // SUBLANES, unsort, 0)
    wts = rt_ref[0]
    y = jnp.concatenate([wts[:, TOP_K:TOP_K + 1] * pick_s[0, pl.ds(s, tm, stride=SUBLANES), :]
                         + wts[:, TOP_K + 1:TOP_K + 2] * pick_s[1, pl.ds(s, tm, stride=SUBLANES), :]
                         for s in range(SUBLANES)], axis=1)
    o_ref[0] = _layer_norm(DEEPNORM_ALPHA * x1_ref[0] + g2_ref[0] * y, lng_ref[...], lnb_ref[...])


def _combine_and_norm(y_sorted, meta, route, x1, gate2, ln_g, ln_b, tm):
    B, S, D = x1.shape
    nt = S // tm
    pos = (meta * SUBLANES).astype(jnp.int32).reshape(B * nt * SUBLANES, tm)
    return pl.pallas_call(
        _final_kernel,
        grid=(B, nt),
        in_specs=[pl.BlockSpec((SUBLANES, tm), lambda b, i: (b * nt + i, 0), memory_space=pltpu.SMEM),
                  pl.BlockSpec((1, tm, LANES), lambda b, i: (b, i, 0)),
                  pl.BlockSpec((TOP_K * tm * SUBLANES, LANES), lambda b, i: (b * nt + i, 0)),
                  pl.BlockSpec((1, tm, D), lambda b, i: (b, i, 0)),
                  pl.BlockSpec((1, 1, D), lambda b, i: (b, 0, 0)),
                  pl.BlockSpec((1, D), lambda b, i: (0, 0)),
                  pl.BlockSpec((1, D), lambda b, i: (0, 0))],
        out_specs=pl.BlockSpec((1, tm, D), lambda b, i: (b, i, 0)),
        out_shape=jax.ShapeDtypeStruct((B, S, D), F32),
        scratch_shapes=[pltpu.VMEM((TOP_K, tm * SUBLANES, LANES), F32)],
        compiler_params=_params("arbitrary", "arbitrary"),
        name="final",
    )(pos, route, y_sorted, x1, gate2.reshape(B, 1, D), ln_g.reshape(1, D), ln_b.reshape(1, D))


def kernel(x, c, positions, w_ada, b_ada, w_in, b_forget, w_out, ln1_g, ln1_b, w_router_group, b_router_group,
           w_router_expert, b_router_expert, w_up, w_gate, w_down, ln2_g, ln2_b):
    B, S, D = x.shape
    N = B * S
    assert D == FOX_W + DIL_W and S % 2048 == 0
    mod = _modulation(c, w_ada, b_ada)
    shift1, scale1, gate1, shift2, scale2, gate2 = jnp.split(mod, 6, axis=-1)

    pf, vt, pd, log_f = _projection(x, positions, scale1, shift1, w_in, b_forget, tm=512)
    ck, cq = _cumulative_gate(log_f)
    of = _fox_attention(pf, vt, ck, cq)
    od = _dilated_attention(pd)

    tm = 512
    x1, x_sorted, meta, route, tile_cnt = _out_and_route(
        of, od, x, gate1, scale2, shift2, w_out, ln1_g, ln1_b,
        w_router_group, b_router_group, w_router_expert, b_router_expert, tm=tm)

    cnt = tile_cnt[::SUBLANES, N_GROUPS:N_GROUPS + N_EXPERTS].astype(jnp.int32)
    off = jnp.cumsum(cnt, axis=1) - cnt
    tot = jnp.sum(cnt, axis=0)
    pend = jnp.cumsum((tot + MOE_BLOCK - 1) // MOE_BLOCK * MOE_BLOCK)
    nblk = (N * TOP_K) // MOE_BLOCK + N_EXPERTS
    nused = pend[-1:] // MOE_BLOCK
    blk_start = jnp.arange(nblk, dtype=jnp.int32) * MOE_BLOCK
    blk_e = jnp.sum((pend[None, :] <= jnp.minimum(blk_start, pend[-1] - 1)[:, None]).astype(jnp.int32), axis=1)

    y_sorted = _expert_mlp(x_sorted, cnt.reshape(-1), off.reshape(-1), tot, blk_e, nused, w_gate, w_up, w_down)
    return _combine_and_norm(y_sorted, meta, route, x1, gate2, ln2_g, ln2_b, tm=tm)
```
